```python
import math
import jax, jax.numpy as jnp
from jax import lax
import numpy as np

D_MODEL = 1024
BATCH = 16
SEQ = 4096
DEPTH = 1

MIX_WIDTH = D_MODEL
DA_HEADS = 4
DA_QK_DIM = 64
DA_V_DIM = 2 * DA_QK_DIM
DA_WIDTH = DA_HEADS * DA_V_DIM
Q_BLOCK = 128
ML_HEADS = 4
ML_QK_DIM = 64
ML_V_DIM = 128
ML_WIDTH = ML_HEADS * ML_V_DIM
ML_CONV = 4
ML_CHUNK = 64
FORGET_BIAS_LO = 3.0
FORGET_BIAS_HI = 6.0
REL_BUCKETS = 32
REL_MAX_DIST = 128
N_EXPERTS = 32
TOP_K = 4
D_FF = D_MODEL
SWIGLU_LIMIT = 7.0
SWIGLU_ALPHA = 1.702
MOE_BLOCK = 128
NORM_EPS = 1e-5
DA_Q_COLS = DA_HEADS * 2 * DA_QK_DIM
DA_K_COLS = DA_HEADS * 2 * DA_QK_DIM
DA_V_COLS = DA_WIDTH
ML_Q_COLS = ML_HEADS * ML_QK_DIM
ML_K_COLS = ML_HEADS * ML_QK_DIM
ML_V_COLS = ML_WIDTH
ML_O_COLS = ML_WIDTH
ML_GATE_COLS = 2 * ML_HEADS
IN_COLS = DA_Q_COLS + DA_K_COLS + DA_V_COLS + ML_Q_COLS + ML_K_COLS + ML_V_COLS + ML_O_COLS + ML_GATE_COLS

kernel_name = "hymba_diffattn_mlstm_moe_block"


def rms_norm(x, g):
    xf = x.astype(jnp.float32)
    y = xf * lax.rsqrt(jnp.mean(xf * xf, axis=-1, keepdims=True) + NORM_EPS)
    return (y * g.astype(jnp.float32)).astype(x.dtype)


def head_rms_norm(x, g):
    H, d = x.shape[-2], x.shape[-1]
    return rms_norm(x, g.reshape(H, d))


def lambda_init_fn(layer):
    return 0.8 - 0.6 * math.exp(-0.3 * layer)


def t5_bucket(rel):
    n = jnp.maximum(rel, 0)
    max_exact = REL_BUCKETS // 2
    nf = jnp.maximum(n, 1).astype(jnp.float32)
    large = max_exact + (jnp.log(nf / max_exact) / math.log(REL_MAX_DIST / max_exact)
                         * (REL_BUCKETS - max_exact)).astype(jnp.int32)
    large = jnp.minimum(large, REL_BUCKETS - 1)
    return jnp.where(n < max_exact, n, large)


def diff_attention(q, k, v, table, lam):
    S = q.shape[1]
    q = jnp.transpose(q, (0, 2, 3, 1, 4))
    k = jnp.transpose(k, (0, 2, 3, 1, 4))
    v = jnp.transpose(v, (0, 2, 1, 3))
    scale = DA_QK_DIM ** -0.5
    table = table.astype(jnp.float32)
    outs = []
    for blk in range(S // Q_BLOCK):
        qs = blk * Q_BLOCK
        end = qs + Q_BLOCK
        logits = jnp.einsum('bhmqd,bhmkd->bhmqk', q[:, :, :, qs:end], k[:, :, :, :end]).astype(jnp.float32) * scale
        rel = (qs + jnp.arange(Q_BLOCK))[:, None] - jnp.arange(end)[None, :]
        bias = jnp.transpose(table[t5_bucket(rel)], (2, 0, 1))
        logits = jnp.where(rel >= 0, logits + bias[None, :, None], -jnp.inf)
        p = jax.nn.softmax(logits, axis=-1)
        a = p[:, :, 0] - lam * p[:, :, 1]
        outs.append(jnp.einsum('bhqk,bhkd->bhqd', a.astype(v.dtype), v[:, :, :end]))
    out = jnp.concatenate(outs, axis=2)
    return jnp.transpose(out, (0, 2, 1, 3))


def causal_conv(x, w, b):
    K = w.shape[0]
    S = x.shape[1]
    xp = jnp.pad(x, ((0, 0), (K - 1, 0), (0, 0)))
    y = b
    for tap in range(K):
        y = y + xp[:, tap:tap + S] * w[tap]
    return y


def mlstm_chunkwise(q, k, v, log_i, log_f):
    B, H, S, dk = q.shape
    dv = v.shape[-1]
    L = ML_CHUNK
    nc = S // L

    def chunks(a):
        return jnp.moveaxis(a.reshape(a.shape[:2] + (nc, L) + a.shape[3:]), 2, 0)

    causal = jnp.tril(jnp.ones((L, L), dtype=bool))

    def step(carry, inp):
        C, n, m = carry
        qc, kc, vc, li, lf = inp
        b = jnp.cumsum(lf, axis=-1)
        dlog = jnp.where(causal, b[..., :, None] - b[..., None, :] + li[..., None, :], -jnp.inf)
        inter = b + m[..., None]
        m_t = jnp.maximum(jnp.max(dlog, axis=-1), inter)
        w_intra = jnp.exp(dlog - m_t[..., None])
        w_inter = jnp.exp(inter - m_t)
        s = jnp.einsum('bhtd,bhjd->bhtj', qc, kc) * w_intra
        num = jnp.einsum('bhtj,bhje->bhte', s, vc) + w_inter[..., None] * jnp.einsum('bhtd,bhde->bhte', qc, C)
        den = jnp.sum(s, axis=-1) + w_inter * jnp.einsum('bhtd,bhd->bht', qc, n)
        h = num / jnp.maximum(jnp.abs(den), jnp.exp(-m_t))[..., None]
        g = b[..., -1]
        wlog = g[..., None] - b + li
        m_new = jnp.maximum(g + m, jnp.max(wlog, axis=-1))
        decay = jnp.exp(g + m - m_new)
        wk = jnp.exp(wlog - m_new[..., None])
        C_new = decay[..., None, None] * C + jnp.einsum('bhj,bhjd,bhje->bhde', wk, kc, vc)
        n_new = decay[..., None] * n + jnp.einsum('bhj,bhjd->bhd', wk, kc)
        return (C_new, n_new, m_new), h

    init = (jnp.zeros((B, H, dk, dv), jnp.float32), jnp.zeros((B, H, dk), jnp.float32),
            jnp.zeros((B, H), jnp.float32))
    _, hs = lax.scan(step, init, (chunks(q), chunks(k), chunks(v), chunks(log_i), chunks(log_f)))
    return jnp.moveaxis(hs, 0, 2).reshape(B, H, S, dv)


def moe(h, router_w, router_b, w_gu, b_gu, w_down, b_down):
    B, S, D = h.shape
    T = B * S
    xt = h.reshape(T, D)
    logits = (xt @ router_w + router_b).astype(jnp.float32)
    top_val, top_idx = lax.top_k(logits, TOP_K)
    gates = jax.nn.softmax(top_val, axis=-1)
    P = T * TOP_K
    e_flat = top_idx.reshape(P)
    tok_flat = jnp.repeat(jnp.arange(T, dtype=jnp.int32), TOP_K)
    g_flat = gates.reshape(P)
    order = jnp.argsort(e_flat)
    e_sorted = e_flat[order]
    counts = jnp.bincount(e_flat, length=N_EXPERTS)
    padded = (counts + MOE_BLOCK - 1) // MOE_BLOCK * MOE_BLOCK
    start = jnp.cumsum(counts) - counts
    pstart = jnp.cumsum(padded) - padded
    dest = pstart[e_sorted] + (jnp.arange(P) - start[e_sorted])
    n_rows = (P + MOE_BLOCK - 1) // MOE_BLOCK * MOE_BLOCK + N_EXPERTS * MOE_BLOCK
    n_blocks = n_rows // MOE_BLOCK
    row_tok = jnp.zeros((n_rows,), jnp.int32).at[dest].set(tok_flat[order])
    row_gate = jnp.zeros((n_rows,), jnp.float32).at[dest].set(g_flat[order])
    block_expert = jnp.minimum(
        jnp.searchsorted(jnp.cumsum(padded), jnp.arange(n_blocks) * MOE_BLOCK, side='right'),
        N_EXPERTS - 1)

    def expert_block(args):
        rows, e = args
        xb = xt[rows]
        gu = xb @ w_gu[e] + b_gu[e]
        glu = jnp.minimum(gu[:, :D_FF], SWIGLU_LIMIT)
        lin = jnp.clip(gu[:, D_FF:], -SWIGLU_LIMIT, SWIGLU_LIMIT)
        act = glu * jax.nn.sigmoid(SWIGLU_ALPHA * glu) * (lin + 1)
        return act @ w_down[e] + b_down[e]

    ys = lax.map(expert_block, (row_tok.reshape(n_blocks, MOE_BLOCK), block_expert)).reshape(n_rows, D)
    y = jax.ops.segment_sum(ys * row_gate[:, None].astype(ys.dtype), row_tok, num_segments=T)
    return y.reshape(B, S, D)


def setup_inputs(seed: int = 0) -> dict:
    key = jax.random.key(seed)
    ks = jax.random.split(key, 24)

    def nrm(k, shape, s):
        return s * jax.random.normal(k, shape, jnp.float32)

    forget_b = jnp.linspace(FORGET_BIAS_LO, FORGET_BIAS_HI, ML_HEADS, dtype=jnp.float32)[None] + nrm(ks[10], (DEPTH, ML_HEADS), 0.1)
    input_b = nrm(ks[11], (DEPTH, ML_HEADS), 0.1)
    return {
        "x": jax.random.normal(ks[0], (BATCH, SEQ, D_MODEL), jnp.float32),
        "norm_attn_g": 1.0 + nrm(ks[1], (DEPTH, D_MODEL), 0.02),
        "w_in": nrm(ks[2], (DEPTH, D_MODEL, IN_COLS), D_MODEL ** -0.5),
        "rel_bias_table": nrm(ks[3], (REL_BUCKETS, DA_HEADS), 0.5),
        "lambda_q1": nrm(ks[4], (DEPTH, DA_QK_DIM), 0.1),
        "lambda_k1": nrm(ks[5], (DEPTH, DA_QK_DIM), 0.1),
        "lambda_q2": nrm(ks[6], (DEPTH, DA_QK_DIM), 0.1),
        "lambda_k2": nrm(ks[7], (DEPTH, DA_QK_DIM), 0.1),
        "diff_norm_g": 1.0 + nrm(ks[8], (DEPTH, DA_WIDTH), 0.02),
        "mlstm_conv_w": nrm(ks[9], (DEPTH, ML_CONV, ML_Q_COLS + ML_K_COLS), 0.5),
        "mlstm_conv_b": nrm(ks[12], (DEPTH, ML_Q_COLS + ML_K_COLS), 0.02),
        "mlstm_gate_b": jnp.concatenate([input_b, forget_b], axis=-1),
        "mlstm_o_b": nrm(ks[13], (DEPTH, ML_WIDTH), 0.1),
        "mlstm_norm_g": 1.0 + nrm(ks[14], (DEPTH, ML_WIDTH), 0.02),
        "w_out": nrm(ks[15], (DEPTH, MIX_WIDTH, D_MODEL), MIX_WIDTH ** -0.5),
        "norm_ffn_g": 1.0 + nrm(ks[16], (DEPTH, D_MODEL), 0.02),
        "router_w": nrm(ks[17], (DEPTH, D_MODEL, N_EXPERTS), D_MODEL ** -0.5),
        "router_b": nrm(ks[18], (DEPTH, N_EXPERTS), 0.01),
        "expert_w_gu": nrm(ks[19], (DEPTH, N_EXPERTS, D_MODEL, 2 * D_FF), D_MODEL ** -0.5),
        "expert_b_gu": nrm(ks[20], (DEPTH, N_EXPERTS, 2 * D_FF), 0.01),
        "expert_w_down": nrm(ks[21], (DEPTH, N_EXPERTS, D_FF, D_MODEL), D_FF ** -0.5),
        "expert_b_down": nrm(ks[22], (DEPTH, N_EXPERTS, D_MODEL), 0.01),
        "final_norm_g": 1.0 + nrm(ks[23], (D_MODEL,), 0.02),
    }


def reference(x, norm_attn_g, w_in, rel_bias_table, lambda_q1, lambda_k1, lambda_q2, lambda_k2,
              diff_norm_g, mlstm_conv_w, mlstm_conv_b, mlstm_gate_b, mlstm_o_b, mlstm_norm_g,
              w_out, norm_ffn_g, router_w, router_b, expert_w_gu, expert_b_gu, expert_w_down,
              expert_b_down, final_norm_g):
    B, S, _ = x.shape
    split_at = list(np.cumsum([DA_Q_COLS, DA_K_COLS, DA_V_COLS, ML_Q_COLS + ML_K_COLS, ML_V_COLS, ML_O_COLS]))
    for l in range(DEPTH):
        h = rms_norm(x, norm_attn_g[l])
        proj = h @ w_in[l]
        da_q, da_k, da_v, ml_qk, ml_v, ml_o, ml_g = jnp.split(proj, split_at, axis=-1)

        lam_init = lambda_init_fn(l)
        lam = (jnp.exp(jnp.sum(lambda_q1[l] * lambda_k1[l])) - jnp.exp(jnp.sum(lambda_q2[l] * lambda_k2[l]))
               + lam_init).astype(jnp.float32)
        a_out = diff_attention(da_q.reshape(B, S, DA_HEADS, 2, DA_QK_DIM),
                               da_k.reshape(B, S, DA_HEADS, 2, DA_QK_DIM),
                               da_v.reshape(B, S, DA_HEADS, DA_V_DIM), rel_bias_table, lam)
        a_out = head_rms_norm(a_out, diff_norm_g[l]) * (1.0 - lam_init)

        qk = jax.nn.silu(causal_conv(ml_qk, mlstm_conv_w[l], mlstm_conv_b[l]))
        ml_q = qk[..., :ML_Q_COLS].reshape(B, S, ML_HEADS, ML_QK_DIM).transpose(0, 2, 1, 3).astype(jnp.float32)
        ml_k = qk[..., ML_Q_COLS:].reshape(B, S, ML_HEADS, ML_QK_DIM).transpose(0, 2, 1, 3).astype(jnp.float32)
        ml_k = ml_k * (ML_QK_DIM ** -0.5)
        ml_vh = ml_v.reshape(B, S, ML_HEADS, ML_V_DIM).transpose(0, 2, 1, 3).astype(jnp.float32)
        gate_pre = (ml_g + mlstm_gate_b[l]).astype(jnp.float32)
        log_i = jnp.transpose(gate_pre[..., :ML_HEADS], (0, 2, 1))
        log_f = jnp.transpose(jax.nn.log_sigmoid(gate_pre[..., ML_HEADS:]), (0, 2, 1))
        m_out = mlstm_chunkwise(ml_q, ml_k, ml_vh, log_i, log_f)
        m_out = jnp.transpose(m_out, (0, 2, 1, 3)).astype(x.dtype)
        o_gate = jax.nn.sigmoid(ml_o + mlstm_o_b[l]).reshape(B, S, ML_HEADS, ML_V_DIM)
        m_out = head_rms_norm(m_out, mlstm_norm_g[l]) * o_gate

        mixed = jnp.concatenate([a_out.reshape(B, S, DA_WIDTH), m_out.reshape(B, S, ML_WIDTH)], axis=-1)
        x = x + mixed @ w_out[l]

        x = x + moe(rms_norm(x, norm_ffn_g[l]), router_w[l], router_b[l], expert_w_gu[l],
                    expert_b_gu[l], expert_w_down[l], expert_b_down[l])
    return rms_norm(x, final_norm_g)
```

```python
import functools
import math

import numpy as np
import jax
import jax.numpy as jnp
from jax import lax
from jax.experimental import pallas as pl
from jax.experimental.pallas import tpu as pltpu

F32 = jnp.float32
BF16 = jnp.bfloat16

D_MODEL = 1024
NORM_EPS = 1e-5
DA_HEADS = 4
DA_QK_DIM = 64
DA_V_DIM = 128
DA_WIDTH = DA_HEADS * DA_V_DIM
REL_BUCKETS = 32
REL_MAX_DIST = 128
ML_HEADS = 4
ML_QK_DIM = 64
ML_V_DIM = 128
ML_WIDTH = ML_HEADS * ML_V_DIM
ML_QK_COLS = 2 * ML_HEADS * ML_QK_DIM
ML_CONV = 4
N_EXPERTS = 32
TOP_K = 4
D_FF = D_MODEL
SWIGLU_LIMIT = 7.0
SWIGLU_ALPHA = 1.702

N_MAIN = 3 * DA_WIDTH + ML_QK_COLS + 2 * ML_WIDTH
LANES = 128
SUBLANES = 8
MASK_NEG = -1e30

TM_PROJ = 512
TQ = 256
ML_CHUNK = 256
MOE_ROWS = 512
TM_COMBINE = 256
VMEM_LIMIT = 56 * 1024 * 1024


def _bucket_lower_bounds():
    n = np.arange(0, 4 * REL_MAX_DIST)
    max_exact = REL_BUCKETS // 2
    nf = np.maximum(n, 1).astype(np.float32)
    large = max_exact + (np.log(nf / np.float32(max_exact)) / np.float32(math.log(REL_MAX_DIST / max_exact))
                         * np.float32(REL_BUCKETS - max_exact)).astype(np.int32)
    large = np.minimum(large, REL_BUCKETS - 1)
    bucket = np.where(n < max_exact, n, large)
    return [int(np.argmax(bucket >= b)) for b in range(REL_BUCKETS)]


BUCKET_LO = _bucket_lower_bounds()
assert BUCKET_LO[-1] <= TQ, "keys two blocks away must all sit in the last bucket"


def _rms(x, g):
    return x * lax.rsqrt(jnp.mean(x * x, axis=-1, keepdims=True) + NORM_EPS) * g


def _log_sigmoid(x):
    return jnp.minimum(x, 0.0) - jnp.log1p(jnp.exp(-jnp.abs(x)))


def _inproj_kernel(x_ref, g_ref, w_ref, wg_ref, o_ref, og_ref):
    hb = _rms(x_ref[...], g_ref[...]).astype(BF16)
    for n in range(N_MAIN // 512):
        cols = slice(n * 512, (n + 1) * 512)
        o_ref[:, cols] = jnp.dot(hb, w_ref[:, cols], preferred_element_type=F32).astype(BF16)
    og_ref[...] = jnp.dot(hb, wg_ref[...], preferred_element_type=F32)


def _inproj(x2, g, w_main, w_gate):
    T = x2.shape[0]
    return pl.pallas_call(
        _inproj_kernel,
        grid=(T // TM_PROJ,),
        in_specs=[
            pl.BlockSpec((TM_PROJ, D_MODEL), lambda i: (i, 0)),
            pl.BlockSpec((1, D_MODEL), lambda i: (0, 0)),
            pl.BlockSpec((D_MODEL, N_MAIN), lambda i: (0, 0)),
            pl.BlockSpec((D_MODEL, 2 * LANES), lambda i: (0, 0)),
        ],
        out_specs=[
            pl.BlockSpec((TM_PROJ, N_MAIN), lambda i: (i, 0)),
            pl.BlockSpec((TM_PROJ, 2 * LANES), lambda i: (i, 0)),
        ],
        out_shape=[jax.ShapeDtypeStruct((T, N_MAIN), BF16), jax.ShapeDtypeStruct((T, 2 * LANES), F32)],
        compiler_params=pltpu.CompilerParams(dimension_semantics=("parallel",), vmem_limit_bytes=VMEM_LIMIT),
        name="inproj",
    )(x2, g, w_main, w_gate)


def _relbias_kernel(tbl_ref, o_ref):
    h = pl.program_id(0)
    r = lax.broadcasted_iota(jnp.int32, (2 * TQ, TQ), 0) & (TQ - 1)
    c = lax.broadcasted_iota(jnp.int32, (2 * TQ, TQ), 1)
    far = tbl_ref[REL_BUCKETS - 1, h]
    for w in range(2):
        rel = r - c + w * TQ
        val = jnp.full((2 * TQ, TQ), tbl_ref[0, h], F32)
        for b in range(1, REL_BUCKETS):
            val = jnp.where(rel >= BUCKET_LO[b], tbl_ref[b, h], val)
        val = val - far
        if w == 0:
            val = jnp.where(rel >= 0, val, MASK_NEG)
        o_ref[0, w] = val


def _relbias(table):
    return pl.pallas_call(
        _relbias_kernel,
        grid=(DA_HEADS,),
        in_specs=[pl.BlockSpec(memory_space=pltpu.SMEM)],
        out_specs=pl.BlockSpec((1, 2, 2 * TQ, TQ), lambda h: (h, 0, 0, 0)),
        out_shape=jax.ShapeDtypeStruct((DA_HEADS, 2, 2 * TQ, TQ), F32),
        compiler_params=pltpu.CompilerParams(dimension_semantics=("parallel",)),
        name="relbias",
    )(table)


def _attn_kernel(lam_init, lam_ref, q_ref, k_ref, v_ref, bias_ref, g_ref, o_ref, acc_ref, m_ref, l_ref):
    i = pl.program_id(2)
    q = q_ref[...] * jnp.asarray(DA_QK_DIM ** -0.5, BF16)
    lane = lax.broadcasted_iota(jnp.int32, q.shape, 1)
    zero = jnp.zeros_like(q)
    qq = jnp.concatenate([jnp.where(lane < DA_QK_DIM, q, zero), jnp.where(lane >= DA_QK_DIM, q, zero)], axis=0)

    def scores(j):
        kb = k_ref[pl.ds(pl.multiple_of(j * TQ, TQ), TQ), :]
        return lax.dot_general(qq, kb, (((1,), (1,)), ((), ())), preferred_element_type=F32)

    def values(j):
        return v_ref[pl.ds(pl.multiple_of(j * TQ, TQ), TQ), :]

    s = scores(i) + bias_ref[0, 0]
    m = jnp.max(s, axis=-1, keepdims=True)
    p = jnp.exp(s - m)
    m_ref[...] = m
    l_ref[...] = jnp.sum(p, axis=-1, keepdims=True)
    acc_ref[...] = jnp.dot(p.astype(BF16), values(i), preferred_element_type=F32)

    def update(s, vb):
        m_old = m_ref[...]
        m_new = jnp.maximum(m_old, jnp.max(s, axis=-1, keepdims=True))
        alpha = jnp.exp(m_old - m_new)
        p = jnp.exp(s - m_new)
        l_ref[...] = alpha * l_ref[...] + jnp.sum(p, axis=-1, keepdims=True)
        acc_ref[...] = alpha * acc_ref[...] + jnp.dot(p.astype(BF16), vb, preferred_element_type=F32)
        m_ref[...] = m_new

    @pl.when(i >= 1)
    def _():
        update(scores(i - 1) + bias_ref[0, 1], values(i - 1))

    def far_block(j, carry):
        update(scores(j), values(j))
        return carry

    lax.fori_loop(0, jnp.maximum(i - 1, 0), far_block, 0)

    lam_v = lam_ref[...]
    lam = (jnp.exp(jnp.sum(lam_v[0:1] * lam_v[1:2], axis=-1, keepdims=True))
           - jnp.exp(jnp.sum(lam_v[2:3] * lam_v[3:4], axis=-1, keepdims=True)) + lam_init)
    o = acc_ref[...] / l_ref[...]
    out = o[:TQ] - lam * o[TQ:]
    o_ref[...] = (_rms(out, g_ref[0]) * (1.0 - lam_init)).astype(BF16)


def _attention(proj, bias, lam_vecs, norm_g, B, S, lam_init):
    nq = S // TQ
    q_off, k_off, v_off = 0, DA_HEADS, 2 * DA_HEADS
    return pl.pallas_call(
        functools.partial(_attn_kernel, lam_init),
        grid=(B, DA_HEADS, nq),
        in_specs=[
            pl.BlockSpec((4, DA_QK_DIM), lambda b, h, i: (0, 0)),
            pl.BlockSpec((TQ, LANES), lambda b, h, i: (b * nq + i, q_off + h)),
            pl.BlockSpec((S, LANES), lambda b, h, i: (b, k_off + h)),
            pl.BlockSpec((S, LANES), lambda b, h, i: (b, v_off + h)),
            pl.BlockSpec((1, 2, 2 * TQ, TQ), lambda b, h, i: (h, 0, 0, 0)),
            pl.BlockSpec((1, 1, LANES), lambda b, h, i: (h, 0, 0)),
        ],
        out_specs=pl.BlockSpec((TQ, LANES), lambda b, h, i: (b * nq + i, h)),
        out_shape=jax.ShapeDtypeStruct((B * S, DA_WIDTH), BF16),
        scratch_shapes=[
            pltpu.VMEM((2 * TQ, DA_V_DIM), F32),
            pltpu.VMEM((2 * TQ, 1), F32),
            pltpu.VMEM((2 * TQ, 1), F32),
        ],
        compiler_params=pltpu.CompilerParams(
            dimension_semantics=("parallel", "parallel", "arbitrary"), vmem_limit_bytes=VMEM_LIMIT),
        name="diff_attention",
    )(lam_vecs, proj, proj, proj, bias, norm_g)


def _prefix_sum(x, axis):
    n = x.shape[axis]
    idx = lax.broadcasted_iota(jnp.int32, x.shape, axis)
    d = 1
    while d < n:
        x = x + jnp.where(idx >= d, pltpu.roll(x, d, axis), 0.0)
        d *= 2
    return x


def _mlstm_kernel(qk_ref, v_ref, og_ref, g_ref, gt_ref, cw_ref, cb_ref, gb_ref, gbt_ref, ob_ref, ng_ref,
                  out_ref, prev_ref, cn_ref, m_ref):
    L = ML_CHUNK
    H = ML_HEADS
    QW = H * ML_QK_DIM

    @pl.when(pl.program_id(1) == 0)
    def _():
        prev_ref[...] = jnp.zeros_like(prev_ref)
        cn_ref[...] = jnp.zeros_like(cn_ref)
        m_ref[...] = jnp.zeros_like(m_ref)

    x = qk_ref[...].astype(F32)
    prev = prev_ref[...]
    row = lax.broadcasted_iota(jnp.int32, x.shape, 0)
    y = cb_ref[...] + cw_ref[ML_CONV - 1:ML_CONV, :] * x
    for d in range(1, ML_CONV):
        shifted = jnp.where(row < d, pltpu.roll(prev, d, 0), pltpu.roll(x, d, 0))
        y = y + cw_ref[ML_CONV - 1 - d:ML_CONV - d, :] * shifted
    prev_ref[...] = x
    qkc = y * jax.nn.sigmoid(y)
    q = qkc[:, :QW]
    k = qkc[:, QW:] * (ML_QK_DIM ** -0.5)
    kb = k.astype(BF16)

    G = g_ref[...]
    li_all = G[:, :LANES] + gb_ref[:, :LANES]
    b_all = _prefix_sum(_log_sigmoid(G[:, LANES:] + gb_ref[:, LANES:]), 0)
    GT = gt_ref[...] + gbt_ref[...]
    bT = _prefix_sum(_log_sigmoid(GT), 1)

    qlane = lax.broadcasted_iota(jnp.int32, (L, QW), 1) // ML_QK_DIM
    qz = jnp.concatenate([jnp.where(qlane == h, q, 0.0) for h in range(H)], axis=0).astype(BF16)
    s_all = lax.dot_general(qz, kb, (((1,), (1,)), ((), ())), preferred_element_type=F32)
    cn_old = cn_ref[...]
    qcn = jnp.dot(qz, cn_old.astype(BF16), preferred_element_type=F32)

    t_idx = lax.broadcasted_iota(jnp.int32, (L, L), 0)
    j_idx = lax.broadcasted_iota(jnp.int32, (L, L), 1)
    causal = j_idx <= t_idx
    m_old_row = m_ref[...]
    mlane = lax.broadcasted_iota(jnp.int32, (1, LANES), 1)
    crow = lax.broadcasted_iota(jnp.int32, (QW, 1), 0) // ML_QK_DIM
    m_new_row = m_old_row
    wk_exp = jnp.zeros((L, QW), F32)
    decay_col = jnp.zeros((QW, 1), F32)

    for h in range(H):
        li_c = li_all[:, h:h + 1]
        b_c = b_all[:, h:h + 1]
        li_r = GT[h:h + 1, :]
        b_r = bT[H + h:H + h + 1, :]
        m_prev = m_old_row[:, h:h + 1]

        dlog = jnp.where(causal, b_c - b_r + li_r, MASK_NEG)
        inter = b_c + m_prev
        m_t = jnp.maximum(jnp.max(dlog, axis=-1, keepdims=True), inter)
        w_intra = jnp.exp(dlog - m_t)
        w_inter = jnp.exp(inter - m_t)
        s = s_all[h * L:(h + 1) * L, :] * w_intra
        vh = v_ref[:, h * ML_V_DIM:(h + 1) * ML_V_DIM]
        num = (jnp.dot(s.astype(BF16), vh, preferred_element_type=F32)
               + w_inter * qcn[h * L:(h + 1) * L, :ML_V_DIM])
        den = jnp.sum(s, axis=-1, keepdims=True) + w_inter * qcn[h * L:(h + 1) * L, ML_V_DIM:]
        hval = num / jnp.maximum(jnp.abs(den), jnp.exp(-m_t))

        cols = slice(h * ML_V_DIM, (h + 1) * ML_V_DIM)
        gate = jax.nn.sigmoid(og_ref[:, cols].astype(F32) + ob_ref[:, cols])
        out_ref[:, cols] = (_rms(hval, ng_ref[:, cols]) * gate).astype(BF16)

        g_tot = b_c[L - 1:L, :]
        wlog = g_tot - b_c + li_c
        m_new = jnp.maximum(g_tot + m_prev, jnp.max(wlog, axis=0, keepdims=True))
        decay = jnp.exp(g_tot + m_prev - m_new)
        wk = jnp.exp(wlog - m_new)
        m_new_row = jnp.where(mlane == h, m_new, m_new_row)
        wk_exp = jnp.where(qlane == h, wk, wk_exp)
        decay_col = jnp.where(crow == h, decay, decay_col)

    kw = (k * wk_exp).astype(BF16)
    vext = jnp.concatenate([v_ref[...], jnp.ones((L, LANES), BF16)], axis=1)
    upd = lax.dot_general(kw, vext, (((0,), (0,)), ((), ())), preferred_element_type=F32)
    for h in range(H):
        rows = slice(h * ML_QK_DIM, (h + 1) * ML_QK_DIM)
        dh = decay_col[rows, :]
        cn_ref[rows, :ML_V_DIM] = dh * cn_old[rows, :ML_V_DIM] + upd[rows, h * ML_V_DIM:(h + 1) * ML_V_DIM]
        cn_ref[rows, ML_V_DIM:] = dh * cn_old[rows, ML_V_DIM:] + upd[rows, H * ML_V_DIM:]
    m_ref[...] = m_new_row


def _mlstm(proj, gates, gates_t, conv_w, conv_b, gate_b, gate_bt, o_b, norm_g, B, S):
    L = ML_CHUNK
    nc = S // L
    qk_blk = 3 * DA_WIDTH // ML_QK_COLS
    v_blk, o_blk = qk_blk + 1, qk_blk + 2
    full = lambda shape: pl.BlockSpec(shape, lambda b, c: (0,) * len(shape))
    return pl.pallas_call(
        _mlstm_kernel,
        grid=(B, nc),
        in_specs=[
            pl.BlockSpec((L, ML_QK_COLS), lambda b, c: (b * nc + c, qk_blk)),
            pl.BlockSpec((L, ML_WIDTH), lambda b, c: (b * nc + c, v_blk)),
            pl.BlockSpec((L, ML_WIDTH), lambda b, c: (b * nc + c, o_blk)),
            pl.BlockSpec((L, 2 * LANES), lambda b, c: (b * nc + c, 0)),
            pl.BlockSpec((2 * ML_HEADS, L), lambda b, c: (0, b * nc + c)),
            full((ML_CONV, ML_QK_COLS)),
            full((1, ML_QK_COLS)),
            full((1, 2 * LANES)),
            full((2 * ML_HEADS, 1)),
            full((1, ML_WIDTH)),
            full((1, ML_WIDTH)),
        ],
        out_specs=pl.BlockSpec((L, ML_WIDTH), lambda b, c: (b * nc + c, 0)),
        out_shape=jax.ShapeDtypeStruct((B * S, ML_WIDTH), BF16),
        scratch_shapes=[
            pltpu.VMEM((L, ML_QK_COLS), F32),
            pltpu.VMEM((ML_HEADS * ML_QK_DIM, 2 * ML_V_DIM), F32),
            pltpu.VMEM((1, LANES), F32),
        ],
        compiler_params=pltpu.CompilerParams(
            dimension_semantics=("parallel", "arbitrary"), vmem_limit_bytes=VMEM_LIMIT),
        name="mlstm",
    )(proj, proj, proj, gates, gates_t, conv_w, conv_b, gate_b, gate_bt, o_b, norm_g)


def _outproj_kernel(a_ref, m_ref, x_ref, wo_ref, g_ref, rw_ref, rb_ref, x1_ref, h2_ref, idx_ref, gate_ref):
    x1 = (x_ref[...]
          + jnp.dot(a_ref[...], wo_ref[:DA_WIDTH, :], preferred_element_type=F32)
          + jnp.dot(m_ref[...], wo_ref[DA_WIDTH:, :], preferred_element_type=F32))
    x1_ref[...] = x1
    h2 = _rms(x1, g_ref[...])
    h2_ref[...] = h2
    work = jnp.dot(h2.astype(BF16), rw_ref[...], preferred_element_type=F32) + rb_ref[...]
    lane = lax.broadcasted_iota(jnp.int32, work.shape, 1).astype(F32)
    idxs = jnp.zeros_like(work)
    vals = jnp.zeros_like(work)
    top = None
    for kk in range(TOP_K):
        mx = jnp.max(work, axis=-1, keepdims=True)
        am = jnp.min(jnp.where(work == mx, lane, float(LANES)), axis=-1, keepdims=True)
        idxs = jnp.where(lane == kk, am, idxs)
        vals = jnp.where(lane == kk, mx, vals)
        work = jnp.where(lane == am, -jnp.inf, work)
        if kk == 0:
            top = mx
    e = jnp.where(lane < TOP_K, jnp.exp(vals - top), 0.0)
    gate_ref[...] = e / jnp.sum(e, axis=-1, keepdims=True)
    idx_ref[...] = idxs.astype(jnp.int32)


def _outproj(a_out, m_out, x2, w_out, g, router_w, router_b):
    T = x2.shape[0]
    row = lambda w: pl.BlockSpec((TM_PROJ, w), lambda i: (i, 0))
    full = lambda r, c: pl.BlockSpec((r, c), lambda i: (0, 0))
    return pl.pallas_call(
        _outproj_kernel,
        grid=(T // TM_PROJ,),
        in_specs=[row(DA_WIDTH), row(ML_WIDTH), row(D_MODEL), full(D_MODEL, D_MODEL), full(1, D_MODEL),
                  full(D_MODEL, LANES), full(1, LANES)],
        out_specs=[row(D_MODEL), row(D_MODEL), row(LANES), row(LANES)],
        out_shape=[jax.ShapeDtypeStruct((T, D_MODEL), F32), jax.ShapeDtypeStruct((T, D_MODEL), F32),
                   jax.ShapeDtypeStruct((T, LANES), jnp.int32), jax.ShapeDtypeStruct((T, LANES), F32)],
        compiler_params=pltpu.CompilerParams(dimension_semantics=("parallel",), vmem_limit_bytes=VMEM_LIMIT),
        name="outproj_router",
    )(a_out, m_out, x2, w_out, g, router_w, router_b)


DMA_UNROLL = 8


def _expert_kernel(be_ref, nu_ref, nv_ref, tok0_ref, tokn_ref, dst_ref, h_hbm, wgu_ref, bgu_ref, wd_ref, bd_ref,
                   out_hbm, xbuf, obuf, gsem, ssem):
    R = MOE_ROWS
    i = pl.program_id(0)
    nb = pl.num_programs(0)
    n_used = nu_ref[0]
    slot = i % 2

    def gather_row(tok_ref, s, r):
        return pltpu.make_async_copy(h_hbm.at[pl.ds(tok_ref[0, 0, r], 1), :], xbuf.at[s, pl.ds(r, 1), :], gsem.at[s])

    def scatter_row(s, r):
        return pltpu.make_async_copy(obuf.at[s, pl.ds(r, 1), :], out_hbm.at[pl.ds(dst_ref[0, 0, r], 1), :], ssem.at[s])

    def for_rows(fn):
        def body(c, carry):
            for u in range(DMA_UNROLL):
                fn(c * DMA_UNROLL + u)
            return carry
        lax.fori_loop(0, R // DMA_UNROLL, body, 0)

    def wait_gather(s):
        pltpu.make_async_copy(h_hbm.at[pl.ds(0, R), :], xbuf.at[s], gsem.at[s]).wait()

    def wait_scatter(s, n):
        def wait_rows(rows):
            pltpu.make_async_copy(obuf.at[s, rows, :], out_hbm.at[rows, :], ssem.at[s]).wait()

        @pl.when(n == R)
        def _():
            wait_rows(pl.ds(0, R))

        @pl.when(n < R)
        def _():
            n8 = pl.multiple_of((n // SUBLANES) * SUBLANES, SUBLANES)

            @pl.when(n8 > 0)
            def _():
                wait_rows(pl.ds(0, n8))

            def body(r, carry):
                wait_rows(pl.ds(0, 1))
                return carry
            lax.fori_loop(0, n - n8, body, 0)

    def start_scatter(s, n):
        @pl.when(n == R)
        def _():
            for_rows(lambda r: scatter_row(s, r).start())

        @pl.when(n < R)
        def _():
            def body(r, carry):
                scatter_row(s, r).start()
                return carry
            lax.fori_loop(0, n, body, 0)

    @pl.when(i == 0)
    def _():
        for_rows(lambda r: gather_row(tok0_ref, 0, r).start())

    @pl.when((i + 1 < nb) & (i + 1 < n_used))
    def _():
        for_rows(lambda r: gather_row(tokn_ref, 1 - slot, r).start())

    @pl.when(i < n_used)
    def _():
        wait_gather(slot)
        xb = xbuf[slot].astype(BF16)
        gu = jnp.dot(xb, wgu_ref[0], preferred_element_type=F32) + bgu_ref[0]
        glu = jnp.minimum(gu[:, :D_FF], SWIGLU_LIMIT)
        lin = jnp.clip(gu[:, D_FF:], -SWIGLU_LIMIT, SWIGLU_LIMIT)
        act = glu * jax.nn.sigmoid(SWIGLU_ALPHA * glu) * (lin + 1.0)
        y = jnp.dot(act.astype(BF16), wd_ref[0], preferred_element_type=F32) + bd_ref[0]

        @pl.when(i >= 2)
        def _():
            wait_scatter(slot, nv_ref[i - 2])

        obuf[slot] = y
        start_scatter(slot, nv_ref[i])

        @pl.when(i == n_used - 1)
        def _():
            wait_scatter(slot, nv_ref[i])

            @pl.when(i >= 1)
            def _():
                wait_scatter(1 - slot, nv_ref[i - 1])


def _experts(h2, block_expert, n_used, n_valid, row_tok, row_dst, w_gu, b_gu, w_down, b_down, n_out_rows):
    R = MOE_ROWS
    n_blocks = row_tok.shape[0]
    idx_spec = lambda f: pl.BlockSpec((1, 1, R), f, memory_space=pltpu.SMEM)
    grid_spec = pltpu.PrefetchScalarGridSpec(
        num_scalar_prefetch=3,
        grid=(n_blocks,),
        in_specs=[
            idx_spec(lambda i, be, nu, nv: (0, 0, 0)),
            idx_spec(lambda i, be, nu, nv: (jnp.minimum(i + 1, n_blocks - 1), 0, 0)),
            idx_spec(lambda i, be, nu, nv: (i, 0, 0)),
            pl.BlockSpec(memory_space=pl.ANY),
            pl.BlockSpec((1, D_MODEL, 2 * D_FF), lambda i, be, nu, nv: (be[i], 0, 0)),
            pl.BlockSpec((1, 1, 2 * D_FF), lambda i, be, nu, nv: (be[i], 0, 0)),
            pl.BlockSpec((1, D_FF, D_MODEL), lambda i, be, nu, nv: (be[i], 0, 0)),
            pl.BlockSpec((1, 1, D_MODEL), lambda i, be, nu, nv: (be[i], 0, 0)),
        ],
        out_specs=pl.BlockSpec(memory_space=pl.ANY),
        scratch_shapes=[
            pltpu.VMEM((2, R, D_MODEL), F32),
            pltpu.VMEM((2, R, D_MODEL), F32),
            pltpu.SemaphoreType.DMA((2,)),
            pltpu.SemaphoreType.DMA((2,)),
        ],
    )
    return pl.pallas_call(
        _expert_kernel,
        grid_spec=grid_spec,
        out_shape=jax.ShapeDtypeStruct((n_out_rows, D_MODEL), F32),
        compiler_params=pltpu.CompilerParams(dimension_semantics=("arbitrary",), vmem_limit_bytes=VMEM_LIMIT),
        name="experts",
    )(block_expert, n_used, n_valid, row_tok, row_tok, row_dst, h2, w_gu, b_gu, w_down, b_down)


def _combine_kernel(x1_ref, o4_ref, gate_ref, g_ref, out_ref):
    y = x1_ref[...]
    gt = gate_ref[...]
    for kk in range(TOP_K):
        y = y + gt[:, kk:kk + 1] * o4_ref[:, kk * D_MODEL:(kk + 1) * D_MODEL]
    out_ref[...] = _rms(y, g_ref[...])


def _combine(x1, o4, gates, g):
    T = x1.shape[0]
    tm = TM_COMBINE
    return pl.pallas_call(
        _combine_kernel,
        grid=(T // tm,),
        in_specs=[
            pl.BlockSpec((tm, D_MODEL), lambda i: (i, 0)),
            pl.BlockSpec((tm, TOP_K * D_MODEL), lambda i: (i, 0)),
            pl.BlockSpec((tm, LANES), lambda i: (i, 0)),
            pl.BlockSpec((1, D_MODEL), lambda i: (0, 0)),
        ],
        out_specs=pl.BlockSpec((tm, D_MODEL), lambda i: (i, 0)),
        out_shape=jax.ShapeDtypeStruct((T, D_MODEL), F32),
        compiler_params=pltpu.CompilerParams(dimension_semantics=("parallel",), vmem_limit_bytes=VMEM_LIMIT),
        name="combine_norm",
    )(x1, o4, gates, g)


def _dispatch_plan(top_idx, T):
    R = MOE_ROWS
    P = T * TOP_K
    e_flat = top_idx.reshape(P)
    order = jnp.argsort(e_flat).astype(jnp.int32)
    e_sorted = e_flat[order]
    counts = jnp.bincount(e_flat, length=N_EXPERTS).astype(jnp.int32)
    padded = (counts + R - 1) // R * R
    start = jnp.cumsum(counts) - counts
    pend = jnp.cumsum(padded)
    pstart = pend - padded
    dest = pstart[e_sorted] + (jnp.arange(P, dtype=jnp.int32) - start[e_sorted])
    n_rows = P + N_EXPERTS * R
    n_blocks = n_rows // R
    row_tok = jnp.zeros((n_rows,), jnp.int32).at[dest].set(order // TOP_K)
    row_dst = jnp.zeros((n_rows,), jnp.int32).at[dest].set(order)
    block_row0 = jnp.arange(n_blocks, dtype=jnp.int32) * R
    block_expert = jnp.minimum(jnp.searchsorted(pend, block_row0, side="right"), N_EXPERTS - 1).astype(jnp.int32)
    n_valid = jnp.clip(counts[block_expert] - (block_row0 - pstart[block_expert]), 0, R)
    n_valid = jnp.where(block_row0 < pend[-1], n_valid, 0).astype(jnp.int32)
    n_used = (pend[-1] // R).reshape(1).astype(jnp.int32)
    return (block_expert, n_used, n_valid, row_tok.reshape(n_blocks, 1, R), row_dst.reshape(n_blocks, 1, R))


def _layer(x2, B, S, layer, norm_attn_g, w_in, rel_bias_table, lam_vecs, diff_norm_g, conv_w, conv_b, gate_b,
           o_b, ml_norm_g, w_out, norm_ffn_g, router_w, router_b, w_gu, b_gu, w_down, b_down):
    T = B * S
    H = ML_HEADS
    w_main = w_in[:, :N_MAIN].astype(BF16)
    wg = w_in[:, N_MAIN:]
    w_gate = jnp.zeros((D_MODEL, 2 * LANES), F32).at[:, :H].set(wg[:, :H]).at[:, LANES:LANES + H].set(wg[:, H:])
    proj, gates = _inproj(x2, norm_attn_g.reshape(1, D_MODEL), w_main, w_gate.astype(BF16))

    lam_init = 0.8 - 0.6 * math.exp(-0.3 * layer)
    bias = _relbias(rel_bias_table.astype(F32))
    a_out = _attention(proj, bias, lam_vecs, diff_norm_g.reshape(DA_HEADS, 1, DA_V_DIM), B, S, lam_init)

    gates_t = jnp.concatenate([gates[:, :H], gates[:, LANES:LANES + H]], axis=1).T
    gate_b_pad = jnp.zeros((1, 2 * LANES), F32).at[0, :H].set(gate_b[:H]).at[0, LANES:LANES + H].set(gate_b[H:])
    m_out = _mlstm(proj, gates, gates_t, conv_w, conv_b.reshape(1, -1), gate_b_pad, gate_b.reshape(2 * H, 1),
                   o_b.reshape(1, -1), ml_norm_g.reshape(1, -1), B, S)

    rw = jnp.zeros((D_MODEL, LANES), F32).at[:, :N_EXPERTS].set(router_w).astype(BF16)
    rb = jnp.full((1, LANES), MASK_NEG, F32).at[0, :N_EXPERTS].set(router_b)
    x1, h2, top_idx, top_gate = _outproj(a_out, m_out, x2, w_out.astype(BF16), norm_ffn_g.reshape(1, D_MODEL), rw, rb)

    block_expert, n_used, n_valid, row_tok, row_dst = _dispatch_plan(top_idx[:, :TOP_K], T)
    o4 = _experts(h2, block_expert, n_used, n_valid, row_tok, row_dst, w_gu.astype(BF16),
                  b_gu.reshape(N_EXPERTS, 1, 2 * D_FF), w_down.astype(BF16),
                  b_down.reshape(N_EXPERTS, 1, D_MODEL), T * TOP_K)
    return x1, o4.reshape(T, TOP_K * D_MODEL), top_gate


def kernel(x, norm_attn_g, w_in, rel_bias_table, lambda_q1, lambda_k1, lambda_q2, lambda_k2, diff_norm_g,
           mlstm_conv_w, mlstm_conv_b, mlstm_gate_b, mlstm_o_b, mlstm_norm_g, w_out, norm_ffn_g, router_w,
           router_b, expert_w_gu, expert_b_gu, expert_w_down, expert_b_down, final_norm_g):
    B, S, _ = x.shape
    depth = w_in.shape[0]
    assert depth == 1, "the combine kernel fuses the last layer's MoE residual with the final norm"
    assert S % TQ == 0 and S % ML_CHUNK == 0 and (B * S) % TM_PROJ == 0 and (B * S * TOP_K) % MOE_ROWS == 0
    x2 = x.reshape(B * S, D_MODEL)
    l = 0
    lam_vecs = jnp.stack([lambda_q1[l], lambda_k1[l], lambda_q2[l], lambda_k2[l]])
    x1, o4, top_gate = _layer(
        x2, B, S, l, norm_attn_g[l], w_in[l], rel_bias_table, lam_vecs, diff_norm_g[l], mlstm_conv_w[l],
        mlstm_conv_b[l], mlstm_gate_b[l], mlstm_o_b[l], mlstm_norm_g[l], w_out[l], norm_ffn_g[l], router_w[l],
        router_b[l], expert_w_gu[l], expert_b_gu[l], expert_w_down[l], expert_b_down[l])
    out = _combine(x1, o4, top_gate, final_norm_g.reshape(1, D_MODEL))
    return out.reshape(B, S, D_MODEL)
```

```python
import functools
import math

import numpy as np
import jax
import jax.numpy as jnp
from jax import lax
from jax.experimental import pallas as pl
from jax.experimental.pallas import tpu as pltpu

F32 = jnp.float32
BF16 = jnp.bfloat16

D_MODEL = 1024
NORM_EPS = 1e-5
DA_HEADS = 4
DA_QK_DIM = 64
DA_V_DIM = 128
DA_WIDTH = DA_HEADS * DA_V_DIM
REL_BUCKETS = 32
REL_MAX_DIST = 128
ML_HEADS = 4
ML_QK_DIM = 64
ML_V_DIM = 128
ML_WIDTH = ML_HEADS * ML_V_DIM
ML_QK_COLS = 2 * ML_HEADS * ML_QK_DIM
ML_CONV = 4
N_EXPERTS = 32
TOP_K = 4
D_FF = D_MODEL
SWIGLU_LIMIT = 7.0
SWIGLU_ALPHA = 1.702

N_MAIN = 3 * DA_WIDTH + ML_QK_COLS + 2 * ML_WIDTH
LANES = 128
SUBLANES = 8
MASK_NEG = -1e30

TM_PROJ = 512
TQ = 256
ML_CHUNK = 256
MOE_ROWS = 512
TM_COMBINE = 256
VMEM_LIMIT = 56 * 1024 * 1024


def _bucket_lower_bounds():
    n = np.arange(0, 4 * REL_MAX_DIST)
    max_exact = REL_BUCKETS // 2
    nf = np.maximum(n, 1).astype(np.float32)
    large = max_exact + (np.log(nf / np.float32(max_exact)) / np.float32(math.log(REL_MAX_DIST / max_exact))
                         * np.float32(REL_BUCKETS - max_exact)).astype(np.int32)
    large = np.minimum(large, REL_BUCKETS - 1)
    bucket = np.where(n < max_exact, n, large)
    return [int(np.argmax(bucket >= b)) for b in range(REL_BUCKETS)]


BUCKET_LO = _bucket_lower_bounds()
assert BUCKET_LO[-1] <= TQ, "keys two blocks away must all sit in the last bucket"


def _rms(x, g):
    return x * lax.rsqrt(jnp.mean(x * x, axis=-1, keepdims=True) + NORM_EPS) * g


def _log_sigmoid(x):
    return jnp.minimum(x, 0.0) - jnp.log1p(jnp.exp(-jnp.abs(x)))


def _inproj_kernel(x_ref, g_ref, w_ref, wg_ref, o_ref, og_ref):
    hb = _rms(x_ref[...], g_ref[...]).astype(BF16)
    for n in range(N_MAIN // 512):
        cols = slice(n * 512, (n + 1) * 512)
        o_ref[:, cols] = jnp.dot(hb, w_ref[:, cols], preferred_element_type=F32).astype(BF16)
    og_ref[...] = jnp.dot(hb, wg_ref[...], preferred_element_type=F32)


def _inproj(x2, g, w_main, w_gate):
    T = x2.shape[0]
    return pl.pallas_call(
        _inproj_kernel,
        grid=(T // TM_PROJ,),
        in_specs=[
            pl.BlockSpec((TM_PROJ, D_MODEL), lambda i: (i, 0)),
            pl.BlockSpec((1, D_MODEL), lambda i: (0, 0)),
            pl.BlockSpec((D_MODEL, N_MAIN), lambda i: (0, 0)),
            pl.BlockSpec((D_MODEL, 2 * LANES), lambda i: (0, 0)),
        ],
        out_specs=[
            pl.BlockSpec((TM_PROJ, N_MAIN), lambda i: (i, 0)),
            pl.BlockSpec((TM_PROJ, 2 * LANES), lambda i: (i, 0)),
        ],
        out_shape=[jax.ShapeDtypeStruct((T, N_MAIN), BF16), jax.ShapeDtypeStruct((T, 2 * LANES), F32)],
        compiler_params=pltpu.CompilerParams(dimension_semantics=("parallel",), vmem_limit_bytes=VMEM_LIMIT),
        name="inproj",
    )(x2, g, w_main, w_gate)


def _relbias_kernel(tbl_ref, o_ref):
    h = pl.program_id(0)
    r = lax.broadcasted_iota(jnp.int32, (2 * TQ, 2 * TQ), 0) & (TQ - 1)
    c = lax.broadcasted_iota(jnp.int32, (2 * TQ, 2 * TQ), 1)
    rel = r + TQ - c
    val = jnp.full((2 * TQ, 2 * TQ), tbl_ref[0, h], F32)
    for b in range(1, REL_BUCKETS):
        val = jnp.where(rel >= BUCKET_LO[b], tbl_ref[b, h], val)
    o_ref[0] = jnp.where(rel >= 0, val - tbl_ref[REL_BUCKETS - 1, h], MASK_NEG)


def _relbias(table):
    return pl.pallas_call(
        _relbias_kernel,
        grid=(DA_HEADS,),
        in_specs=[pl.BlockSpec(memory_space=pltpu.SMEM)],
        out_specs=pl.BlockSpec((1, 2 * TQ, 2 * TQ), lambda h: (h, 0, 0)),
        out_shape=jax.ShapeDtypeStruct((DA_HEADS, 2 * TQ, 2 * TQ), F32),
        compiler_params=pltpu.CompilerParams(dimension_semantics=("parallel",)),
        name="relbias",
    )(table)


def _attn_kernel(lam_init, lam_ref, q_ref, k_ref, v_ref, bias_ref, g_ref, o_ref, acc_ref, m_ref, l_ref):
    i = pl.program_id(2)
    q = q_ref[...] * jnp.asarray(DA_QK_DIM ** -0.5, BF16)
    lane = lax.broadcasted_iota(jnp.int32, q.shape, 1)
    zero = jnp.zeros_like(q)
    qq = jnp.concatenate([jnp.where(lane < DA_QK_DIM, q, zero), jnp.where(lane >= DA_QK_DIM, q, zero)], axis=0)

    def rows(j, nblk):
        start = j * TQ if isinstance(j, int) else pl.multiple_of(j * TQ, TQ)
        return pl.ds(start, nblk * TQ)

    def update(j, nblk, bias):
        s = lax.dot_general(qq, k_ref[rows(j, nblk), :], (((1,), (1,)), ((), ())), preferred_element_type=F32)
        if bias is not None:
            s = s + bias
        tiles = [s[:, t * LANES:(t + 1) * LANES] for t in range(nblk * TQ // LANES)]
        m_old = m_ref[...]
        m_new = jnp.maximum(m_old, jnp.max(functools.reduce(jnp.maximum, tiles), axis=-1, keepdims=True))
        alpha = jnp.exp(m_old - m_new)
        ps = [jnp.exp(t - m_new) for t in tiles]
        p = jnp.concatenate(ps, axis=1).astype(BF16)
        l_ref[...] = alpha * l_ref[...] + jnp.sum(functools.reduce(jnp.add, ps), axis=-1, keepdims=True)
        acc_ref[...] = alpha * acc_ref[...] + jnp.dot(p, v_ref[rows(j, nblk), :], preferred_element_type=F32)
        m_ref[...] = m_new

    m_ref[...] = jnp.full_like(m_ref, MASK_NEG)
    l_ref[...] = jnp.zeros_like(l_ref)
    acc_ref[...] = jnp.zeros_like(acc_ref)

    @pl.when(i == 0)
    def _():
        update(0, 1, bias_ref[0, :, TQ:])

    @pl.when(i >= 1)
    def _():
        update(i - 1, 2, bias_ref[0])

    n_far = jnp.maximum(i - 1, 0)
    odd = n_far % 2

    @pl.when(odd == 1)
    def _():
        update(0, 1, None)

    def far_pair(jj, carry):
        update(odd + 2 * jj, 2, None)
        return carry

    lax.fori_loop(0, n_far // 2, far_pair, 0)

    lam_v = lam_ref[...]
    lam = (jnp.exp(jnp.sum(lam_v[0:1] * lam_v[1:2], axis=-1, keepdims=True))
           - jnp.exp(jnp.sum(lam_v[2:3] * lam_v[3:4], axis=-1, keepdims=True)) + lam_init)
    o = acc_ref[...] / l_ref[...]
    out = o[:TQ] - lam * o[TQ:]
    o_ref[...] = (_rms(out, g_ref[0]) * (1.0 - lam_init)).astype(BF16)


def _attention(proj, bias, lam_vecs, norm_g, B, S, lam_init):
    nq = S // TQ
    q_off, k_off, v_off = 0, DA_HEADS, 2 * DA_HEADS
    return pl.pallas_call(
        functools.partial(_attn_kernel, lam_init),
        grid=(B, DA_HEADS, nq),
        in_specs=[
            pl.BlockSpec((4, DA_QK_DIM), lambda b, h, i: (0, 0)),
            pl.BlockSpec((TQ, LANES), lambda b, h, i: (b * nq + i, q_off + h)),
            pl.BlockSpec((S, LANES), lambda b, h, i: (b, k_off + h)),
            pl.BlockSpec((S, LANES), lambda b, h, i: (b, v_off + h)),
            pl.BlockSpec((1, 2 * TQ, 2 * TQ), lambda b, h, i: (h, 0, 0)),
            pl.BlockSpec((1, 1, LANES), lambda b, h, i: (h, 0, 0)),
        ],
        out_specs=pl.BlockSpec((TQ, LANES), lambda b, h, i: (b * nq + i, h)),
        out_shape=jax.ShapeDtypeStruct((B * S, DA_WIDTH), BF16),
        scratch_shapes=[
            pltpu.VMEM((2 * TQ, DA_V_DIM), F32),
            pltpu.VMEM((2 * TQ, LANES), F32),
            pltpu.VMEM((2 * TQ, LANES), F32),
        ],
        compiler_params=pltpu.CompilerParams(
            dimension_semantics=("parallel", "parallel", "arbitrary"), vmem_limit_bytes=VMEM_LIMIT),
        name="diff_attention",
    )(lam_vecs, proj, proj, proj, bias, norm_g)


def _prefix_sum(x, axis):
    n = x.shape[axis]
    idx = lax.broadcasted_iota(jnp.int32, x.shape, axis)
    d = 1
    while d < n:
        x = x + jnp.where(idx >= d, pltpu.roll(x, d, axis), 0.0)
        d *= 2
    return x


def _mlstm_kernel(qk_ref, v_ref, og_ref, g_ref, gt_ref, cw_ref, cb_ref, gb_ref, gbt_ref, ob_ref, ng_ref,
                  out_ref, prev_ref, cn_ref, m_ref):
    L = ML_CHUNK
    H = ML_HEADS
    QW = H * ML_QK_DIM

    @pl.when(pl.program_id(1) == 0)
    def _():
        prev_ref[...] = jnp.zeros_like(prev_ref)
        cn_ref[...] = jnp.zeros_like(cn_ref)
        m_ref[...] = jnp.zeros_like(m_ref)

    x = qk_ref[...].astype(F32)
    prev = prev_ref[...]
    row = lax.broadcasted_iota(jnp.int32, x.shape, 0)
    y = cb_ref[...] + cw_ref[ML_CONV - 1:ML_CONV, :] * x
    for d in range(1, ML_CONV):
        shifted = jnp.where(row < d, pltpu.roll(prev, d, 0), pltpu.roll(x, d, 0))
        y = y + cw_ref[ML_CONV - 1 - d:ML_CONV - d, :] * shifted
    prev_ref[...] = x
    qkc = y * jax.nn.sigmoid(y)
    q = qkc[:, :QW]
    k = qkc[:, QW:] * (ML_QK_DIM ** -0.5)
    kb = k.astype(BF16)

    G = g_ref[...]
    li_all = G[:, :LANES] + gb_ref[:, :LANES]
    b_all = _prefix_sum(_log_sigmoid(G[:, LANES:] + gb_ref[:, LANES:]), 0)
    GT = gt_ref[...] + gbt_ref[...]
    bT = _prefix_sum(_log_sigmoid(GT), 1)

    qlane = lax.broadcasted_iota(jnp.int32, (L, QW), 1) // ML_QK_DIM
    qz = jnp.concatenate([jnp.where(qlane == h, q, 0.0) for h in range(H)], axis=0).astype(BF16)
    s_all = lax.dot_general(qz, kb, (((1,), (1,)), ((), ())), preferred_element_type=F32)
    cn_old = cn_ref[...]
    qcn = jnp.dot(qz, cn_old.astype(BF16), preferred_element_type=F32)

    t_idx = lax.broadcasted_iota(jnp.int32, (L, L), 0)
    j_idx = lax.broadcasted_iota(jnp.int32, (L, L), 1)
    causal = j_idx <= t_idx
    m_old_row = m_ref[...]
    mlane = lax.broadcasted_iota(jnp.int32, (1, LANES), 1)
    crow = lax.broadcasted_iota(jnp.int32, (QW, 1), 0) // ML_QK_DIM
    m_new_row = m_old_row
    wk_exp = jnp.zeros((L, QW), F32)
    decay_col = jnp.zeros((QW, 1), F32)

    for h in range(H):
        li_c = li_all[:, h:h + 1]
        b_c = b_all[:, h:h + 1]
        li_r = GT[h:h + 1, :]
        b_r = bT[H + h:H + h + 1, :]
        m_prev = m_old_row[:, h:h + 1]

        dlog = jnp.where(causal, b_c - b_r + li_r, MASK_NEG)
        inter = b_c + m_prev
        m_t = jnp.maximum(jnp.max(dlog, axis=-1, keepdims=True), inter)
        w_intra = jnp.exp(dlog - m_t)
        w_inter = jnp.exp(inter - m_t)
        s = s_all[h * L:(h + 1) * L, :] * w_intra
        vh = v_ref[:, h * ML_V_DIM:(h + 1) * ML_V_DIM]
        num = (jnp.dot(s.astype(BF16), vh, preferred_element_type=F32)
               + w_inter * qcn[h * L:(h + 1) * L, :ML_V_DIM])
        den = jnp.sum(s, axis=-1, keepdims=True) + w_inter * qcn[h * L:(h + 1) * L, ML_V_DIM:]
        hval = num / jnp.maximum(jnp.abs(den), jnp.exp(-m_t))

        cols = slice(h * ML_V_DIM, (h + 1) * ML_V_DIM)
        gate = jax.nn.sigmoid(og_ref[:, cols].astype(F32) + ob_ref[:, cols])
        out_ref[:, cols] = (_rms(hval, ng_ref[:, cols]) * gate).astype(BF16)

        g_tot = b_c[L - 1:L, :]
        wlog = g_tot - b_c + li_c
        m_new = jnp.maximum(g_tot + m_prev, jnp.max(wlog, axis=0, keepdims=True))
        decay = jnp.exp(g_tot + m_prev - m_new)
        wk = jnp.exp(wlog - m_new)
        m_new_row = jnp.where(mlane == h, m_new, m_new_row)
        wk_exp = jnp.where(qlane == h, wk, wk_exp)
        decay_col = jnp.where(crow == h, decay, decay_col)

    kw = (k * wk_exp).astype(BF16)
    vext = jnp.concatenate([v_ref[...], jnp.ones((L, LANES), BF16)], axis=1)
    upd = lax.dot_general(kw, vext, (((0,), (0,)), ((), ())), preferred_element_type=F32)
    for h in range(H):
        rows = slice(h * ML_QK_DIM, (h + 1) * ML_QK_DIM)
        dh = decay_col[rows, :]
        cn_ref[rows, :ML_V_DIM] = dh * cn_old[rows, :ML_V_DIM] + upd[rows, h * ML_V_DIM:(h + 1) * ML_V_DIM]
        cn_ref[rows, ML_V_DIM:] = dh * cn_old[rows, ML_V_DIM:] + upd[rows, H * ML_V_DIM:]
    m_ref[...] = m_new_row


def _mlstm(proj, gates, gates_t, conv_w, conv_b, gate_b, gate_bt, o_b, norm_g, B, S):
    L = ML_CHUNK
    nc = S // L
    qk_blk = 3 * DA_WIDTH // ML_QK_COLS
    v_blk, o_blk = qk_blk + 1, qk_blk + 2
    full = lambda shape: pl.BlockSpec(shape, lambda b, c: (0,) * len(shape))
    return pl.pallas_call(
        _mlstm_kernel,
        grid=(B, nc),
        in_specs=[
            pl.BlockSpec((L, ML_QK_COLS), lambda b, c: (b * nc + c, qk_blk)),
            pl.BlockSpec((L, ML_WIDTH), lambda b, c: (b * nc + c, v_blk)),
            pl.BlockSpec((L, ML_WIDTH), lambda b, c: (b * nc + c, o_blk)),
            pl.BlockSpec((L, 2 * LANES), lambda b, c: (b * nc + c, 0)),
            pl.BlockSpec((2 * ML_HEADS, L), lambda b, c: (0, b * nc + c)),
            full((ML_CONV, ML_QK_COLS)),
            full((1, ML_QK_COLS)),
            full((1, 2 * LANES)),
            full((2 * ML_HEADS, 1)),
            full((1, ML_WIDTH)),
            full((1, ML_WIDTH)),
        ],
        out_specs=pl.BlockSpec((L, ML_WIDTH), lambda b, c: (b * nc + c, 0)),
        out_shape=jax.ShapeDtypeStruct((B * S, ML_WIDTH), BF16),
        scratch_shapes=[
            pltpu.VMEM((L, ML_QK_COLS), F32),
            pltpu.VMEM((ML_HEADS * ML_QK_DIM, 2 * ML_V_DIM), F32),
            pltpu.VMEM((1, LANES), F32),
        ],
        compiler_params=pltpu.CompilerParams(
            dimension_semantics=("parallel", "arbitrary"), vmem_limit_bytes=VMEM_LIMIT),
        name="mlstm",
    )(proj, proj, proj, gates, gates_t, conv_w, conv_b, gate_b, gate_bt, o_b, norm_g)


def _outproj_kernel(a_ref, m_ref, x_ref, wo_ref, g_ref, rw_ref, rb_ref, x1_ref, h2_ref, idx_ref, gate_ref):
    x1 = (x_ref[...]
          + jnp.dot(a_ref[...], wo_ref[:DA_WIDTH, :], preferred_element_type=F32)
          + jnp.dot(m_ref[...], wo_ref[DA_WIDTH:, :], preferred_element_type=F32))
    x1_ref[...] = x1
    h2 = _rms(x1, g_ref[...])
    h2_ref[...] = h2
    work = jnp.dot(h2.astype(BF16), rw_ref[...], preferred_element_type=F32) + rb_ref[...]
    lane = lax.broadcasted_iota(jnp.int32, work.shape, 1).astype(F32)
    idxs = jnp.zeros_like(work)
    vals = jnp.zeros_like(work)
    top = None
    for kk in range(TOP_K):
        mx = jnp.max(work, axis=-1, keepdims=True)
        am = jnp.min(jnp.where(work == mx, lane, float(LANES)), axis=-1, keepdims=True)
        idxs = jnp.where(lane == kk, am, idxs)
        vals = jnp.where(lane == kk, mx, vals)
        work = jnp.where(lane == am, -jnp.inf, work)
        if kk == 0:
            top = mx
    e = jnp.where(lane < TOP_K, jnp.exp(vals - top), 0.0)
    gate_ref[...] = e / jnp.sum(e, axis=-1, keepdims=True)
    idx_ref[...] = idxs.astype(jnp.int32)


def _outproj(a_out, m_out, x2, w_out, g, router_w, router_b):
    T = x2.shape[0]
    row = lambda w: pl.BlockSpec((TM_PROJ, w), lambda i: (i, 0))
    full = lambda r, c: pl.BlockSpec((r, c), lambda i: (0, 0))
    return pl.pallas_call(
        _outproj_kernel,
        grid=(T // TM_PROJ,),
        in_specs=[row(DA_WIDTH), row(ML_WIDTH), row(D_MODEL), full(D_MODEL, D_MODEL), full(1, D_MODEL),
                  full(D_MODEL, LANES), full(1, LANES)],
        out_specs=[row(D_MODEL), row(D_MODEL), row(LANES), row(LANES)],
        out_shape=[jax.ShapeDtypeStruct((T, D_MODEL), F32), jax.ShapeDtypeStruct((T, D_MODEL), F32),
                   jax.ShapeDtypeStruct((T, LANES), jnp.int32), jax.ShapeDtypeStruct((T, LANES), F32)],
        compiler_params=pltpu.CompilerParams(dimension_semantics=("parallel",), vmem_limit_bytes=VMEM_LIMIT),
        name="outproj_router",
    )(a_out, m_out, x2, w_out, g, router_w, router_b)


DMA_UNROLL = 8


def _expert_kernel(be_ref, nu_ref, nv_ref, tok0_ref, tokn_ref, dst_ref, h_hbm, wgu_ref, bgu_ref, wd_ref, bd_ref,
                   out_hbm, xbuf, obuf, gsem, ssem):
    R = MOE_ROWS
    i = pl.program_id(0)
    nb = pl.num_programs(0)
    n_used = nu_ref[0]
    slot = i % 2

    def gather_row(tok_ref, s, r):
        return pltpu.make_async_copy(h_hbm.at[pl.ds(tok_ref[0, 0, r], 1), :], xbuf.at[s, pl.ds(r, 1), :], gsem.at[s])

    def scatter_row(s, r):
        return pltpu.make_async_copy(obuf.at[s, pl.ds(r, 1), :], out_hbm.at[pl.ds(dst_ref[0, 0, r], 1), :], ssem.at[s])

    def for_rows(fn):
        def body(c, carry):
            for u in range(DMA_UNROLL):
                fn(c * DMA_UNROLL + u)
            return carry
        lax.fori_loop(0, R // DMA_UNROLL, body, 0)

    def wait_gather(s):
        pltpu.make_async_copy(h_hbm.at[pl.ds(0, R), :], xbuf.at[s], gsem.at[s]).wait()

    def wait_scatter(s, n):
        def wait_rows(rows):
            pltpu.make_async_copy(obuf.at[s, rows, :], out_hbm.at[rows, :], ssem.at[s]).wait()

        @pl.when(n == R)
        def _():
            wait_rows(pl.ds(0, R))

        @pl.when(n < R)
        def _():
            n8 = pl.multiple_of((n // SUBLANES) * SUBLANES, SUBLANES)

            @pl.when(n8 > 0)
            def _():
                wait_rows(pl.ds(0, n8))

            def body(r, carry):
                wait_rows(pl.ds(0, 1))
                return carry
            lax.fori_loop(0, n - n8, body, 0)

    def start_scatter(s, n):
        @pl.when(n == R)
        def _():
            for_rows(lambda r: scatter_row(s, r).start())

        @pl.when(n < R)
        def _():
            def body(r, carry):
                scatter_row(s, r).start()
                return carry
            lax.fori_loop(0, n, body, 0)

    @pl.when(i == 0)
    def _():
        for_rows(lambda r: gather_row(tok0_ref, 0, r).start())

    @pl.when((i + 1 < nb) & (i + 1 < n_used))
    def _():
        for_rows(lambda r: gather_row(tokn_ref, 1 - slot, r).start())

    @pl.when(i < n_used)
    def _():
        wait_gather(slot)
        xb = xbuf[slot].astype(BF16)
        gu = jnp.dot(xb, wgu_ref[0], preferred_element_type=F32) + bgu_ref[0]
        glu = jnp.minimum(gu[:, :D_FF], SWIGLU_LIMIT)
        lin = jnp.clip(gu[:, D_FF:], -SWIGLU_LIMIT, SWIGLU_LIMIT)
        act = glu * jax.nn.sigmoid(SWIGLU_ALPHA * glu) * (lin + 1.0)
        y = jnp.dot(act.astype(BF16), wd_ref[0], preferred_element_type=F32) + bd_ref[0]

        @pl.when(i >= 2)
        def _():
            wait_scatter(slot, nv_ref[i - 2])

        obuf[slot] = y
        start_scatter(slot, nv_ref[i])

        @pl.when(i == n_used - 1)
        def _():
            wait_scatter(slot, nv_ref[i])

            @pl.when(i >= 1)
            def _():
                wait_scatter(1 - slot, nv_ref[i - 1])


def _experts(h2, block_expert, n_used, n_valid, row_tok, row_dst, w_gu, b_gu, w_down, b_down, n_out_rows):
    R = MOE_ROWS
    n_blocks = row_tok.shape[0]
    idx_spec = lambda f: pl.BlockSpec((1, 1, R), f, memory_space=pltpu.SMEM)
    grid_spec = pltpu.PrefetchScalarGridSpec(
        num_scalar_prefetch=3,
        grid=(n_blocks,),
        in_specs=[
            idx_spec(lambda i, be, nu, nv: (0, 0, 0)),
            idx_spec(lambda i, be, nu, nv: (jnp.minimum(i + 1, n_blocks - 1), 0, 0)),
            idx_spec(lambda i, be, nu, nv: (i, 0, 0)),
            pl.BlockSpec(memory_space=pl.ANY),
            pl.BlockSpec((1, D_MODEL, 2 * D_FF), lambda i, be, nu, nv: (be[i], 0, 0)),
            pl.BlockSpec((1, 1, 2 * D_FF), lambda i, be, nu, nv: (be[i], 0, 0)),
            pl.BlockSpec((1, D_FF, D_MODEL), lambda i, be, nu, nv: (be[i], 0, 0)),
            pl.BlockSpec((1, 1, D_MODEL), lambda i, be, nu, nv: (be[i], 0, 0)),
        ],
        out_specs=pl.BlockSpec(memory_space=pl.ANY),
        scratch_shapes=[
            pltpu.VMEM((2, R, D_MODEL), F32),
            pltpu.VMEM((2, R, D_MODEL), F32),
            pltpu.SemaphoreType.DMA((2,)),
            pltpu.SemaphoreType.DMA((2,)),
        ],
    )
    return pl.pallas_call(
        _expert_kernel,
        grid_spec=grid_spec,
        out_shape=jax.ShapeDtypeStruct((n_out_rows, D_MODEL), F32),
        compiler_params=pltpu.CompilerParams(dimension_semantics=("arbitrary",), vmem_limit_bytes=VMEM_LIMIT),
        name="experts",
    )(block_expert, n_used, n_valid, row_tok, row_tok, row_dst, h2, w_gu, b_gu, w_down, b_down)


def _combine_kernel(x1_ref, gate_ref, g_ref, *refs):
    o_refs, out_ref = refs[:TOP_K], refs[TOP_K]
    y = x1_ref[...]
    gt = gate_ref[...]
    for kk in range(TOP_K):
        y = y + gt[:, kk:kk + 1] * o_refs[kk][...]
    out_ref[...] = _rms(y, g_ref[...])


def _combine(x1, o4, gates, g):
    T = x1.shape[0]
    tm = TM_COMBINE
    nt = T // tm
    slot_spec = lambda kk: pl.BlockSpec((tm, D_MODEL), lambda i: (kk * nt + i, 0))
    return pl.pallas_call(
        _combine_kernel,
        grid=(nt,),
        in_specs=[
            pl.BlockSpec((tm, D_MODEL), lambda i: (i, 0)),
            pl.BlockSpec((tm, LANES), lambda i: (i, 0)),
            pl.BlockSpec((1, D_MODEL), lambda i: (0, 0)),
        ] + [slot_spec(kk) for kk in range(TOP_K)],
        out_specs=pl.BlockSpec((tm, D_MODEL), lambda i: (i, 0)),
        out_shape=jax.ShapeDtypeStruct((T, D_MODEL), F32),
        compiler_params=pltpu.CompilerParams(dimension_semantics=("parallel",), vmem_limit_bytes=VMEM_LIMIT),
        name="combine_norm",
    )(x1, gates, g, *([o4] * TOP_K))


def _dispatch_plan(top_idx, T):
    R = MOE_ROWS
    P = T * TOP_K
    e_flat = top_idx.reshape(P)
    order = jnp.argsort(e_flat).astype(jnp.int32)
    counts = jnp.sum(e_flat[:, None] == jnp.arange(N_EXPERTS, dtype=jnp.int32)[None, :], axis=0, dtype=jnp.int32)
    padded = (counts + R - 1) // R * R
    start = jnp.cumsum(counts) - counts
    pend = jnp.cumsum(padded)
    pstart = pend - padded
    n_blocks = P // R + N_EXPERTS
    block_row0 = jnp.arange(n_blocks, dtype=jnp.int32) * R
    block_expert = jnp.minimum(jnp.sum(pend[None, :] <= block_row0[:, None], axis=1, dtype=jnp.int32), N_EXPERTS - 1)
    within = block_row0 - pstart[block_expert]
    n_valid = jnp.where(block_row0 < pend[-1], jnp.clip(counts[block_expert] - within, 0, R), 0).astype(jnp.int32)
    r_in = jnp.arange(R, dtype=jnp.int32)[None, :]
    src = jnp.clip((start[block_expert] + within)[:, None] + r_in, 0, P - 1)
    pair = jnp.where(r_in < n_valid[:, None], order[src], 0)
    row_tok = pair // TOP_K
    row_dst = (pair % TOP_K) * T + row_tok
    n_used = (pend[-1] // R).reshape(1).astype(jnp.int32)
    return (block_expert, n_used, n_valid, row_tok.reshape(n_blocks, 1, R), row_dst.reshape(n_blocks, 1, R))


def _layer(x2, B, S, layer, norm_attn_g, w_in, rel_bias_table, lam_vecs, diff_norm_g, conv_w, conv_b, gate_b,
           o_b, ml_norm_g, w_out, norm_ffn_g, router_w, router_b, w_gu, b_gu, w_down, b_down):
    T = B * S
    H = ML_HEADS
    w_main = w_in[:, :N_MAIN].astype(BF16)
    wg = w_in[:, N_MAIN:]
    w_gate = jnp.zeros((D_MODEL, 2 * LANES), F32).at[:, :H].set(wg[:, :H]).at[:, LANES:LANES + H].set(wg[:, H:])
    proj, gates = _inproj(x2, norm_attn_g.reshape(1, D_MODEL), w_main, w_gate.astype(BF16))

    lam_init = 0.8 - 0.6 * math.exp(-0.3 * layer)
    bias = _relbias(rel_bias_table.astype(F32))
    a_out = _attention(proj, bias, lam_vecs, diff_norm_g.reshape(DA_HEADS, 1, DA_V_DIM), B, S, lam_init)

    gates_t = jnp.concatenate([gates[:, :H], gates[:, LANES:LANES + H]], axis=1).T
    gate_b_pad = jnp.zeros((1, 2 * LANES), F32).at[0, :H].set(gate_b[:H]).at[0, LANES:LANES + H].set(gate_b[H:])
    m_out = _mlstm(proj, gates, gates_t, conv_w, conv_b.reshape(1, -1), gate_b_pad, gate_b.reshape(2 * H, 1),
                   o_b.reshape(1, -1), ml_norm_g.reshape(1, -1), B, S)

    rw = jnp.zeros((D_MODEL, LANES), F32).at[:, :N_EXPERTS].set(router_w).astype(BF16)
    rb = jnp.full((1, LANES), MASK_NEG, F32).at[0, :N_EXPERTS].set(router_b)
    x1, h2, top_idx, top_gate = _outproj(a_out, m_out, x2, w_out.astype(BF16), norm_ffn_g.reshape(1, D_MODEL), rw, rb)

    block_expert, n_used, n_valid, row_tok, row_dst = _dispatch_plan(top_idx[:, :TOP_K], T)
    o4 = _experts(h2, block_expert, n_used, n_valid, row_tok, row_dst, w_gu.astype(BF16),
                  b_gu.reshape(N_EXPERTS, 1, 2 * D_FF), w_down.astype(BF16),
                  b_down.reshape(N_EXPERTS, 1, D_MODEL), T * TOP_K)
    return x1, o4, top_gate


def kernel(x, norm_attn_g, w_in, rel_bias_table, lambda_q1, lambda_k1, lambda_q2, lambda_k2, diff_norm_g,
           mlstm_conv_w, mlstm_conv_b, mlstm_gate_b, mlstm_o_b, mlstm_norm_g, w_out, norm_ffn_g, router_w,
           router_b, expert_w_gu, expert_b_gu, expert_w_down, expert_b_down, final_norm_g):
    B, S, _ = x.shape
    depth = w_in.shape[0]
    assert depth == 1, "the combine kernel fuses the last layer's MoE residual with the final norm"
    assert S % TQ == 0 and S % ML_CHUNK == 0 and (B * S) % TM_PROJ == 0 and (B * S * TOP_K) % MOE_ROWS == 0
    x2 = x.reshape(B * S, D_MODEL)
    l = 0
    lam_vecs = jnp.stack([lambda_q1[l], lambda_k1[l], lambda_q2[l], lambda_k2[l]])
    x1, o4, top_gate = _layer(
        x2, B, S, l, norm_attn_g[l], w_in[l], rel_bias_table, lam_vecs, diff_norm_g[l], mlstm_conv_w[l],
        mlstm_conv_b[l], mlstm_gate_b[l], mlstm_o_b[l], mlstm_norm_g[l], w_out[l], norm_ffn_g[l], router_w[l],
        router_b[l], expert_w_gu[l], expert_b_gu[l], expert_w_down[l], expert_b_down[l])
    out = _combine(x1, o4, top_gate, final_norm_g.reshape(1, D_MODEL))
    return out.reshape(B, S, D_MODEL)
```

```python
import functools
import math

import numpy as np
import jax
import jax.numpy as jnp
from jax import lax
from jax.experimental import pallas as pl
from jax.experimental.pallas import tpu as pltpu

F32 = jnp.float32
BF16 = jnp.bfloat16

D_MODEL = 1024
NORM_EPS = 1e-5
DA_HEADS = 4
DA_QK_DIM = 64
DA_V_DIM = 128
DA_WIDTH = DA_HEADS * DA_V_DIM
REL_BUCKETS = 32
REL_MAX_DIST = 128
ML_HEADS = 4
ML_QK_DIM = 64
ML_V_DIM = 128
ML_WIDTH = ML_HEADS * ML_V_DIM
ML_QK_COLS = 2 * ML_HEADS * ML_QK_DIM
ML_CONV = 4
N_EXPERTS = 32
TOP_K = 4
D_FF = D_MODEL
SWIGLU_LIMIT = 7.0
SWIGLU_ALPHA = 1.702

N_MAIN = 3 * DA_WIDTH + ML_QK_COLS + 2 * ML_WIDTH
LANES = 128
SUBLANES = 8
MASK_NEG = -1e30

TM_PROJ = 512
TQ = 256
ML_CHUNK = 256
MOE_ROWS = 512
TM_COMBINE = 256
VMEM_LIMIT = 56 * 1024 * 1024


def _bucket_lower_bounds():
    n = np.arange(0, 4 * REL_MAX_DIST)
    max_exact = REL_BUCKETS // 2
    nf = np.maximum(n, 1).astype(np.float32)
    large = max_exact + (np.log(nf / np.float32(max_exact)) / np.float32(math.log(REL_MAX_DIST / max_exact))
                         * np.float32(REL_BUCKETS - max_exact)).astype(np.int32)
    large = np.minimum(large, REL_BUCKETS - 1)
    bucket = np.where(n < max_exact, n, large)
    return [int(np.argmax(bucket >= b)) for b in range(REL_BUCKETS)]


BUCKET_LO = _bucket_lower_bounds()
assert BUCKET_LO[-1] <= TQ, "keys two blocks away must all sit in the last bucket"


def _rms(x, g):
    return x * lax.rsqrt(jnp.mean(x * x, axis=-1, keepdims=True) + NORM_EPS) * g


def _log_sigmoid(x):
    return jnp.minimum(x, 0.0) - jnp.log1p(jnp.exp(-jnp.abs(x)))


def _inproj_kernel(x_ref, g_ref, w_ref, wg_ref, o_ref, og_ref):
    hb = _rms(x_ref[...], g_ref[...]).astype(BF16)
    for n in range(N_MAIN // 512):
        cols = slice(n * 512, (n + 1) * 512)
        o_ref[:, cols] = jnp.dot(hb, w_ref[:, cols], preferred_element_type=F32).astype(BF16)
    og_ref[...] = jnp.dot(hb, wg_ref[...], preferred_element_type=F32)


def _inproj(x2, g, w_main, w_gate):
    T = x2.shape[0]
    return pl.pallas_call(
        _inproj_kernel,
        grid=(T // TM_PROJ,),
        in_specs=[
            pl.BlockSpec((TM_PROJ, D_MODEL), lambda i: (i, 0)),
            pl.BlockSpec((1, D_MODEL), lambda i: (0, 0)),
            pl.BlockSpec((D_MODEL, N_MAIN), lambda i: (0, 0)),
            pl.BlockSpec((D_MODEL, 2 * LANES), lambda i: (0, 0)),
        ],
        out_specs=[
            pl.BlockSpec((TM_PROJ, N_MAIN), lambda i: (i, 0)),
            pl.BlockSpec((TM_PROJ, 2 * LANES), lambda i: (i, 0)),
        ],
        out_shape=[jax.ShapeDtypeStruct((T, N_MAIN), BF16), jax.ShapeDtypeStruct((T, 2 * LANES), F32)],
        compiler_params=pltpu.CompilerParams(dimension_semantics=("parallel",), vmem_limit_bytes=VMEM_LIMIT),
        name="inproj",
    )(x2, g, w_main, w_gate)


def _relbias_kernel(tbl_ref, o_ref):
    h = pl.program_id(0)
    r = lax.broadcasted_iota(jnp.int32, (2 * TQ, 2 * TQ), 0) & (TQ - 1)
    c = lax.broadcasted_iota(jnp.int32, (2 * TQ, 2 * TQ), 1)
    rel = r + TQ - c
    val = jnp.full((2 * TQ, 2 * TQ), tbl_ref[0, h], F32)
    for b in range(1, REL_BUCKETS):
        val = jnp.where(rel >= BUCKET_LO[b], tbl_ref[b, h], val)
    o_ref[0] = jnp.where(rel >= 0, val - tbl_ref[REL_BUCKETS - 1, h], MASK_NEG)


def _relbias(table):
    return pl.pallas_call(
        _relbias_kernel,
        grid=(DA_HEADS,),
        in_specs=[pl.BlockSpec(memory_space=pltpu.SMEM)],
        out_specs=pl.BlockSpec((1, 2 * TQ, 2 * TQ), lambda h: (h, 0, 0)),
        out_shape=jax.ShapeDtypeStruct((DA_HEADS, 2 * TQ, 2 * TQ), F32),
        compiler_params=pltpu.CompilerParams(dimension_semantics=("parallel",)),
        name="relbias",
    )(table)


def _attn_kernel(lam_init, lam_ref, q_ref, k_ref, v_ref, bias_ref, g_ref, o_ref, acc_ref, m_ref, l_ref):
    i = pl.program_id(2)
    q = q_ref[...] * jnp.asarray(DA_QK_DIM ** -0.5, BF16)
    lane = lax.broadcasted_iota(jnp.int32, q.shape, 1)
    zero = jnp.zeros_like(q)
    qq = jnp.concatenate([jnp.where(lane < DA_QK_DIM, q, zero), jnp.where(lane >= DA_QK_DIM, q, zero)], axis=0)

    def rows(j, nblk):
        start = j * TQ if isinstance(j, int) else pl.multiple_of(j * TQ, TQ)
        return pl.ds(start, nblk * TQ)

    def update(j, nblk, bias):
        s = lax.dot_general(qq, k_ref[rows(j, nblk), :], (((1,), (1,)), ((), ())), preferred_element_type=F32)
        if bias is not None:
            s = s + bias
        tiles = [s[:, t * LANES:(t + 1) * LANES] for t in range(nblk * TQ // LANES)]
        m_old = m_ref[...]
        m_new = jnp.maximum(m_old, jnp.max(functools.reduce(jnp.maximum, tiles), axis=-1, keepdims=True))
        alpha = jnp.exp(m_old - m_new)
        ps = [jnp.exp(t - m_new) for t in tiles]
        p = jnp.concatenate(ps, axis=1).astype(BF16)
        l_ref[...] = alpha * l_ref[...] + jnp.sum(functools.reduce(jnp.add, ps), axis=-1, keepdims=True)
        acc_ref[...] = alpha * acc_ref[...] + jnp.dot(p, v_ref[rows(j, nblk), :], preferred_element_type=F32)
        m_ref[...] = m_new

    m_ref[...] = jnp.full_like(m_ref, MASK_NEG)
    l_ref[...] = jnp.zeros_like(l_ref)
    acc_ref[...] = jnp.zeros_like(acc_ref)

    @pl.when(i == 0)
    def _():
        update(0, 1, bias_ref[0, :, TQ:])

    @pl.when(i >= 1)
    def _():
        update(i - 1, 2, bias_ref[0])

    n_far = jnp.maximum(i - 1, 0)
    odd = n_far % 2

    @pl.when(odd == 1)
    def _():
        update(0, 1, None)

    def far_pair(jj, carry):
        update(odd + 2 * jj, 2, None)
        return carry

    lax.fori_loop(0, n_far // 2, far_pair, 0)

    lam_v = lam_ref[...]
    lam = (jnp.exp(jnp.sum(lam_v[0:1] * lam_v[1:2], axis=-1, keepdims=True))
           - jnp.exp(jnp.sum(lam_v[2:3] * lam_v[3:4], axis=-1, keepdims=True)) + lam_init)
    o = acc_ref[...] / l_ref[...]
    out = o[:TQ] - lam * o[TQ:]
    o_ref[...] = (_rms(out, g_ref[0]) * (1.0 - lam_init)).astype(BF16)


def _attention(proj, bias, lam_vecs, norm_g, B, S, lam_init):
    nq = S // TQ
    q_off, k_off, v_off = 0, DA_HEADS, 2 * DA_HEADS
    return pl.pallas_call(
        functools.partial(_attn_kernel, lam_init),
        grid=(B, DA_HEADS, nq),
        in_specs=[
            pl.BlockSpec((4, DA_QK_DIM), lambda b, h, i: (0, 0)),
            pl.BlockSpec((TQ, LANES), lambda b, h, i: (b * nq + i, q_off + h)),
            pl.BlockSpec((S, LANES), lambda b, h, i: (b, k_off + h)),
            pl.BlockSpec((S, LANES), lambda b, h, i: (b, v_off + h)),
            pl.BlockSpec((1, 2 * TQ, 2 * TQ), lambda b, h, i: (h, 0, 0)),
            pl.BlockSpec((1, 1, LANES), lambda b, h, i: (h, 0, 0)),
        ],
        out_specs=pl.BlockSpec((TQ, LANES), lambda b, h, i: (b * nq + i, h)),
        out_shape=jax.ShapeDtypeStruct((B * S, DA_WIDTH), BF16),
        scratch_shapes=[
            pltpu.VMEM((2 * TQ, DA_V_DIM), F32),
            pltpu.VMEM((2 * TQ, LANES), F32),
            pltpu.VMEM((2 * TQ, LANES), F32),
        ],
        compiler_params=pltpu.CompilerParams(
            dimension_semantics=("parallel", "parallel", "arbitrary"), vmem_limit_bytes=VMEM_LIMIT),
        name="diff_attention",
    )(lam_vecs, proj, proj, proj, bias, norm_g)


def _prefix_sum(x, axis):
    n = x.shape[axis]
    idx = lax.broadcasted_iota(jnp.int32, x.shape, axis)
    d = 1
    while d < n:
        x = x + jnp.where(idx >= d, pltpu.roll(x, d, axis), 0.0)
        d *= 2
    return x


def _mlstm_kernel(qk_ref, v_ref, og_ref, g_ref, gt_ref, cw_ref, cb_ref, gb_ref, gbt_ref, ob_ref, ng_ref,
                  out_ref, prev_ref, cn_ref, m_ref):
    L = ML_CHUNK
    H = ML_HEADS
    QW = H * ML_QK_DIM

    @pl.when(pl.program_id(1) == 0)
    def _():
        prev_ref[...] = jnp.zeros_like(prev_ref)
        cn_ref[...] = jnp.zeros_like(cn_ref)
        m_ref[...] = jnp.zeros_like(m_ref)

    x = qk_ref[...].astype(F32)
    prev = prev_ref[...]
    row = lax.broadcasted_iota(jnp.int32, x.shape, 0)
    y = cb_ref[...] + cw_ref[ML_CONV - 1:ML_CONV, :] * x
    for d in range(1, ML_CONV):
        shifted = jnp.where(row < d, pltpu.roll(prev, d, 0), pltpu.roll(x, d, 0))
        y = y + cw_ref[ML_CONV - 1 - d:ML_CONV - d, :] * shifted
    prev_ref[...] = x
    qkc = y * jax.nn.sigmoid(y)
    q = qkc[:, :QW]
    k = qkc[:, QW:] * (ML_QK_DIM ** -0.5)
    kb = k.astype(BF16)

    G = g_ref[...]
    li_all = G[:, :LANES] + gb_ref[:, :LANES]
    b_all = _prefix_sum(_log_sigmoid(G[:, LANES:] + gb_ref[:, LANES:]), 0)
    GT = gt_ref[...] + gbt_ref[...]
    bT = _prefix_sum(_log_sigmoid(GT), 1)

    qlane = lax.broadcasted_iota(jnp.int32, (L, QW), 1) // ML_QK_DIM
    qz = jnp.concatenate([jnp.where(qlane == h, q, 0.0) for h in range(H)], axis=0).astype(BF16)
    s_all = lax.dot_general(qz, kb, (((1,), (1,)), ((), ())), preferred_element_type=F32)
    cn_old = cn_ref[...]
    qcn = jnp.dot(qz, cn_old.astype(BF16), preferred_element_type=F32)

    t_idx = lax.broadcasted_iota(jnp.int32, (L, L), 0)
    j_idx = lax.broadcasted_iota(jnp.int32, (L, L), 1)
    causal = j_idx <= t_idx
    m_old_row = m_ref[...]
    mlane = lax.broadcasted_iota(jnp.int32, (1, LANES), 1)
    crow = lax.broadcasted_iota(jnp.int32, (QW, 1), 0) // ML_QK_DIM
    m_new_row = m_old_row
    wk_exp = jnp.zeros((L, QW), F32)
    decay_col = jnp.zeros((QW, 1), F32)

    for h in range(H):
        li_c = li_all[:, h:h + 1]
        b_c = b_all[:, h:h + 1]
        li_r = GT[h:h + 1, :]
        b_r = bT[H + h:H + h + 1, :]
        m_prev = m_old_row[:, h:h + 1]

        dlog = jnp.where(causal, b_c - b_r + li_r, MASK_NEG)
        inter = b_c + m_prev
        m_t = jnp.maximum(jnp.max(dlog, axis=-1, keepdims=True), inter)
        w_intra = jnp.exp(dlog - m_t)
        w_inter = jnp.exp(inter - m_t)
        s = s_all[h * L:(h + 1) * L, :] * w_intra
        vh = v_ref[:, h * ML_V_DIM:(h + 1) * ML_V_DIM]
        num = (jnp.dot(s.astype(BF16), vh, preferred_element_type=F32)
               + w_inter * qcn[h * L:(h + 1) * L, :ML_V_DIM])
        den = jnp.sum(s, axis=-1, keepdims=True) + w_inter * qcn[h * L:(h + 1) * L, ML_V_DIM:]
        hval = num / jnp.maximum(jnp.abs(den), jnp.exp(-m_t))

        cols = slice(h * ML_V_DIM, (h + 1) * ML_V_DIM)
        gate = jax.nn.sigmoid(og_ref[:, cols].astype(F32) + ob_ref[:, cols])
        out_ref[:, cols] = (_rms(hval, ng_ref[:, cols]) * gate).astype(BF16)

        g_tot = b_c[L - 1:L, :]
        wlog = g_tot - b_c + li_c
        m_new = jnp.maximum(g_tot + m_prev, jnp.max(wlog, axis=0, keepdims=True))
        decay = jnp.exp(g_tot + m_prev - m_new)
        wk = jnp.exp(wlog - m_new)
        m_new_row = jnp.where(mlane == h, m_new, m_new_row)
        wk_exp = jnp.where(qlane == h, wk, wk_exp)
        decay_col = jnp.where(crow == h, decay, decay_col)

    kw = (k * wk_exp).astype(BF16)
    vext = jnp.concatenate([v_ref[...], jnp.ones((L, LANES), BF16)], axis=1)
    upd = lax.dot_general(kw, vext, (((0,), (0,)), ((), ())), preferred_element_type=F32)
    for h in range(H):
        rows = slice(h * ML_QK_DIM, (h + 1) * ML_QK_DIM)
        dh = decay_col[rows, :]
        cn_ref[rows, :ML_V_DIM] = dh * cn_old[rows, :ML_V_DIM] + upd[rows, h * ML_V_DIM:(h + 1) * ML_V_DIM]
        cn_ref[rows, ML_V_DIM:] = dh * cn_old[rows, ML_V_DIM:] + upd[rows, H * ML_V_DIM:]
    m_ref[...] = m_new_row


def _mlstm(proj, gates, gates_t, conv_w, conv_b, gate_b, gate_bt, o_b, norm_g, B, S):
    L = ML_CHUNK
    nc = S // L
    qk_blk = 3 * DA_WIDTH // ML_QK_COLS
    v_blk, o_blk = qk_blk + 1, qk_blk + 2
    full = lambda shape: pl.BlockSpec(shape, lambda b, c: (0,) * len(shape))
    return pl.pallas_call(
        _mlstm_kernel,
        grid=(B, nc),
        in_specs=[
            pl.BlockSpec((L, ML_QK_COLS), lambda b, c: (b * nc + c, qk_blk)),
            pl.BlockSpec((L, ML_WIDTH), lambda b, c: (b * nc + c, v_blk)),
            pl.BlockSpec((L, ML_WIDTH), lambda b, c: (b * nc + c, o_blk)),
            pl.BlockSpec((L, 2 * LANES), lambda b, c: (b * nc + c, 0)),
            pl.BlockSpec((2 * ML_HEADS, L), lambda b, c: (0, b * nc + c)),
            full((ML_CONV, ML_QK_COLS)),
            full((1, ML_QK_COLS)),
            full((1, 2 * LANES)),
            full((2 * ML_HEADS, 1)),
            full((1, ML_WIDTH)),
            full((1, ML_WIDTH)),
        ],
        out_specs=pl.BlockSpec((L, ML_WIDTH), lambda b, c: (b * nc + c, 0)),
        out_shape=jax.ShapeDtypeStruct((B * S, ML_WIDTH), BF16),
        scratch_shapes=[
            pltpu.VMEM((L, ML_QK_COLS), F32),
            pltpu.VMEM((ML_HEADS * ML_QK_DIM, 2 * ML_V_DIM), F32),
            pltpu.VMEM((1, LANES), F32),
        ],
        compiler_params=pltpu.CompilerParams(
            dimension_semantics=("parallel", "arbitrary"), vmem_limit_bytes=VMEM_LIMIT),
        name="mlstm",
    )(proj, proj, proj, gates, gates_t, conv_w, conv_b, gate_b, gate_bt, o_b, norm_g)


def _outproj_kernel(a_ref, m_ref, x_ref, wo_ref, g_ref, rw_ref, rb_ref, x1_ref, h2_ref, idx_ref, gate_ref):
    x1 = (x_ref[...]
          + jnp.dot(a_ref[...], wo_ref[:DA_WIDTH, :], preferred_element_type=F32)
          + jnp.dot(m_ref[...], wo_ref[DA_WIDTH:, :], preferred_element_type=F32))
    x1_ref[...] = x1
    h2 = _rms(x1, g_ref[...])
    h2_ref[...] = h2
    work = jnp.dot(h2.astype(BF16), rw_ref[...], preferred_element_type=F32) + rb_ref[...]
    lane = lax.broadcasted_iota(jnp.int32, work.shape, 1).astype(F32)
    idxs = jnp.zeros_like(work)
    vals = jnp.zeros_like(work)
    top = None
    for kk in range(TOP_K):
        mx = jnp.max(work, axis=-1, keepdims=True)
        am = jnp.min(jnp.where(work == mx, lane, float(LANES)), axis=-1, keepdims=True)
        idxs = jnp.where(lane == kk, am, idxs)
        vals = jnp.where(lane == kk, mx, vals)
        work = jnp.where(lane == am, -jnp.inf, work)
        if kk == 0:
            top = mx
    e = jnp.where(lane < TOP_K, jnp.exp(vals - top), 0.0)
    gate_ref[...] = e / jnp.sum(e, axis=-1, keepdims=True)
    idx_ref[...] = idxs.astype(jnp.int32)


def _outproj(a_out, m_out, x2, w_out, g, router_w, router_b):
    T = x2.shape[0]
    row = lambda w: pl.BlockSpec((TM_PROJ, w), lambda i: (i, 0))
    full = lambda r, c: pl.BlockSpec((r, c), lambda i: (0, 0))
    return pl.pallas_call(
        _outproj_kernel,
        grid=(T // TM_PROJ,),
        in_specs=[row(DA_WIDTH), row(ML_WIDTH), row(D_MODEL), full(D_MODEL, D_MODEL), full(1, D_MODEL),
                  full(D_MODEL, LANES), full(1, LANES)],
        out_specs=[row(D_MODEL), row(D_MODEL), row(LANES), row(LANES)],
        out_shape=[jax.ShapeDtypeStruct((T, D_MODEL), F32), jax.ShapeDtypeStruct((T, D_MODEL), F32),
                   jax.ShapeDtypeStruct((T, LANES), jnp.int32), jax.ShapeDtypeStruct((T, LANES), F32)],
        compiler_params=pltpu.CompilerParams(dimension_semantics=("parallel",), vmem_limit_bytes=VMEM_LIMIT),
        name="outproj_router",
    )(a_out, m_out, x2, w_out, g, router_w, router_b)


DMA_UNROLL = 8


def _expert_kernel(be_ref, nu_ref, tok0_ref, tok_a_ref, tok_b_ref, dst_a_ref, dst_b_ref, h_hbm,
                   wgu_a, bgu_a, wd_a, bd_a, wgu_b, bgu_b, wd_b, bd_b,
                   out_hbm, xbuf0, xbuf1, obuf0, obuf1, gsem, ssem):
    R = MOE_ROWS
    g = pl.program_id(0)
    n_used = nu_ref[0]
    n_real = out_hbm.shape[0] - 2 * R

    def gather_row(tok_ref, xdst, sem, r):
        return pltpu.make_async_copy(h_hbm.at[pl.ds(tok_ref[0, 0, r], 1), :], xdst.at[pl.ds(r, 1), :], sem)

    def scatter_row(dst_ref, osrc, sem, r):
        return pltpu.make_async_copy(osrc.at[pl.ds(r, 1), :], out_hbm.at[pl.ds(dst_ref[0, 0, r], 1), :], sem)

    def for_rows(fn):
        def body(c, carry):
            for u in range(DMA_UNROLL):
                fn(c * DMA_UNROLL + u)
            return carry
        lax.fori_loop(0, R // DMA_UNROLL, body, 0)

    def wait_gather(xdst, sem):
        pltpu.make_async_copy(h_hbm.at[pl.ds(0, R), :], xdst, sem).wait()

    def wait_scatter(osrc, sem):
        pltpu.make_async_copy(osrc, out_hbm.at[pl.ds(0, R), :], sem).wait()

    @pl.when(g == 0)
    def _():
        obuf0[...] = jnp.zeros_like(obuf0)
        obuf1[...] = jnp.zeros_like(obuf1)
        for half, (osrc, sem) in enumerate(((obuf0, ssem.at[0]), (obuf1, ssem.at[1]))):
            fill = pltpu.make_async_copy(osrc, out_hbm.at[pl.ds(n_real + half * R, R), :], sem)
            fill.start()
            fill.wait()
        for_rows(lambda r: gather_row(tok0_ref, xbuf0, gsem.at[0], r).start())

    def substep(blk, cur, tok_next_ref, dst_prev_ref, wgu, bgu, wd, bd):
        xcur, xnext = (xbuf0, xbuf1) if cur == 0 else (xbuf1, xbuf0)
        ocur, oprev = (obuf0, obuf1) if cur == 0 else (obuf1, obuf0)
        gs_cur, gs_next = gsem.at[cur], gsem.at[1 - cur]
        ss_cur, ss_prev = ssem.at[cur], ssem.at[1 - cur]

        @pl.when(blk < n_used)
        def _():
            for r in range(R):
                gather_row(tok_next_ref, xnext, gs_next, r).start(priority=r % 2)
                scatter_row(dst_prev_ref, oprev, ss_prev, r).start(priority=r % 2)
            wait_gather(xcur, gs_cur)
            xb = xcur[...].astype(BF16)
            gu = jnp.dot(xb, wgu[0], preferred_element_type=F32) + bgu[0]
            glu = jnp.minimum(gu[:, :D_FF], SWIGLU_LIMIT)
            lin = jnp.clip(gu[:, D_FF:], -SWIGLU_LIMIT, SWIGLU_LIMIT)
            act = glu * jax.nn.sigmoid(SWIGLU_ALPHA * glu) * (lin + 1.0)
            y = jnp.dot(act.astype(BF16), wd[0], preferred_element_type=F32) + bd[0]

            @pl.when(blk >= 1)
            def _():
                wait_scatter(ocur, ss_cur)

            ocur[...] = y

        @pl.when(blk == n_used)
        def _():
            wait_gather(xcur, gs_cur)
            for_rows(lambda r: scatter_row(dst_prev_ref, oprev, ss_prev, r).start())
            wait_scatter(ocur, ss_cur)
            wait_scatter(oprev, ss_prev)

    substep(2 * g, 0, tok_a_ref, dst_a_ref, wgu_a, bgu_a, wd_a, bd_a)
    substep(2 * g + 1, 1, tok_b_ref, dst_b_ref, wgu_b, bgu_b, wd_b, bd_b)


def _experts(h2, block_expert, n_used, row_tok, row_dst_ext, w_gu, b_gu, w_down, b_down, n_real_rows):
    R = MOE_ROWS
    n_blocks = row_tok.shape[0]
    assert n_blocks % 2 == 0
    last = n_blocks - 1
    idx_spec = lambda f: pl.BlockSpec((1, 1, R), f, memory_space=pltpu.SMEM)
    weights = lambda off: [
        pl.BlockSpec((1, D_MODEL, 2 * D_FF), lambda g, be, nu: (be[2 * g + off], 0, 0)),
        pl.BlockSpec((1, 1, 2 * D_FF), lambda g, be, nu: (be[2 * g + off], 0, 0)),
        pl.BlockSpec((1, D_FF, D_MODEL), lambda g, be, nu: (be[2 * g + off], 0, 0)),
        pl.BlockSpec((1, 1, D_MODEL), lambda g, be, nu: (be[2 * g + off], 0, 0)),
    ]
    grid_spec = pltpu.PrefetchScalarGridSpec(
        num_scalar_prefetch=2,
        grid=(n_blocks // 2,),
        in_specs=[
            idx_spec(lambda g, be, nu: (0, 0, 0)),
            idx_spec(lambda g, be, nu: (jnp.minimum(2 * g + 1, last), 0, 0)),
            idx_spec(lambda g, be, nu: (jnp.minimum(2 * g + 2, last), 0, 0)),
            idx_spec(lambda g, be, nu: (2 * g, 0, 0)),
            idx_spec(lambda g, be, nu: (2 * g + 1, 0, 0)),
            pl.BlockSpec(memory_space=pl.ANY),
        ] + weights(0) + weights(1),
        out_specs=pl.BlockSpec(memory_space=pl.ANY),
        scratch_shapes=[pltpu.VMEM((R, D_MODEL), F32)] * 4
        + [pltpu.SemaphoreType.DMA((2,)), pltpu.SemaphoreType.DMA((2,))],
    )
    return pl.pallas_call(
        _expert_kernel,
        grid_spec=grid_spec,
        out_shape=jax.ShapeDtypeStruct((n_real_rows + 2 * R, D_MODEL), F32),
        compiler_params=pltpu.CompilerParams(dimension_semantics=("arbitrary",), vmem_limit_bytes=VMEM_LIMIT),
        name="experts",
    )(block_expert, n_used, row_tok, row_tok, row_tok, row_dst_ext, row_dst_ext, h2,
      w_gu, b_gu, w_down, b_down, w_gu, b_gu, w_down, b_down)


def _combine_kernel(x1_ref, gate_ref, g_ref, *refs):
    o_refs, out_ref = refs[:TOP_K], refs[TOP_K]
    y = x1_ref[...]
    gt = gate_ref[...]
    for kk in range(TOP_K):
        y = y + gt[:, kk:kk + 1] * o_refs[kk][...]
    out_ref[...] = _rms(y, g_ref[...])


def _combine(x1, o4, gates, g):
    T = x1.shape[0]
    tm = TM_COMBINE
    nt = T // tm
    slot_spec = lambda kk: pl.BlockSpec((tm, D_MODEL), lambda i: (kk * nt + i, 0))
    return pl.pallas_call(
        _combine_kernel,
        grid=(nt,),
        in_specs=[
            pl.BlockSpec((tm, D_MODEL), lambda i: (i, 0)),
            pl.BlockSpec((tm, LANES), lambda i: (i, 0)),
            pl.BlockSpec((1, D_MODEL), lambda i: (0, 0)),
        ] + [slot_spec(kk) for kk in range(TOP_K)],
        out_specs=pl.BlockSpec((tm, D_MODEL), lambda i: (i, 0)),
        out_shape=jax.ShapeDtypeStruct((T, D_MODEL), F32),
        compiler_params=pltpu.CompilerParams(dimension_semantics=("parallel",), vmem_limit_bytes=VMEM_LIMIT),
        name="combine_norm",
    )(x1, gates, g, *([o4] * TOP_K))


def _dispatch_plan(top_idx, T):
    R = MOE_ROWS
    P = T * TOP_K
    e_flat = top_idx.reshape(P)
    order = jnp.argsort(e_flat).astype(jnp.int32)
    counts = jnp.sum(e_flat[:, None] == jnp.arange(N_EXPERTS, dtype=jnp.int32)[None, :], axis=0, dtype=jnp.int32)
    padded = (counts + R - 1) // R * R
    start = jnp.cumsum(counts) - counts
    pend = jnp.cumsum(padded)
    pstart = pend - padded
    n_blocks = P // R + N_EXPERTS
    block_row0 = jnp.arange(n_blocks, dtype=jnp.int32) * R
    block_expert = jnp.minimum(jnp.sum(pend[None, :] <= block_row0[:, None], axis=1, dtype=jnp.int32), N_EXPERTS - 1)
    within = block_row0 - pstart[block_expert]
    n_valid = jnp.where(block_row0 < pend[-1], jnp.clip(counts[block_expert] - within, 0, R), 0).astype(jnp.int32)
    r_in = jnp.arange(R, dtype=jnp.int32)[None, :]
    src = jnp.clip((start[block_expert] + within)[:, None] + r_in, 0, P - 1)
    is_valid = r_in < n_valid[:, None]
    pair = jnp.where(is_valid, order[src], 0)
    row_tok = pair // TOP_K
    dump = P + (jnp.arange(-1, n_blocks, dtype=jnp.int32)[:, None] % 2) * R + r_in
    row_dst = jnp.where(is_valid, (pair % TOP_K) * T + row_tok, dump[1:])
    row_dst_ext = jnp.concatenate([dump[:1], row_dst[:-1]], axis=0)
    n_used = (pend[-1] // R).reshape(1).astype(jnp.int32)
    return (block_expert, n_used, row_tok.reshape(n_blocks, 1, R), row_dst_ext.reshape(n_blocks, 1, R))


def _layer(x2, B, S, layer, norm_attn_g, w_in, rel_bias_table, lam_vecs, diff_norm_g, conv_w, conv_b, gate_b,
           o_b, ml_norm_g, w_out, norm_ffn_g, router_w, router_b, w_gu, b_gu, w_down, b_down):
    T = B * S
    H = ML_HEADS
    w_main = w_in[:, :N_MAIN].astype(BF16)
    wg = w_in[:, N_MAIN:]
    w_gate = jnp.zeros((D_MODEL, 2 * LANES), F32).at[:, :H].set(wg[:, :H]).at[:, LANES:LANES + H].set(wg[:, H:])
    proj, gates = _inproj(x2, norm_attn_g.reshape(1, D_MODEL), w_main, w_gate.astype(BF16))

    lam_init = 0.8 - 0.6 * math.exp(-0.3 * layer)
    bias = _relbias(rel_bias_table.astype(F32))
    a_out = _attention(proj, bias, lam_vecs, diff_norm_g.reshape(DA_HEADS, 1, DA_V_DIM), B, S, lam_init)

    gates_t = jnp.concatenate([gates[:, :H], gates[:, LANES:LANES + H]], axis=1).T
    gate_b_pad = jnp.zeros((1, 2 * LANES), F32).at[0, :H].set(gate_b[:H]).at[0, LANES:LANES + H].set(gate_b[H:])
    m_out = _mlstm(proj, gates, gates_t, conv_w, conv_b.reshape(1, -1), gate_b_pad, gate_b.reshape(2 * H, 1),
                   o_b.reshape(1, -1), ml_norm_g.reshape(1, -1), B, S)

    rw = jnp.zeros((D_MODEL, LANES), F32).at[:, :N_EXPERTS].set(router_w).astype(BF16)
    rb = jnp.full((1, LANES), MASK_NEG, F32).at[0, :N_EXPERTS].set(router_b)
    x1, h2, top_idx, top_gate = _outproj(a_out, m_out, x2, w_out.astype(BF16), norm_ffn_g.reshape(1, D_MODEL), rw, rb)

    block_expert, n_used, row_tok, row_dst = _dispatch_plan(top_idx[:, :TOP_K], T)
    o4 = _experts(h2, block_expert, n_used, row_tok, row_dst, w_gu.astype(BF16),
                  b_gu.reshape(N_EXPERTS, 1, 2 * D_FF), w_down.astype(BF16),
                  b_down.reshape(N_EXPERTS, 1, D_MODEL), T * TOP_K)
    return x1, o4, top_gate


def kernel(x, norm_attn_g, w_in, rel_bias_table, lambda_q1, lambda_k1, lambda_q2, lambda_k2, diff_norm_g,
           mlstm_conv_w, mlstm_conv_b, mlstm_gate_b, mlstm_o_b, mlstm_norm_g, w_out, norm_ffn_g, router_w,
           router_b, expert_w_gu, expert_b_gu, expert_w_down, expert_b_down, final_norm_g):
    B, S, _ = x.shape
    depth = w_in.shape[0]
    assert depth == 1, "the combine kernel fuses the last layer's MoE residual with the final norm"
    assert S % TQ == 0 and S % ML_CHUNK == 0 and (B * S) % TM_PROJ == 0 and (B * S * TOP_K) % MOE_ROWS == 0
    x2 = x.reshape(B * S, D_MODEL)
    l = 0
    lam_vecs = jnp.stack([lambda_q1[l], lambda_k1[l], lambda_q2[l], lambda_k2[l]])
    x1, o4, top_gate = _layer(
        x2, B, S, l, norm_attn_g[l], w_in[l], rel_bias_table, lam_vecs, diff_norm_g[l], mlstm_conv_w[l],
        mlstm_conv_b[l], mlstm_gate_b[l], mlstm_o_b[l], mlstm_norm_g[l], w_out[l], norm_ffn_g[l], router_w[l],
        router_b[l], expert_w_gu[l], expert_b_gu[l], expert_w_down[l], expert_b_down[l])
    out = _combine(x1, o4, top_gate, final_norm_g.reshape(1, D_MODEL))
    return out.reshape(B, S, D_MODEL)
```

```python
import functools
import math

import numpy as np
import jax
import jax.numpy as jnp
from jax import lax
from jax.experimental import pallas as pl
from jax.experimental.pallas import tpu as pltpu
from jax.experimental.pallas import tpu_sc as plsc

F32 = jnp.float32
BF16 = jnp.bfloat16

D_MODEL = 1024
NORM_EPS = 1e-5
DA_HEADS = 4
DA_QK_DIM = 64
DA_V_DIM = 128
DA_WIDTH = DA_HEADS * DA_V_DIM
REL_BUCKETS = 32
REL_MAX_DIST = 128
ML_HEADS = 4
ML_QK_DIM = 64
ML_V_DIM = 128
ML_WIDTH = ML_HEADS * ML_V_DIM
ML_QK_COLS = 2 * ML_HEADS * ML_QK_DIM
ML_CONV = 4
N_EXPERTS = 32
TOP_K = 4
D_FF = D_MODEL
SWIGLU_LIMIT = 7.0
SWIGLU_ALPHA = 1.702

N_MAIN = 3 * DA_WIDTH + ML_QK_COLS + 2 * ML_WIDTH
LANES = 128
SUBLANES = 8
MASK_NEG = -1e30

TM_PROJ = 512
TQ = 256
ML_CHUNK = 256
MOE_ROWS = 512
TM_COMBINE = 256
ROW_PARTS = 4
PART_W = D_MODEL // ROW_PARTS
SC_WINDOW = 128
VMEM_LIMIT = 56 * 1024 * 1024


def _bucket_lower_bounds():
    n = np.arange(0, 4 * REL_MAX_DIST)
    max_exact = REL_BUCKETS // 2
    nf = np.maximum(n, 1).astype(np.float32)
    large = max_exact + (np.log(nf / np.float32(max_exact)) / np.float32(math.log(REL_MAX_DIST / max_exact))
                         * np.float32(REL_BUCKETS - max_exact)).astype(np.int32)
    large = np.minimum(large, REL_BUCKETS - 1)
    bucket = np.where(n < max_exact, n, large)
    return [int(np.argmax(bucket >= b)) for b in range(REL_BUCKETS)]


BUCKET_LO = _bucket_lower_bounds()
assert BUCKET_LO[-1] <= TQ, "keys two blocks away must all sit in the last bucket"


def _rms(x, g):
    return x * lax.rsqrt(jnp.mean(x * x, axis=-1, keepdims=True) + NORM_EPS) * g


def _log_sigmoid(x):
    return jnp.minimum(x, 0.0) - jnp.log1p(jnp.exp(-jnp.abs(x)))


def _inproj_kernel(x_ref, g_ref, w_ref, wg_ref, o_ref, og_ref):
    hb = _rms(x_ref[...], g_ref[...]).astype(BF16)
    for n in range(N_MAIN // 512):
        cols = slice(n * 512, (n + 1) * 512)
        o_ref[:, cols] = jnp.dot(hb, w_ref[:, cols], preferred_element_type=F32).astype(BF16)
    og_ref[...] = jnp.dot(hb, wg_ref[...], preferred_element_type=F32)


def _inproj(x2, g, w_main, w_gate):
    T = x2.shape[0]
    return pl.pallas_call(
        _inproj_kernel,
        grid=(T // TM_PROJ,),
        in_specs=[
            pl.BlockSpec((TM_PROJ, D_MODEL), lambda i: (i, 0)),
            pl.BlockSpec((1, D_MODEL), lambda i: (0, 0)),
            pl.BlockSpec((D_MODEL, N_MAIN), lambda i: (0, 0)),
            pl.BlockSpec((D_MODEL, 2 * LANES), lambda i: (0, 0)),
        ],
        out_specs=[
            pl.BlockSpec((TM_PROJ, N_MAIN), lambda i: (i, 0)),
            pl.BlockSpec((TM_PROJ, 2 * LANES), lambda i: (i, 0)),
        ],
        out_shape=[jax.ShapeDtypeStruct((T, N_MAIN), BF16), jax.ShapeDtypeStruct((T, 2 * LANES), F32)],
        compiler_params=pltpu.CompilerParams(dimension_semantics=("parallel",), vmem_limit_bytes=VMEM_LIMIT),
        name="inproj",
    )(x2, g, w_main, w_gate)


def _relbias_kernel(tbl_ref, o_ref):
    h = pl.program_id(0)
    r = lax.broadcasted_iota(jnp.int32, (2 * TQ, 2 * TQ), 0) & (TQ - 1)
    c = lax.broadcasted_iota(jnp.int32, (2 * TQ, 2 * TQ), 1)
    rel = r + TQ - c
    val = jnp.full((2 * TQ, 2 * TQ), tbl_ref[0, h], F32)
    for b in range(1, REL_BUCKETS):
        val = jnp.where(rel >= BUCKET_LO[b], tbl_ref[b, h], val)
    o_ref[0] = jnp.where(rel >= 0, val - tbl_ref[REL_BUCKETS - 1, h], MASK_NEG)


def _relbias(table):
    return pl.pallas_call(
        _relbias_kernel,
        grid=(DA_HEADS,),
        in_specs=[pl.BlockSpec(memory_space=pltpu.SMEM)],
        out_specs=pl.BlockSpec((1, 2 * TQ, 2 * TQ), lambda h: (h, 0, 0)),
        out_shape=jax.ShapeDtypeStruct((DA_HEADS, 2 * TQ, 2 * TQ), F32),
        compiler_params=pltpu.CompilerParams(dimension_semantics=("parallel",)),
        name="relbias",
    )(table)


def _attn_kernel(lam_init, lam_ref, q_ref, k_ref, v_ref, bias_ref, g_ref, o_ref, acc_ref, m_ref, l_ref):
    i = pl.program_id(2)
    q = q_ref[...] * jnp.asarray(DA_QK_DIM ** -0.5, BF16)
    lane = lax.broadcasted_iota(jnp.int32, q.shape, 1)
    zero = jnp.zeros_like(q)
    qq = jnp.concatenate([jnp.where(lane < DA_QK_DIM, q, zero), jnp.where(lane >= DA_QK_DIM, q, zero)], axis=0)

    def rows(j, nblk):
        start = j * TQ if isinstance(j, int) else pl.multiple_of(j * TQ, TQ)
        return pl.ds(start, nblk * TQ)

    def update(j, nblk, bias):
        s = lax.dot_general(qq, k_ref[rows(j, nblk), :], (((1,), (1,)), ((), ())), preferred_element_type=F32)
        if bias is not None:
            s = s + bias
        tiles = [s[:, t * LANES:(t + 1) * LANES] for t in range(nblk * TQ // LANES)]
        m_old = m_ref[...]
        m_new = jnp.maximum(m_old, jnp.max(functools.reduce(jnp.maximum, tiles), axis=-1, keepdims=True))
        alpha = jnp.exp(m_old - m_new)
        ps = [jnp.exp(t - m_new) for t in tiles]
        p = jnp.concatenate(ps, axis=1).astype(BF16)
        l_ref[...] = alpha * l_ref[...] + jnp.sum(functools.reduce(jnp.add, ps), axis=-1, keepdims=True)
        acc_ref[...] = alpha * acc_ref[...] + jnp.dot(p, v_ref[rows(j, nblk), :], preferred_element_type=F32)
        m_ref[...] = m_new

    m_ref[...] = jnp.full_like(m_ref, MASK_NEG)
    l_ref[...] = jnp.zeros_like(l_ref)
    acc_ref[...] = jnp.zeros_like(acc_ref)

    @pl.when(i == 0)
    def _():
        update(0, 1, bias_ref[0, :, TQ:])

    @pl.when(i >= 1)
    def _():
        update(i - 1, 2, bias_ref[0])

    n_far = jnp.maximum(i - 1, 0)
    odd = n_far % 2

    @pl.when(odd == 1)
    def _():
        update(0, 1, None)

    def far_pair(jj, carry):
        update(odd + 2 * jj, 2, None)
        return carry

    lax.fori_loop(0, n_far // 2, far_pair, 0)

    lam_v = lam_ref[...]
    lam = (jnp.exp(jnp.sum(lam_v[0:1] * lam_v[1:2], axis=-1, keepdims=True))
           - jnp.exp(jnp.sum(lam_v[2:3] * lam_v[3:4], axis=-1, keepdims=True)) + lam_init)
    o = acc_ref[...] / l_ref[...]
    out = o[:TQ] - lam * o[TQ:]
    o_ref[...] = (_rms(out, g_ref[0]) * (1.0 - lam_init)).astype(BF16)


def _attention(proj, bias, lam_vecs, norm_g, B, S, lam_init):
    nq = S // TQ
    q_off, k_off, v_off = 0, DA_HEADS, 2 * DA_HEADS
    return pl.pallas_call(
        functools.partial(_attn_kernel, lam_init),
        grid=(B, DA_HEADS, nq),
        in_specs=[
            pl.BlockSpec((4, DA_QK_DIM), lambda b, h, i: (0, 0)),
            pl.BlockSpec((TQ, LANES), lambda b, h, i: (b * nq + i, q_off + h)),
            pl.BlockSpec((S, LANES), lambda b, h, i: (b, k_off + h)),
            pl.BlockSpec((S, LANES), lambda b, h, i: (b, v_off + h)),
            pl.BlockSpec((1, 2 * TQ, 2 * TQ), lambda b, h, i: (h, 0, 0)),
            pl.BlockSpec((1, 1, LANES), lambda b, h, i: (h, 0, 0)),
        ],
        out_specs=pl.BlockSpec((TQ, LANES), lambda b, h, i: (b * nq + i, h)),
        out_shape=jax.ShapeDtypeStruct((B * S, DA_WIDTH), BF16),
        scratch_shapes=[
            pltpu.VMEM((2 * TQ, DA_V_DIM), F32),
            pltpu.VMEM((2 * TQ, LANES), F32),
            pltpu.VMEM((2 * TQ, LANES), F32),
        ],
        compiler_params=pltpu.CompilerParams(
            dimension_semantics=("parallel", "parallel", "arbitrary"), vmem_limit_bytes=VMEM_LIMIT),
        name="diff_attention",
    )(lam_vecs, proj, proj, proj, bias, norm_g)


def _prefix_sum(x, axis):
    n = x.shape[axis]
    idx = lax.broadcasted_iota(jnp.int32, x.shape, axis)
    d = 1
    while d < n:
        x = x + jnp.where(idx >= d, pltpu.roll(x, d, axis), 0.0)
        d *= 2
    return x


def _mlstm_kernel(qk_ref, v_ref, og_ref, g_ref, gt_ref, cw_ref, cb_ref, gb_ref, gbt_ref, ob_ref, ng_ref,
                  out_ref, prev_ref, cn_ref, m_ref):
    L = ML_CHUNK
    H = ML_HEADS
    QW = H * ML_QK_DIM

    @pl.when(pl.program_id(1) == 0)
    def _():
        prev_ref[...] = jnp.zeros_like(prev_ref)
        cn_ref[...] = jnp.zeros_like(cn_ref)
        m_ref[...] = jnp.zeros_like(m_ref)

    x = qk_ref[...].astype(F32)
    prev = prev_ref[...]
    row = lax.broadcasted_iota(jnp.int32, x.shape, 0)
    y = cb_ref[...] + cw_ref[ML_CONV - 1:ML_CONV, :] * x
    for d in range(1, ML_CONV):
        shifted = jnp.where(row < d, pltpu.roll(prev, d, 0), pltpu.roll(x, d, 0))
        y = y + cw_ref[ML_CONV - 1 - d:ML_CONV - d, :] * shifted
    prev_ref[...] = x
    qkc = y * jax.nn.sigmoid(y)
    q = qkc[:, :QW]
    k = qkc[:, QW:] * (ML_QK_DIM ** -0.5)
    kb = k.astype(BF16)

    G = g_ref[...]
    li_all = G[:, :LANES] + gb_ref[:, :LANES]
    b_all = _prefix_sum(_log_sigmoid(G[:, LANES:] + gb_ref[:, LANES:]), 0)
    GT = gt_ref[...] + gbt_ref[...]
    bT = _prefix_sum(_log_sigmoid(GT), 1)

    qlane = lax.broadcasted_iota(jnp.int32, (L, QW), 1) // ML_QK_DIM
    qz = jnp.concatenate([jnp.where(qlane == h, q, 0.0) for h in range(H)], axis=0).astype(BF16)
    s_all = lax.dot_general(qz, kb, (((1,), (1,)), ((), ())), preferred_element_type=F32)
    cn_old = cn_ref[...]
    qcn = jnp.dot(qz, cn_old.astype(BF16), preferred_element_type=F32)

    t_idx = lax.broadcasted_iota(jnp.int32, (L, L), 0)
    j_idx = lax.broadcasted_iota(jnp.int32, (L, L), 1)
    causal = j_idx <= t_idx
    m_old_row = m_ref[...]
    mlane = lax.broadcasted_iota(jnp.int32, (1, LANES), 1)
    crow = lax.broadcasted_iota(jnp.int32, (QW, 1), 0) // ML_QK_DIM
    m_new_row = m_old_row
    wk_exp = jnp.zeros((L, QW), F32)
    decay_col = jnp.zeros((QW, 1), F32)

    for h in range(H):
        li_c = li_all[:, h:h + 1]
        b_c = b_all[:, h:h + 1]
        li_r = GT[h:h + 1, :]
        b_r = bT[H + h:H + h + 1, :]
        m_prev = m_old_row[:, h:h + 1]

        dlog = jnp.where(causal, b_c - b_r + li_r, MASK_NEG)
        inter = b_c + m_prev
        m_t = jnp.maximum(jnp.max(dlog, axis=-1, keepdims=True), inter)
        w_intra = jnp.exp(dlog - m_t)
        w_inter = jnp.exp(inter - m_t)
        s = s_all[h * L:(h + 1) * L, :] * w_intra
        vh = v_ref[:, h * ML_V_DIM:(h + 1) * ML_V_DIM]
        num = (jnp.dot(s.astype(BF16), vh, preferred_element_type=F32)
               + w_inter * qcn[h * L:(h + 1) * L, :ML_V_DIM])
        den = jnp.sum(s, axis=-1, keepdims=True) + w_inter * qcn[h * L:(h + 1) * L, ML_V_DIM:]
        hval = num / jnp.maximum(jnp.abs(den), jnp.exp(-m_t))

        cols = slice(h * ML_V_DIM, (h + 1) * ML_V_DIM)
        gate = jax.nn.sigmoid(og_ref[:, cols].astype(F32) + ob_ref[:, cols])
        out_ref[:, cols] = (_rms(hval, ng_ref[:, cols]) * gate).astype(BF16)

        g_tot = b_c[L - 1:L, :]
        wlog = g_tot - b_c + li_c
        m_new = jnp.maximum(g_tot + m_prev, jnp.max(wlog, axis=0, keepdims=True))
        decay = jnp.exp(g_tot + m_prev - m_new)
        wk = jnp.exp(wlog - m_new)
        m_new_row = jnp.where(mlane == h, m_new, m_new_row)
        wk_exp = jnp.where(qlane == h, wk, wk_exp)
        decay_col = jnp.where(crow == h, decay, decay_col)

    kw = (k * wk_exp).astype(BF16)
    vext = jnp.concatenate([v_ref[...], jnp.ones((L, LANES), BF16)], axis=1)
    upd = lax.dot_general(kw, vext, (((0,), (0,)), ((), ())), preferred_element_type=F32)
    for h in range(H):
        rows = slice(h * ML_QK_DIM, (h + 1) * ML_QK_DIM)
        dh = decay_col[rows, :]
        cn_ref[rows, :ML_V_DIM] = dh * cn_old[rows, :ML_V_DIM] + upd[rows, h * ML_V_DIM:(h + 1) * ML_V_DIM]
        cn_ref[rows, ML_V_DIM:] = dh * cn_old[rows, ML_V_DIM:] + upd[rows, H * ML_V_DIM:]
    m_ref[...] = m_new_row


def _mlstm(proj, gates, gates_t, conv_w, conv_b, gate_b, gate_bt, o_b, norm_g, B, S):
    L = ML_CHUNK
    nc = S // L
    qk_blk = 3 * DA_WIDTH // ML_QK_COLS
    v_blk, o_blk = qk_blk + 1, qk_blk + 2
    full = lambda shape: pl.BlockSpec(shape, lambda b, c: (0,) * len(shape))
    return pl.pallas_call(
        _mlstm_kernel,
        grid=(B, nc),
        in_specs=[
            pl.BlockSpec((L, ML_QK_COLS), lambda b, c: (b * nc + c, qk_blk)),
            pl.BlockSpec((L, ML_WIDTH), lambda b, c: (b * nc + c, v_blk)),
            pl.BlockSpec((L, ML_WIDTH), lambda b, c: (b * nc + c, o_blk)),
            pl.BlockSpec((L, 2 * LANES), lambda b, c: (b * nc + c, 0)),
            pl.BlockSpec((2 * ML_HEADS, L), lambda b, c: (0, b * nc + c)),
            full((ML_CONV, ML_QK_COLS)),
            full((1, ML_QK_COLS)),
            full((1, 2 * LANES)),
            full((2 * ML_HEADS, 1)),
            full((1, ML_WIDTH)),
            full((1, ML_WIDTH)),
        ],
        out_specs=pl.BlockSpec((L, ML_WIDTH), lambda b, c: (b * nc + c, 0)),
        out_shape=jax.ShapeDtypeStruct((B * S, ML_WIDTH), BF16),
        scratch_shapes=[
            pltpu.VMEM((L, ML_QK_COLS), F32),
            pltpu.VMEM((ML_HEADS * ML_QK_DIM, 2 * ML_V_DIM), F32),
            pltpu.VMEM((1, LANES), F32),
        ],
        compiler_params=pltpu.CompilerParams(
            dimension_semantics=("parallel", "arbitrary"), vmem_limit_bytes=VMEM_LIMIT),
        name="mlstm",
    )(proj, proj, proj, gates, gates_t, conv_w, conv_b, gate_b, gate_bt, o_b, norm_g)


def _outproj_kernel(a_ref, m_ref, x_ref, wo_ref, g_ref, rw_ref, rb_ref, x1_ref, h2_ref, idx_ref, gate_ref):
    x1 = (x_ref[...]
          + jnp.dot(a_ref[...], wo_ref[:DA_WIDTH, :], preferred_element_type=F32)
          + jnp.dot(m_ref[...], wo_ref[DA_WIDTH:, :], preferred_element_type=F32))
    x1_ref[...] = x1
    h2 = _rms(x1, g_ref[...])
    for j in range(ROW_PARTS):
        h2_ref[j] = h2[:, j * PART_W:(j + 1) * PART_W]
    work = jnp.dot(h2.astype(BF16), rw_ref[...], preferred_element_type=F32) + rb_ref[...]
    lane = lax.broadcasted_iota(jnp.int32, work.shape, 1).astype(F32)
    idxs = jnp.zeros_like(work)
    vals = jnp.zeros_like(work)
    top = None
    for kk in range(TOP_K):
        mx = jnp.max(work, axis=-1, keepdims=True)
        am = jnp.min(jnp.where(work == mx, lane, float(LANES)), axis=-1, keepdims=True)
        idxs = jnp.where(lane == kk, am, idxs)
        vals = jnp.where(lane == kk, mx, vals)
        work = jnp.where(lane == am, -jnp.inf, work)
        if kk == 0:
            top = mx
    e = jnp.where(lane < TOP_K, jnp.exp(vals - top), 0.0)
    gate_ref[...] = e / jnp.sum(e, axis=-1, keepdims=True)
    idx_ref[...] = idxs.astype(jnp.int32)


def _outproj(a_out, m_out, x2, w_out, g, router_w, router_b):
    T = x2.shape[0]
    row = lambda w: pl.BlockSpec((TM_PROJ, w), lambda i: (i, 0))
    full = lambda r, c: pl.BlockSpec((r, c), lambda i: (0, 0))
    return pl.pallas_call(
        _outproj_kernel,
        grid=(T // TM_PROJ,),
        in_specs=[row(DA_WIDTH), row(ML_WIDTH), row(D_MODEL), full(D_MODEL, D_MODEL), full(1, D_MODEL),
                  full(D_MODEL, LANES), full(1, LANES)],
        out_specs=[row(D_MODEL), pl.BlockSpec((ROW_PARTS, TM_PROJ, PART_W), lambda i: (0, i, 0)),
                   row(LANES), row(LANES)],
        out_shape=[jax.ShapeDtypeStruct((T, D_MODEL), F32), jax.ShapeDtypeStruct((ROW_PARTS, T, PART_W), F32),
                   jax.ShapeDtypeStruct((T, LANES), jnp.int32), jax.ShapeDtypeStruct((T, LANES), F32)],
        compiler_params=pltpu.CompilerParams(dimension_semantics=("parallel",), vmem_limit_bytes=VMEM_LIMIT),
        name="outproj_router",
    )(a_out, m_out, x2, w_out, g, router_w, router_b)


def _sc_mesh():
    return plsc.VectorSubcoreMesh(core_axis_name="core", subcore_axis_name="subcore")


def _sc_gather_rows(x, idx):
    n = idx.shape[1]

    @functools.partial(pl.kernel, out_type=jax.ShapeDtypeStruct((n, x.shape[1]), x.dtype), mesh=_sc_mesh())
    def gather_kernel(x_hbm, i_hbm, o_hbm):
        def body(i_vmem, o_vmem):
            pltpu.sync_copy(x_hbm.at[i_vmem.at[0]], o_vmem)

        pltpu.emit_pipeline(
            body,
            grid=(n // SC_WINDOW,),
            in_specs=[pl.BlockSpec((1, SC_WINDOW), lambda i: (0, i))],
            out_specs=[pl.BlockSpec((SC_WINDOW, x.shape[1]), lambda i: (i, 0))],
            core_axis_name=("core", "subcore"),
            dimension_semantics=(pltpu.PARALLEL,),
        )(i_hbm, o_hbm)

    return gather_kernel(x, idx)


def _sc_scatter_rows(y, idx):
    n = idx.shape[1]

    @functools.partial(pl.kernel, out_type=jax.ShapeDtypeStruct(y.shape, y.dtype), mesh=_sc_mesh())
    def scatter_kernel(y_hbm, i_hbm, o_hbm):
        def body(y_vmem, i_vmem):
            pltpu.sync_copy(y_vmem, o_hbm.at[i_vmem.at[0]])

        pltpu.emit_pipeline(
            body,
            grid=(n // SC_WINDOW,),
            in_specs=[pl.BlockSpec((SC_WINDOW, y.shape[1]), lambda i: (i, 0)),
                      pl.BlockSpec((1, SC_WINDOW), lambda i: (0, i))],
            out_specs=[],
            core_axis_name=("core", "subcore"),
            dimension_semantics=(pltpu.PARALLEL,),
        )(y_hbm, i_hbm)

    return scatter_kernel(y, idx)


def _expert_kernel(be_ref, nu_ref, x_ref, wgu_ref, bgu_ref, wd_ref, bd_ref, y_ref):
    i = pl.program_id(0)

    @pl.when(i < nu_ref[0])
    def _():
        xb = jnp.concatenate([x_ref[j] for j in range(ROW_PARTS)], axis=1).astype(BF16)
        gu = jnp.dot(xb, wgu_ref[0], preferred_element_type=F32) + bgu_ref[0]
        glu = jnp.minimum(gu[:, :D_FF], SWIGLU_LIMIT)
        lin = jnp.clip(gu[:, D_FF:], -SWIGLU_LIMIT, SWIGLU_LIMIT)
        act = glu * jax.nn.sigmoid(SWIGLU_ALPHA * glu) * (lin + 1.0)
        y = jnp.dot(act.astype(BF16), wd_ref[0], preferred_element_type=F32) + bd_ref[0]
        for j in range(ROW_PARTS):
            y_ref[j] = y[:, j * PART_W:(j + 1) * PART_W]

    @pl.when(i >= nu_ref[0])
    def _():
        y_ref[...] = jnp.zeros_like(y_ref)


def _experts(h2, block_expert, n_used, row_tok, row_dst, w_gu, b_gu, w_down, b_down):
    R = MOE_ROWS
    T = h2.shape[1]
    n_rows = row_tok.shape[0]
    part = jnp.arange(ROW_PARTS, dtype=jnp.int32)[:, None]
    src_idx = (part * T + row_tok[None, :]).reshape(1, ROW_PARTS * n_rows)
    dst_idx = (part * n_rows + row_dst[None, :]).reshape(1, ROW_PARTS * n_rows)
    xs = _sc_gather_rows(h2.reshape(ROW_PARTS * T, PART_W), src_idx).reshape(ROW_PARTS, n_rows, PART_W)

    rows_spec = pl.BlockSpec((ROW_PARTS, R, PART_W), lambda i, be, nu: (0, i, 0))
    grid_spec = pltpu.PrefetchScalarGridSpec(
        num_scalar_prefetch=2,
        grid=(n_rows // R,),
        in_specs=[
            rows_spec,
            pl.BlockSpec((1, D_MODEL, 2 * D_FF), lambda i, be, nu: (be[i], 0, 0)),
            pl.BlockSpec((1, 1, 2 * D_FF), lambda i, be, nu: (be[i], 0, 0)),
            pl.BlockSpec((1, D_FF, D_MODEL), lambda i, be, nu: (be[i], 0, 0)),
            pl.BlockSpec((1, 1, D_MODEL), lambda i, be, nu: (be[i], 0, 0)),
        ],
        out_specs=rows_spec,
    )
    ys = pl.pallas_call(
        _expert_kernel,
        grid_spec=grid_spec,
        out_shape=jax.ShapeDtypeStruct((ROW_PARTS, n_rows, PART_W), F32),
        compiler_params=pltpu.CompilerParams(dimension_semantics=("arbitrary",), vmem_limit_bytes=VMEM_LIMIT),
        name="experts",
    )(block_expert, n_used, xs, w_gu, b_gu, w_down, b_down)
    out = _sc_scatter_rows(ys.reshape(ROW_PARTS * n_rows, PART_W), dst_idx)
    return out.reshape(ROW_PARTS, n_rows, PART_W)


def _combine_kernel(x1_ref, gate_ref, g_ref, *refs):
    o_refs, out_ref = refs[:TOP_K], refs[TOP_K]
    y = x1_ref[...]
    gt = gate_ref[...]
    for kk in range(TOP_K):
        o = jnp.concatenate([o_refs[kk][j] for j in range(ROW_PARTS)], axis=1)
        y = y + gt[:, kk:kk + 1] * o
    out_ref[...] = _rms(y, g_ref[...])


def _combine(x1, o4, gates, g):
    T = x1.shape[0]
    tm = TM_COMBINE
    nt = T // tm
    slot_spec = lambda kk: pl.BlockSpec((ROW_PARTS, tm, PART_W), lambda i: (0, kk * nt + i, 0))
    return pl.pallas_call(
        _combine_kernel,
        grid=(nt,),
        in_specs=[
            pl.BlockSpec((tm, D_MODEL), lambda i: (i, 0)),
            pl.BlockSpec((tm, LANES), lambda i: (i, 0)),
            pl.BlockSpec((1, D_MODEL), lambda i: (0, 0)),
        ] + [slot_spec(kk) for kk in range(TOP_K)],
        out_specs=pl.BlockSpec((tm, D_MODEL), lambda i: (i, 0)),
        out_shape=jax.ShapeDtypeStruct((T, D_MODEL), F32),
        compiler_params=pltpu.CompilerParams(dimension_semantics=("parallel",), vmem_limit_bytes=VMEM_LIMIT),
        name="combine_norm",
    )(x1, gates, g, *([o4] * TOP_K))


def _dispatch_plan(top_idx, T):
    R = MOE_ROWS
    P = T * TOP_K
    e_flat = top_idx.reshape(P)
    order = jnp.argsort(e_flat).astype(jnp.int32)
    counts = jnp.sum(e_flat[:, None] == jnp.arange(N_EXPERTS, dtype=jnp.int32)[None, :], axis=0, dtype=jnp.int32)
    padded = (counts + R - 1) // R * R
    start = jnp.cumsum(counts) - counts
    pend = jnp.cumsum(padded)
    pstart = pend - padded
    n_blocks = P // R + N_EXPERTS
    block_row0 = jnp.arange(n_blocks, dtype=jnp.int32) * R
    block_expert = jnp.minimum(jnp.sum(pend[None, :] <= block_row0[:, None], axis=1, dtype=jnp.int32), N_EXPERTS - 1)
    within = block_row0 - pstart[block_expert]
    n_valid = jnp.where(block_row0 < pend[-1], jnp.clip(counts[block_expert] - within, 0, R), 0).astype(jnp.int32)
    r_in = jnp.arange(R, dtype=jnp.int32)[None, :]
    src = jnp.clip((start[block_expert] + within)[:, None] + r_in, 0, P - 1)
    is_valid = (r_in < n_valid[:, None]).reshape(-1)
    pair = jnp.where(is_valid, order[src].reshape(-1), 0)
    row_tok = pair // TOP_K
    pad_rank = jnp.cumsum(1 - is_valid.astype(jnp.int32)) - 1
    row_dst = jnp.where(is_valid, (pair % TOP_K) * T + row_tok, P + pad_rank)
    n_used = (pend[-1] // R).reshape(1).astype(jnp.int32)
    return block_expert, n_used, row_tok, row_dst


def _layer(x2, B, S, layer, norm_attn_g, w_in, rel_bias_table, lam_vecs, diff_norm_g, conv_w, conv_b, gate_b,
           o_b, ml_norm_g, w_out, norm_ffn_g, router_w, router_b, w_gu, b_gu, w_down, b_down):
    T = B * S
    H = ML_HEADS
    w_main = w_in[:, :N_MAIN].astype(BF16)
    wg = w_in[:, N_MAIN:]
    w_gate = jnp.zeros((D_MODEL, 2 * LANES), F32).at[:, :H].set(wg[:, :H]).at[:, LANES:LANES + H].set(wg[:, H:])
    proj, gates = _inproj(x2, norm_attn_g.reshape(1, D_MODEL), w_main, w_gate.astype(BF16))

    lam_init = 0.8 - 0.6 * math.exp(-0.3 * layer)
    bias = _relbias(rel_bias_table.astype(F32))
    a_out = _attention(proj, bias, lam_vecs, diff_norm_g.reshape(DA_HEADS, 1, DA_V_DIM), B, S, lam_init)

    gates_t = jnp.concatenate([gates[:, :H], gates[:, LANES:LANES + H]], axis=1).T
    gate_b_pad = jnp.zeros((1, 2 * LANES), F32).at[0, :H].set(gate_b[:H]).at[0, LANES:LANES + H].set(gate_b[H:])
    m_out = _mlstm(proj, gates, gates_t, conv_w, conv_b.reshape(1, -1), gate_b_pad, gate_b.reshape(2 * H, 1),
                   o_b.reshape(1, -1), ml_norm_g.reshape(1, -1), B, S)

    rw = jnp.zeros((D_MODEL, LANES), F32).at[:, :N_EXPERTS].set(router_w).astype(BF16)
    rb = jnp.full((1, LANES), MASK_NEG, F32).at[0, :N_EXPERTS].set(router_b)
    x1, h2, top_idx, top_gate = _outproj(a_out, m_out, x2, w_out.astype(BF16), norm_ffn_g.reshape(1, D_MODEL), rw, rb)

    block_expert, n_used, row_tok, row_dst = _dispatch_plan(top_idx[:, :TOP_K], T)
    o4 = _experts(h2, block_expert, n_used, row_tok, row_dst, w_gu.astype(BF16),
                  b_gu.reshape(N_EXPERTS, 1, 2 * D_FF), w_down.astype(BF16),
                  b_down.reshape(N_EXPERTS, 1, D_MODEL))
    return x1, o4, top_gate


def kernel(x, norm_attn_g, w_in, rel_bias_table, lambda_q1, lambda_k1, lambda_q2, lambda_k2, diff_norm_g,
           mlstm_conv_w, mlstm_conv_b, mlstm_gate_b, mlstm_o_b, mlstm_norm_g, w_out, norm_ffn_g, router_w,
           router_b, expert_w_gu, expert_b_gu, expert_w_down, expert_b_down, final_norm_g):
    B, S, _ = x.shape
    depth = w_in.shape[0]
    assert depth == 1, "the combine kernel fuses the last layer's MoE residual with the final norm"
    assert S % TQ == 0 and S % ML_CHUNK == 0 and (B * S) % TM_PROJ == 0 and (B * S * TOP_K) % MOE_ROWS == 0
    x2 = x.reshape(B * S, D_MODEL)
    l = 0
    lam_vecs = jnp.stack([lambda_q1[l], lambda_k1[l], lambda_q2[l], lambda_k2[l]])
    x1, o4, top_gate = _layer(
        x2, B, S, l, norm_attn_g[l], w_in[l], rel_bias_table, lam_vecs, diff_norm_g[l], mlstm_conv_w[l],
        mlstm_conv_b[l], mlstm_gate_b[l], mlstm_o_b[l], mlstm_norm_g[l], w_out[l], norm_ffn_g[l], router_w[l],
        router_b[l], expert_w_gu[l], expert_b_gu[l], expert_w_down[l], expert_b_down[l])
    out = _combine(x1, o4, top_gate, final_norm_g.reshape(1, D_MODEL))
    return out.reshape(B, S, D_MODEL)
```

```python
import functools
import math

import numpy as np
import jax
import jax.numpy as jnp
from jax import lax
from jax.experimental import pallas as pl
from jax.experimental.pallas import tpu as pltpu
from jax.experimental.pallas import tpu_sc as plsc

F32 = jnp.float32
BF16 = jnp.bfloat16

D_MODEL = 1024
NORM_EPS = 1e-5
DA_HEADS = 4
DA_QK_DIM = 64
DA_V_DIM = 128
DA_WIDTH = DA_HEADS * DA_V_DIM
REL_BUCKETS = 32
REL_MAX_DIST = 128
ML_HEADS = 4
ML_QK_DIM = 64
ML_V_DIM = 128
ML_WIDTH = ML_HEADS * ML_V_DIM
ML_QK_COLS = 2 * ML_HEADS * ML_QK_DIM
ML_CONV = 4
N_EXPERTS = 32
TOP_K = 4
D_FF = D_MODEL
SWIGLU_LIMIT = 7.0
SWIGLU_ALPHA = 1.702

N_MAIN = 3 * DA_WIDTH + ML_QK_COLS + 2 * ML_WIDTH
LANES = 128
SUBLANES = 8
MASK_NEG = -1e30

TM_PROJ = 512
TQ = 256
ML_CHUNK = 256
MOE_ROWS = 512
TM_COMBINE = 256
ROW_PARTS = 2
PART_W = D_MODEL // (2 * ROW_PARTS)
SC_WINDOW = 128
MOE_GROUPS = 4
VMEM_LIMIT = 56 * 1024 * 1024


def _bucket_lower_bounds():
    n = np.arange(0, 4 * REL_MAX_DIST)
    max_exact = REL_BUCKETS // 2
    nf = np.maximum(n, 1).astype(np.float32)
    large = max_exact + (np.log(nf / np.float32(max_exact)) / np.float32(math.log(REL_MAX_DIST / max_exact))
                         * np.float32(REL_BUCKETS - max_exact)).astype(np.int32)
    large = np.minimum(large, REL_BUCKETS - 1)
    bucket = np.where(n < max_exact, n, large)
    return [int(np.argmax(bucket >= b)) for b in range(REL_BUCKETS)]


BUCKET_LO = _bucket_lower_bounds()
assert BUCKET_LO[-1] <= TQ, "keys two blocks away must all sit in the last bucket"


def _rms(x, g):
    return x * lax.rsqrt(jnp.mean(x * x, axis=-1, keepdims=True) + NORM_EPS) * g


def _pack_rows(v):
    bits = lax.bitcast_convert_type(v.astype(BF16).astype(F32), jnp.uint32)
    return [bits[:, (2 * j + 1) * PART_W:(2 * j + 2) * PART_W] | (bits[:, 2 * j * PART_W:(2 * j + 1) * PART_W] >> 16)
            for j in range(ROW_PARTS)]


def _unpack_rows(pieces):
    cols = []
    for w in pieces:
        cols.append(lax.bitcast_convert_type(w << 16, F32))
        cols.append(lax.bitcast_convert_type(w & jnp.uint32(0xFFFF0000), F32))
    return jnp.concatenate(cols, axis=1)


def _log_sigmoid(x):
    return jnp.minimum(x, 0.0) - jnp.log1p(jnp.exp(-jnp.abs(x)))


def _inproj_kernel(x_ref, g_ref, w_ref, wg_ref, o_ref, og_ref):
    hb = _rms(x_ref[...], g_ref[...]).astype(BF16)
    for n in range(N_MAIN // 512):
        cols = slice(n * 512, (n + 1) * 512)
        o_ref[:, cols] = jnp.dot(hb, w_ref[:, cols], preferred_element_type=F32).astype(BF16)
    og_ref[...] = jnp.dot(hb, wg_ref[...], preferred_element_type=F32)


def _inproj(x2, g, w_main, w_gate):
    T = x2.shape[0]
    return pl.pallas_call(
        _inproj_kernel,
        grid=(T // TM_PROJ,),
        in_specs=[
            pl.BlockSpec((TM_PROJ, D_MODEL), lambda i: (i, 0)),
            pl.BlockSpec((1, D_MODEL), lambda i: (0, 0)),
            pl.BlockSpec((D_MODEL, N_MAIN), lambda i: (0, 0)),
            pl.BlockSpec((D_MODEL, 2 * LANES), lambda i: (0, 0)),
        ],
        out_specs=[
            pl.BlockSpec((TM_PROJ, N_MAIN), lambda i: (i, 0)),
            pl.BlockSpec((TM_PROJ, 2 * LANES), lambda i: (i, 0)),
        ],
        out_shape=[jax.ShapeDtypeStruct((T, N_MAIN), BF16), jax.ShapeDtypeStruct((T, 2 * LANES), F32)],
        compiler_params=pltpu.CompilerParams(dimension_semantics=("parallel",), vmem_limit_bytes=VMEM_LIMIT),
        name="inproj",
    )(x2, g, w_main, w_gate)


def _relbias_kernel(tbl_ref, o_ref):
    h = pl.program_id(0)
    r = lax.broadcasted_iota(jnp.int32, (2 * TQ, 2 * TQ), 0) & (TQ - 1)
    c = lax.broadcasted_iota(jnp.int32, (2 * TQ, 2 * TQ), 1)
    rel = r + TQ - c
    val = jnp.full((2 * TQ, 2 * TQ), tbl_ref[0, h], F32)
    for b in range(1, REL_BUCKETS):
        val = jnp.where(rel >= BUCKET_LO[b], tbl_ref[b, h], val)
    o_ref[0] = jnp.where(rel >= 0, val - tbl_ref[REL_BUCKETS - 1, h], MASK_NEG)


def _relbias(table):
    return pl.pallas_call(
        _relbias_kernel,
        grid=(DA_HEADS,),
        in_specs=[pl.BlockSpec(memory_space=pltpu.SMEM)],
        out_specs=pl.BlockSpec((1, 2 * TQ, 2 * TQ), lambda h: (h, 0, 0)),
        out_shape=jax.ShapeDtypeStruct((DA_HEADS, 2 * TQ, 2 * TQ), F32),
        compiler_params=pltpu.CompilerParams(dimension_semantics=("parallel",)),
        name="relbias",
    )(table)


def _attn_kernel(lam_init, lam_ref, q_ref, k_ref, v_ref, bias_ref, g_ref, o_ref, acc_ref, m_ref, l_ref):
    i = pl.program_id(2)
    q = q_ref[...] * jnp.asarray(DA_QK_DIM ** -0.5, BF16)
    lane = lax.broadcasted_iota(jnp.int32, q.shape, 1)
    zero = jnp.zeros_like(q)
    qq = jnp.concatenate([jnp.where(lane < DA_QK_DIM, q, zero), jnp.where(lane >= DA_QK_DIM, q, zero)], axis=0)

    def rows(j, nblk):
        start = j * TQ if isinstance(j, int) else pl.multiple_of(j * TQ, TQ)
        return pl.ds(start, nblk * TQ)

    def update(j, nblk, bias):
        s = lax.dot_general(qq, k_ref[rows(j, nblk), :], (((1,), (1,)), ((), ())), preferred_element_type=F32)
        if bias is not None:
            s = s + bias
        tiles = [s[:, t * LANES:(t + 1) * LANES] for t in range(nblk * TQ // LANES)]
        m_old = m_ref[...]
        m_new = jnp.maximum(m_old, jnp.max(functools.reduce(jnp.maximum, tiles), axis=-1, keepdims=True))
        alpha = jnp.exp(m_old - m_new)
        ps = [jnp.exp(t - m_new) for t in tiles]
        p = jnp.concatenate(ps, axis=1).astype(BF16)
        l_ref[...] = alpha * l_ref[...] + jnp.sum(functools.reduce(jnp.add, ps), axis=-1, keepdims=True)
        acc_ref[...] = alpha * acc_ref[...] + jnp.dot(p, v_ref[rows(j, nblk), :], preferred_element_type=F32)
        m_ref[...] = m_new

    m_ref[...] = jnp.full_like(m_ref, MASK_NEG)
    l_ref[...] = jnp.zeros_like(l_ref)
    acc_ref[...] = jnp.zeros_like(acc_ref)

    @pl.when(i == 0)
    def _():
        update(0, 1, bias_ref[0, :, TQ:])

    @pl.when(i >= 1)
    def _():
        update(i - 1, 2, bias_ref[0])

    n_far = jnp.maximum(i - 1, 0)
    odd = n_far % 2

    @pl.when(odd == 1)
    def _():
        update(0, 1, None)

    def far_pair(jj, carry):
        update(odd + 2 * jj, 2, None)
        return carry

    lax.fori_loop(0, n_far // 2, far_pair, 0)

    lam_v = lam_ref[...]
    lam = (jnp.exp(jnp.sum(lam_v[0:1] * lam_v[1:2], axis=-1, keepdims=True))
           - jnp.exp(jnp.sum(lam_v[2:3] * lam_v[3:4], axis=-1, keepdims=True)) + lam_init)
    o = acc_ref[...] / l_ref[...]
    out = o[:TQ] - lam * o[TQ:]
    o_ref[...] = (_rms(out, g_ref[0]) * (1.0 - lam_init)).astype(BF16)


def _attention(proj, bias, lam_vecs, norm_g, B, S, lam_init):
    nq = S // TQ
    q_off, k_off, v_off = 0, DA_HEADS, 2 * DA_HEADS
    return pl.pallas_call(
        functools.partial(_attn_kernel, lam_init),
        grid=(B, DA_HEADS, nq),
        in_specs=[
            pl.BlockSpec((4, DA_QK_DIM), lambda b, h, i: (0, 0)),
            pl.BlockSpec((TQ, LANES), lambda b, h, i: (b * nq + i, q_off + h)),
            pl.BlockSpec((S, LANES), lambda b, h, i: (b, k_off + h)),
            pl.BlockSpec((S, LANES), lambda b, h, i: (b, v_off + h)),
            pl.BlockSpec((1, 2 * TQ, 2 * TQ), lambda b, h, i: (h, 0, 0)),
            pl.BlockSpec((1, 1, LANES), lambda b, h, i: (h, 0, 0)),
        ],
        out_specs=pl.BlockSpec((TQ, LANES), lambda b, h, i: (b * nq + i, h)),
        out_shape=jax.ShapeDtypeStruct((B * S, DA_WIDTH), BF16),
        scratch_shapes=[
            pltpu.VMEM((2 * TQ, DA_V_DIM), F32),
            pltpu.VMEM((2 * TQ, LANES), F32),
            pltpu.VMEM((2 * TQ, LANES), F32),
        ],
        compiler_params=pltpu.CompilerParams(
            dimension_semantics=("parallel", "parallel", "arbitrary"), vmem_limit_bytes=VMEM_LIMIT),
        name="diff_attention",
    )(lam_vecs, proj, proj, proj, bias, norm_g)


def _prefix_sum(x, axis):
    n = x.shape[axis]
    idx = lax.broadcasted_iota(jnp.int32, x.shape, axis)
    d = 1
    while d < n:
        x = x + jnp.where(idx >= d, pltpu.roll(x, d, axis), 0.0)
        d *= 2
    return x


def _mlstm_kernel(qk_ref, v_ref, og_ref, g_ref, gt_ref, cw_ref, cb_ref, gb_ref, gbt_ref, ob_ref, ng_ref,
                  out_ref, prev_ref, cn_ref, m_ref):
    L = ML_CHUNK
    H = ML_HEADS
    QW = H * ML_QK_DIM

    @pl.when(pl.program_id(1) == 0)
    def _():
        prev_ref[...] = jnp.zeros_like(prev_ref)
        cn_ref[...] = jnp.zeros_like(cn_ref)
        m_ref[...] = jnp.zeros_like(m_ref)

    x = qk_ref[...].astype(F32)
    prev = prev_ref[...]
    row = lax.broadcasted_iota(jnp.int32, x.shape, 0)
    y = cb_ref[...] + cw_ref[ML_CONV - 1:ML_CONV, :] * x
    for d in range(1, ML_CONV):
        shifted = jnp.where(row < d, pltpu.roll(prev, d, 0), pltpu.roll(x, d, 0))
        y = y + cw_ref[ML_CONV - 1 - d:ML_CONV - d, :] * shifted
    prev_ref[...] = x
    qkc = y * jax.nn.sigmoid(y)
    q = qkc[:, :QW]
    k = qkc[:, QW:] * (ML_QK_DIM ** -0.5)
    kb = k.astype(BF16)

    G = g_ref[...]
    li_all = G[:, :LANES] + gb_ref[:, :LANES]
    b_all = _prefix_sum(_log_sigmoid(G[:, LANES:] + gb_ref[:, LANES:]), 0)
    GT = gt_ref[...] + gbt_ref[...]
    bT = _prefix_sum(_log_sigmoid(GT), 1)

    qlane = lax.broadcasted_iota(jnp.int32, (L, QW), 1) // ML_QK_DIM
    qz = jnp.concatenate([jnp.where(qlane == h, q, 0.0) for h in range(H)], axis=0).astype(BF16)
    s_all = lax.dot_general(qz, kb, (((1,), (1,)), ((), ())), preferred_element_type=F32)
    cn_old = cn_ref[...]
    qcn = jnp.dot(qz, cn_old.astype(BF16), preferred_element_type=F32)

    t_idx = lax.broadcasted_iota(jnp.int32, (L, L), 0)
    j_idx = lax.broadcasted_iota(jnp.int32, (L, L), 1)
    causal = j_idx <= t_idx
    m_old_row = m_ref[...]
    mlane = lax.broadcasted_iota(jnp.int32, (1, LANES), 1)
    crow = lax.broadcasted_iota(jnp.int32, (QW, 1), 0) // ML_QK_DIM
    m_new_row = m_old_row
    wk_exp = jnp.zeros((L, QW), F32)
    decay_col = jnp.zeros((QW, 1), F32)

    for h in range(H):
        li_c = li_all[:, h:h + 1]
        b_c = b_all[:, h:h + 1]
        li_r = GT[h:h + 1, :]
        b_r = bT[H + h:H + h + 1, :]
        m_prev = m_old_row[:, h:h + 1]

        dlog = jnp.where(causal, b_c - b_r + li_r, MASK_NEG)
        inter = b_c + m_prev
        m_t = jnp.maximum(jnp.max(dlog, axis=-1, keepdims=True), inter)
        w_intra = jnp.exp(dlog - m_t)
        w_inter = jnp.exp(inter - m_t)
        s = s_all[h * L:(h + 1) * L, :] * w_intra
        vh = v_ref[:, h * ML_V_DIM:(h + 1) * ML_V_DIM]
        num = (jnp.dot(s.astype(BF16), vh, preferred_element_type=F32)
               + w_inter * qcn[h * L:(h + 1) * L, :ML_V_DIM])
        den = jnp.sum(s, axis=-1, keepdims=True) + w_inter * qcn[h * L:(h + 1) * L, ML_V_DIM:]
        hval = num / jnp.maximum(jnp.abs(den), jnp.exp(-m_t))

        cols = slice(h * ML_V_DIM, (h + 1) * ML_V_DIM)
        gate = jax.nn.sigmoid(og_ref[:, cols].astype(F32) + ob_ref[:, cols])
        out_ref[:, cols] = (_rms(hval, ng_ref[:, cols]) * gate).astype(BF16)

        g_tot = b_c[L - 1:L, :]
        wlog = g_tot - b_c + li_c
        m_new = jnp.maximum(g_tot + m_prev, jnp.max(wlog, axis=0, keepdims=True))
        decay = jnp.exp(g_tot + m_prev - m_new)
        wk = jnp.exp(wlog - m_new)
        m_new_row = jnp.where(mlane == h, m_new, m_new_row)
        wk_exp = jnp.where(qlane == h, wk, wk_exp)
        decay_col = jnp.where(crow == h, decay, decay_col)

    kw = (k * wk_exp).astype(BF16)
    vext = jnp.concatenate([v_ref[...], jnp.ones((L, LANES), BF16)], axis=1)
    upd = lax.dot_general(kw, vext, (((0,), (0,)), ((), ())), preferred_element_type=F32)
    for h in range(H):
        rows = slice(h * ML_QK_DIM, (h + 1) * ML_QK_DIM)
        dh = decay_col[rows, :]
        cn_ref[rows, :ML_V_DIM] = dh * cn_old[rows, :ML_V_DIM] + upd[rows, h * ML_V_DIM:(h + 1) * ML_V_DIM]
        cn_ref[rows, ML_V_DIM:] = dh * cn_old[rows, ML_V_DIM:] + upd[rows, H * ML_V_DIM:]
    m_ref[...] = m_new_row


def _mlstm(proj, gates, gates_t, conv_w, conv_b, gate_b, gate_bt, o_b, norm_g, B, S):
    L = ML_CHUNK
    nc = S // L
    qk_blk = 3 * DA_WIDTH // ML_QK_COLS
    v_blk, o_blk = qk_blk + 1, qk_blk + 2
    full = lambda shape: pl.BlockSpec(shape, lambda b, c: (0,) * len(shape))
    return pl.pallas_call(
        _mlstm_kernel,
        grid=(B, nc),
        in_specs=[
            pl.BlockSpec((L, ML_QK_COLS), lambda b, c: (b * nc + c, qk_blk)),
            pl.BlockSpec((L, ML_WIDTH), lambda b, c: (b * nc + c, v_blk)),
            pl.BlockSpec((L, ML_WIDTH), lambda b, c: (b * nc + c, o_blk)),
            pl.BlockSpec((L, 2 * LANES), lambda b, c: (b * nc + c, 0)),
            pl.BlockSpec((2 * ML_HEADS, L), lambda b, c: (0, b * nc + c)),
            full((ML_CONV, ML_QK_COLS)),
            full((1, ML_QK_COLS)),
            full((1, 2 * LANES)),
            full((2 * ML_HEADS, 1)),
            full((1, ML_WIDTH)),
            full((1, ML_WIDTH)),
        ],
        out_specs=pl.BlockSpec((L, ML_WIDTH), lambda b, c: (b * nc + c, 0)),
        out_shape=jax.ShapeDtypeStruct((B * S, ML_WIDTH), BF16),
        scratch_shapes=[
            pltpu.VMEM((L, ML_QK_COLS), F32),
            pltpu.VMEM((ML_HEADS * ML_QK_DIM, 2 * ML_V_DIM), F32),
            pltpu.VMEM((1, LANES), F32),
        ],
        compiler_params=pltpu.CompilerParams(
            dimension_semantics=("parallel", "arbitrary"), vmem_limit_bytes=VMEM_LIMIT),
        name="mlstm",
    )(proj, proj, proj, gates, gates_t, conv_w, conv_b, gate_b, gate_bt, o_b, norm_g)


def _outproj_kernel(a_ref, m_ref, x_ref, wo_ref, g_ref, rw_ref, rb_ref, x1_ref, h2_ref, idx_ref, gate_ref):
    x1 = (x_ref[...]
          + jnp.dot(a_ref[...], wo_ref[:DA_WIDTH, :], preferred_element_type=F32)
          + jnp.dot(m_ref[...], wo_ref[DA_WIDTH:, :], preferred_element_type=F32))
    x1_ref[...] = x1
    h2 = _rms(x1, g_ref[...])
    for j, piece in enumerate(_pack_rows(h2)):
        h2_ref[j] = piece
    work = jnp.dot(h2.astype(BF16), rw_ref[...], preferred_element_type=F32) + rb_ref[...]
    lane = lax.broadcasted_iota(jnp.int32, work.shape, 1).astype(F32)
    idxs = jnp.zeros_like(work)
    vals = jnp.zeros_like(work)
    top = None
    for kk in range(TOP_K):
        mx = jnp.max(work, axis=-1, keepdims=True)
        am = jnp.min(jnp.where(work == mx, lane, float(LANES)), axis=-1, keepdims=True)
        idxs = jnp.where(lane == kk, am, idxs)
        vals = jnp.where(lane == kk, mx, vals)
        work = jnp.where(lane == am, -jnp.inf, work)
        if kk == 0:
            top = mx
    e = jnp.where(lane < TOP_K, jnp.exp(vals - top), 0.0)
    gate_ref[...] = e / jnp.sum(e, axis=-1, keepdims=True)
    idx_ref[...] = idxs.astype(jnp.int32)


def _outproj(a_out, m_out, x2, w_out, g, router_w, router_b):
    T = x2.shape[0]
    row = lambda w: pl.BlockSpec((TM_PROJ, w), lambda i: (i, 0))
    full = lambda r, c: pl.BlockSpec((r, c), lambda i: (0, 0))
    return pl.pallas_call(
        _outproj_kernel,
        grid=(T // TM_PROJ,),
        in_specs=[row(DA_WIDTH), row(ML_WIDTH), row(D_MODEL), full(D_MODEL, D_MODEL), full(1, D_MODEL),
                  full(D_MODEL, LANES), full(1, LANES)],
        out_specs=[row(D_MODEL), pl.BlockSpec((ROW_PARTS, TM_PROJ, PART_W), lambda i: (0, i, 0)),
                   row(LANES), row(LANES)],
        out_shape=[jax.ShapeDtypeStruct((T, D_MODEL), F32), jax.ShapeDtypeStruct((ROW_PARTS, T, PART_W), jnp.uint32),
                   jax.ShapeDtypeStruct((T, LANES), jnp.int32), jax.ShapeDtypeStruct((T, LANES), F32)],
        compiler_params=pltpu.CompilerParams(dimension_semantics=("parallel",), vmem_limit_bytes=VMEM_LIMIT),
        name="outproj_router",
    )(a_out, m_out, x2, w_out, g, router_w, router_b)


def _sc_mesh():
    return plsc.VectorSubcoreMesh(core_axis_name="core", subcore_axis_name="subcore")


def _sc_gather_rows(x, idx):
    n = idx.shape[1]

    @functools.partial(pl.kernel, out_type=jax.ShapeDtypeStruct((n, x.shape[1]), x.dtype), mesh=_sc_mesh())
    def gather_kernel(x_hbm, i_hbm, o_hbm):
        def body(i_vmem, o_vmem):
            pltpu.sync_copy(x_hbm.at[i_vmem.at[0]], o_vmem)

        pltpu.emit_pipeline(
            body,
            grid=(n // SC_WINDOW,),
            in_specs=[pl.BlockSpec((1, SC_WINDOW), lambda i: (0, i))],
            out_specs=[pl.BlockSpec((SC_WINDOW, x.shape[1]), lambda i: (i, 0))],
            core_axis_name=("core", "subcore"),
            dimension_semantics=(pltpu.PARALLEL,),
        )(i_hbm, o_hbm)

    return gather_kernel(x, idx)


def _sc_scatter_rows(y, idx):
    n = idx.shape[1]

    @functools.partial(pl.kernel, out_type=jax.ShapeDtypeStruct(y.shape, y.dtype), mesh=_sc_mesh())
    def scatter_kernel(y_hbm, i_hbm, o_hbm):
        def body(y_vmem, i_vmem):
            pltpu.sync_copy(y_vmem, o_hbm.at[i_vmem.at[0]])

        pltpu.emit_pipeline(
            body,
            grid=(n // SC_WINDOW,),
            in_specs=[pl.BlockSpec((SC_WINDOW, y.shape[1]), lambda i: (i, 0)),
                      pl.BlockSpec((1, SC_WINDOW), lambda i: (0, i))],
            out_specs=[],
            core_axis_name=("core", "subcore"),
            dimension_semantics=(pltpu.PARALLEL,),
        )(y_hbm, i_hbm)

    return scatter_kernel(y, idx)


def _expert_kernel(first_block, be_ref, nu_ref, x_ref, wgu_ref, bgu_ref, wd_ref, bd_ref, *rest):
    y_ref = rest[-1]
    blk = first_block + pl.program_id(0)

    @pl.when(blk < nu_ref[0])
    def _():
        xb = _unpack_rows([x_ref[j] for j in range(ROW_PARTS)]).astype(BF16)
        gu = jnp.dot(xb, wgu_ref[0], preferred_element_type=F32) + bgu_ref[0]
        glu = jnp.minimum(gu[:, :D_FF], SWIGLU_LIMIT)
        lin = jnp.clip(gu[:, D_FF:], -SWIGLU_LIMIT, SWIGLU_LIMIT)
        act = glu * jax.nn.sigmoid(SWIGLU_ALPHA * glu) * (lin + 1.0)
        y = jnp.dot(act.astype(BF16), wd_ref[0], preferred_element_type=F32) + bd_ref[0]
        for j, piece in enumerate(_pack_rows(y)):
            y_ref[j] = piece

    @pl.when(blk >= nu_ref[0])
    def _():
        y_ref[...] = jnp.zeros_like(y_ref)


def _experts(h2, block_expert, n_used, row_tok, row_dst, w_gu, b_gu, w_down, b_down):
    R = MOE_ROWS
    T = h2.shape[1]
    n_rows = row_tok.shape[0]
    g_rows = n_rows // MOE_GROUPS
    g_blocks = g_rows // R
    part = jnp.arange(ROW_PARTS, dtype=jnp.int32)[:, None]
    h2_flat = h2.reshape(ROW_PARTS * T, PART_W)
    xs = []
    for grp in range(MOE_GROUPS):
        tok = row_tok[grp * g_rows:(grp + 1) * g_rows]
        src_idx = (part * T + tok[None, :]).reshape(1, ROW_PARTS * g_rows)
        xs.append(_sc_gather_rows(h2_flat, src_idx).reshape(ROW_PARTS, g_rows, PART_W))

    def run_group(grp, ys):
        first = grp * g_blocks
        wspec = lambda shape: pl.BlockSpec(shape, lambda i, be, nu: (be[first + i], 0, 0))
        in_specs = [
            pl.BlockSpec((ROW_PARTS, R, PART_W), lambda i, be, nu: (0, i, 0)),
            wspec((1, D_MODEL, 2 * D_FF)), wspec((1, 1, 2 * D_FF)), wspec((1, D_FF, D_MODEL)), wspec((1, 1, D_MODEL)),
        ]
        operands = [block_expert, n_used, xs[grp], w_gu, b_gu, w_down, b_down]
        aliases = {}
        if ys is not None:
            in_specs.append(pl.BlockSpec(memory_space=pl.ANY))
            aliases = {len(operands): 0}
            operands.append(ys)
        return pl.pallas_call(
            functools.partial(_expert_kernel, first),
            grid_spec=pltpu.PrefetchScalarGridSpec(
                num_scalar_prefetch=2,
                grid=(g_blocks,),
                in_specs=in_specs,
                out_specs=pl.BlockSpec((ROW_PARTS, R, PART_W), lambda i, be, nu: (0, first + i, 0)),
            ),
            out_shape=jax.ShapeDtypeStruct((ROW_PARTS, n_rows, PART_W), jnp.uint32),
            input_output_aliases=aliases,
            compiler_params=pltpu.CompilerParams(dimension_semantics=("arbitrary",), vmem_limit_bytes=VMEM_LIMIT),
            name=f"experts_g{grp}",
        )(*operands)

    ys = None
    for grp in range(MOE_GROUPS):
        ys = run_group(grp, ys)
    dst_idx = (part * n_rows + row_dst[None, :]).reshape(1, ROW_PARTS * n_rows)
    out = _sc_scatter_rows(ys.reshape(ROW_PARTS * n_rows, PART_W), dst_idx)
    return out.reshape(ROW_PARTS, n_rows, PART_W)


def _combine_kernel(x1_ref, gate_ref, g_ref, *refs):
    o_refs, out_ref = refs[:TOP_K], refs[TOP_K]
    y = x1_ref[...]
    gt = gate_ref[...]
    for kk in range(TOP_K):
        y = y + gt[:, kk:kk + 1] * _unpack_rows([o_refs[kk][j] for j in range(ROW_PARTS)])
    out_ref[...] = _rms(y, g_ref[...])


def _combine(x1, o4, gates, g):
    T = x1.shape[0]
    tm = TM_COMBINE
    nt = T // tm
    slot_spec = lambda kk: pl.BlockSpec((ROW_PARTS, tm, PART_W), lambda i: (0, kk * nt + i, 0))
    return pl.pallas_call(
        _combine_kernel,
        grid=(nt,),
        in_specs=[
            pl.BlockSpec((tm, D_MODEL), lambda i: (i, 0)),
            pl.BlockSpec((tm, LANES), lambda i: (i, 0)),
            pl.BlockSpec((1, D_MODEL), lambda i: (0, 0)),
        ] + [slot_spec(kk) for kk in range(TOP_K)],
        out_specs=pl.BlockSpec((tm, D_MODEL), lambda i: (i, 0)),
        out_shape=jax.ShapeDtypeStruct((T, D_MODEL), F32),
        compiler_params=pltpu.CompilerParams(dimension_semantics=("parallel",), vmem_limit_bytes=VMEM_LIMIT),
        name="combine_norm",
    )(x1, gates, g, *([o4] * TOP_K))


def _dispatch_plan(top_idx, T):
    R = MOE_ROWS
    P = T * TOP_K
    e_flat = top_idx.reshape(P)
    order = jnp.argsort(e_flat).astype(jnp.int32)
    counts = jnp.sum(e_flat[:, None] == jnp.arange(N_EXPERTS, dtype=jnp.int32)[None, :], axis=0, dtype=jnp.int32)
    padded = (counts + R - 1) // R * R
    start = jnp.cumsum(counts) - counts
    pend = jnp.cumsum(padded)
    pstart = pend - padded
    n_blocks = P // R + N_EXPERTS
    block_row0 = jnp.arange(n_blocks, dtype=jnp.int32) * R
    block_expert = jnp.minimum(jnp.sum(pend[None, :] <= block_row0[:, None], axis=1, dtype=jnp.int32), N_EXPERTS - 1)
    within = block_row0 - pstart[block_expert]
    n_valid = jnp.where(block_row0 < pend[-1], jnp.clip(counts[block_expert] - within, 0, R), 0).astype(jnp.int32)
    r_in = jnp.arange(R, dtype=jnp.int32)[None, :]
    src = jnp.clip((start[block_expert] + within)[:, None] + r_in, 0, P - 1)
    is_valid = (r_in < n_valid[:, None]).reshape(-1)
    pair = jnp.where(is_valid, order[src].reshape(-1), 0)
    row_tok = pair // TOP_K
    pad_rank = jnp.cumsum(1 - is_valid.astype(jnp.int32)) - 1
    row_dst = jnp.where(is_valid, (pair % TOP_K) * T + row_tok, P + pad_rank)
    n_used = (pend[-1] // R).reshape(1).astype(jnp.int32)
    return block_expert, n_used, row_tok, row_dst


def _layer(x2, B, S, layer, norm_attn_g, w_in, rel_bias_table, lam_vecs, diff_norm_g, conv_w, conv_b, gate_b,
           o_b, ml_norm_g, w_out, norm_ffn_g, router_w, router_b, w_gu, b_gu, w_down, b_down):
    T = B * S
    H = ML_HEADS
    w_main = w_in[:, :N_MAIN].astype(BF16)
    wg = w_in[:, N_MAIN:]
    w_gate = jnp.zeros((D_MODEL, 2 * LANES), F32).at[:, :H].set(wg[:, :H]).at[:, LANES:LANES + H].set(wg[:, H:])
    proj, gates = _inproj(x2, norm_attn_g.reshape(1, D_MODEL), w_main, w_gate.astype(BF16))

    lam_init = 0.8 - 0.6 * math.exp(-0.3 * layer)
    bias = _relbias(rel_bias_table.astype(F32))
    a_out = _attention(proj, bias, lam_vecs, diff_norm_g.reshape(DA_HEADS, 1, DA_V_DIM), B, S, lam_init)

    gates_t = jnp.concatenate([gates[:, :H], gates[:, LANES:LANES + H]], axis=1).T
    gate_b_pad = jnp.zeros((1, 2 * LANES), F32).at[0, :H].set(gate_b[:H]).at[0, LANES:LANES + H].set(gate_b[H:])
    m_out = _mlstm(proj, gates, gates_t, conv_w, conv_b.reshape(1, -1), gate_b_pad, gate_b.reshape(2 * H, 1),
                   o_b.reshape(1, -1), ml_norm_g.reshape(1, -1), B, S)

    rw = jnp.zeros((D_MODEL, LANES), F32).at[:, :N_EXPERTS].set(router_w).astype(BF16)
    rb = jnp.full((1, LANES), MASK_NEG, F32).at[0, :N_EXPERTS].set(router_b)
    x1, h2, top_idx, top_gate = _outproj(a_out, m_out, x2, w_out.astype(BF16), norm_ffn_g.reshape(1, D_MODEL), rw, rb)

    block_expert, n_used, row_tok, row_dst = _dispatch_plan(top_idx[:, :TOP_K], T)
    o4 = _experts(h2, block_expert, n_used, row_tok, row_dst, w_gu.astype(BF16),
                  b_gu.reshape(N_EXPERTS, 1, 2 * D_FF), w_down.astype(BF16),
                  b_down.reshape(N_EXPERTS, 1, D_MODEL))
    return x1, o4, top_gate


def kernel(x, norm_attn_g, w_in, rel_bias_table, lambda_q1, lambda_k1, lambda_q2, lambda_k2, diff_norm_g,
           mlstm_conv_w, mlstm_conv_b, mlstm_gate_b, mlstm_o_b, mlstm_norm_g, w_out, norm_ffn_g, router_w,
           router_b, expert_w_gu, expert_b_gu, expert_w_down, expert_b_down, final_norm_g):
    B, S, _ = x.shape
    depth = w_in.shape[0]
    assert depth == 1, "the combine kernel fuses the last layer's MoE residual with the final norm"
    assert S % TQ == 0 and S % ML_CHUNK == 0 and (B * S) % TM_PROJ == 0 and (B * S * TOP_K) % MOE_ROWS == 0
    x2 = x.reshape(B * S, D_MODEL)
    l = 0
    lam_vecs = jnp.stack([lambda_q1[l], lambda_k1[l], lambda_q2[l], lambda_k2[l]])
    x1, o4, top_gate = _layer(
        x2, B, S, l, norm_attn_g[l], w_in[l], rel_bias_table, lam_vecs, diff_norm_g[l], mlstm_conv_w[l],
        mlstm_conv_b[l], mlstm_gate_b[l], mlstm_o_b[l], mlstm_norm_g[l], w_out[l], norm_ffn_g[l], router_w[l],
        router_b[l], expert_w_gu[l], expert_b_gu[l], expert_w_down[l], expert_b_down[l])
    out = _combine(x1, o4, top_gate, final_norm_g.reshape(1, D_MODEL))
    return out.reshape(B, S, D_MODEL)
```

```python
import functools
import math

import numpy as np
import jax
import jax.numpy as jnp
from jax import lax
from jax.experimental import pallas as pl
from jax.experimental.pallas import tpu as pltpu
from jax.experimental.pallas import tpu_sc as plsc

F32 = jnp.float32
BF16 = jnp.bfloat16

D_MODEL = 1024
NORM_EPS = 1e-5
DA_HEADS = 4
DA_QK_DIM = 64
DA_V_DIM = 128
DA_WIDTH = DA_HEADS * DA_V_DIM
REL_BUCKETS = 32
REL_MAX_DIST = 128
ML_HEADS = 4
ML_QK_DIM = 64
ML_V_DIM = 128
ML_WIDTH = ML_HEADS * ML_V_DIM
ML_QK_COLS = 2 * ML_HEADS * ML_QK_DIM
ML_CONV = 4
N_EXPERTS = 32
TOP_K = 4
D_FF = D_MODEL
SWIGLU_LIMIT = 7.0
SWIGLU_ALPHA = 1.702

N_MAIN = 3 * DA_WIDTH + ML_QK_COLS + 2 * ML_WIDTH
LANES = 128
SUBLANES = 8
MASK_NEG = -1e30

TM_PROJ = 512
TQ = 256
ML_CHUNK = 256
MOE_ROWS = 512
TM_COMBINE = 256
ROW_PARTS = 2
PART_W = D_MODEL // (2 * ROW_PARTS)
SC_WINDOW = 128
MOE_GROUPS = 8
VMEM_LIMIT = 56 * 1024 * 1024


def _bucket_lower_bounds():
    n = np.arange(0, 4 * REL_MAX_DIST)
    max_exact = REL_BUCKETS // 2
    nf = np.maximum(n, 1).astype(np.float32)
    large = max_exact + (np.log(nf / np.float32(max_exact)) / np.float32(math.log(REL_MAX_DIST / max_exact))
                         * np.float32(REL_BUCKETS - max_exact)).astype(np.int32)
    large = np.minimum(large, REL_BUCKETS - 1)
    bucket = np.where(n < max_exact, n, large)
    return [int(np.argmax(bucket >= b)) for b in range(REL_BUCKETS)]


BUCKET_LO = _bucket_lower_bounds()
assert BUCKET_LO[-1] <= TQ, "keys two blocks away must all sit in the last bucket"


def _rms(x, g):
    return x * lax.rsqrt(jnp.mean(x * x, axis=-1, keepdims=True) + NORM_EPS) * g


def _pack_rows(v):
    bits = lax.bitcast_convert_type(v.astype(BF16).astype(F32), jnp.uint32)
    return [bits[:, (2 * j + 1) * PART_W:(2 * j + 2) * PART_W] | (bits[:, 2 * j * PART_W:(2 * j + 1) * PART_W] >> 16)
            for j in range(ROW_PARTS)]


def _unpack_rows(pieces):
    cols = []
    for w in pieces:
        cols.append(lax.bitcast_convert_type(w << 16, F32))
        cols.append(lax.bitcast_convert_type(w & jnp.uint32(0xFFFF0000), F32))
    return jnp.concatenate(cols, axis=1)


def _log_sigmoid(x):
    return jnp.minimum(x, 0.0) - jnp.log1p(jnp.exp(-jnp.abs(x)))


def _inproj_kernel(x_ref, g_ref, w_ref, wg_ref, o_ref, og_ref):
    hb = _rms(x_ref[...], g_ref[...]).astype(BF16)
    for n in range(N_MAIN // 512):
        cols = slice(n * 512, (n + 1) * 512)
        o_ref[:, cols] = jnp.dot(hb, w_ref[:, cols], preferred_element_type=F32).astype(BF16)
    og_ref[...] = jnp.dot(hb, wg_ref[...], preferred_element_type=F32)


def _inproj(x2, g, w_main, w_gate):
    T = x2.shape[0]
    return pl.pallas_call(
        _inproj_kernel,
        grid=(T // TM_PROJ,),
        in_specs=[
            pl.BlockSpec((TM_PROJ, D_MODEL), lambda i: (i, 0)),
            pl.BlockSpec((1, D_MODEL), lambda i: (0, 0)),
            pl.BlockSpec((D_MODEL, N_MAIN), lambda i: (0, 0)),
            pl.BlockSpec((D_MODEL, 2 * LANES), lambda i: (0, 0)),
        ],
        out_specs=[
            pl.BlockSpec((TM_PROJ, N_MAIN), lambda i: (i, 0)),
            pl.BlockSpec((TM_PROJ, 2 * LANES), lambda i: (i, 0)),
        ],
        out_shape=[jax.ShapeDtypeStruct((T, N_MAIN), BF16), jax.ShapeDtypeStruct((T, 2 * LANES), F32)],
        compiler_params=pltpu.CompilerParams(dimension_semantics=("parallel",), vmem_limit_bytes=VMEM_LIMIT),
        name="inproj",
    )(x2, g, w_main, w_gate)


def _relbias_kernel(tbl_ref, o_ref):
    h = pl.program_id(0)
    r = lax.broadcasted_iota(jnp.int32, (2 * TQ, 2 * TQ), 0) & (TQ - 1)
    c = lax.broadcasted_iota(jnp.int32, (2 * TQ, 2 * TQ), 1)
    rel = r + TQ - c
    val = jnp.full((2 * TQ, 2 * TQ), tbl_ref[0, h], F32)
    for b in range(1, REL_BUCKETS):
        val = jnp.where(rel >= BUCKET_LO[b], tbl_ref[b, h], val)
    o_ref[0] = jnp.where(rel >= 0, val - tbl_ref[REL_BUCKETS - 1, h], MASK_NEG)


def _relbias(table):
    return pl.pallas_call(
        _relbias_kernel,
        grid=(DA_HEADS,),
        in_specs=[pl.BlockSpec(memory_space=pltpu.SMEM)],
        out_specs=pl.BlockSpec((1, 2 * TQ, 2 * TQ), lambda h: (h, 0, 0)),
        out_shape=jax.ShapeDtypeStruct((DA_HEADS, 2 * TQ, 2 * TQ), F32),
        compiler_params=pltpu.CompilerParams(dimension_semantics=("parallel",)),
        name="relbias",
    )(table)


def _attn_kernel(lam_init, lam_ref, q_ref, k_ref, v_ref, bias_ref, g_ref, o_ref, acc_ref, m_ref, l_ref):
    i = pl.program_id(2)
    q = q_ref[...] * jnp.asarray(DA_QK_DIM ** -0.5, BF16)
    lane = lax.broadcasted_iota(jnp.int32, q.shape, 1)
    zero = jnp.zeros_like(q)
    qq = jnp.concatenate([jnp.where(lane < DA_QK_DIM, q, zero), jnp.where(lane >= DA_QK_DIM, q, zero)], axis=0)

    def rows(j, nblk):
        start = j * TQ if isinstance(j, int) else pl.multiple_of(j * TQ, TQ)
        return pl.ds(start, nblk * TQ)

    def update(j, nblk, bias):
        s = lax.dot_general(qq, k_ref[rows(j, nblk), :], (((1,), (1,)), ((), ())), preferred_element_type=F32)
        if bias is not None:
            s = s + bias
        tiles = [s[:, t * LANES:(t + 1) * LANES] for t in range(nblk * TQ // LANES)]
        m_old = m_ref[...]
        m_new = jnp.maximum(m_old, jnp.max(functools.reduce(jnp.maximum, tiles), axis=-1, keepdims=True))
        alpha = jnp.exp(m_old - m_new)
        ps = [jnp.exp(t - m_new) for t in tiles]
        p = jnp.concatenate(ps, axis=1).astype(BF16)
        l_ref[...] = alpha * l_ref[...] + jnp.sum(functools.reduce(jnp.add, ps), axis=-1, keepdims=True)
        acc_ref[...] = alpha * acc_ref[...] + jnp.dot(p, v_ref[rows(j, nblk), :], preferred_element_type=F32)
        m_ref[...] = m_new

    m_ref[...] = jnp.full_like(m_ref, MASK_NEG)
    l_ref[...] = jnp.zeros_like(l_ref)
    acc_ref[...] = jnp.zeros_like(acc_ref)

    @pl.when(i == 0)
    def _():
        update(0, 1, bias_ref[0, :, TQ:])

    @pl.when(i >= 1)
    def _():
        update(i - 1, 2, bias_ref[0])

    n_far = jnp.maximum(i - 1, 0)
    odd = n_far % 2

    @pl.when(odd == 1)
    def _():
        update(0, 1, None)

    def far_pair(jj, carry):
        update(odd + 2 * jj, 2, None)
        return carry

    lax.fori_loop(0, n_far // 2, far_pair, 0)

    lam_v = lam_ref[...]
    lam = (jnp.exp(jnp.sum(lam_v[0:1] * lam_v[1:2], axis=-1, keepdims=True))
           - jnp.exp(jnp.sum(lam_v[2:3] * lam_v[3:4], axis=-1, keepdims=True)) + lam_init)
    o = acc_ref[...] / l_ref[...]
    out = o[:TQ] - lam * o[TQ:]
    o_ref[...] = (_rms(out, g_ref[0]) * (1.0 - lam_init)).astype(BF16)


def _attention(proj, bias, lam_vecs, norm_g, B, S, lam_init):
    nq = S // TQ
    q_off, k_off, v_off = 0, DA_HEADS, 2 * DA_HEADS
    return pl.pallas_call(
        functools.partial(_attn_kernel, lam_init),
        grid=(B, DA_HEADS, nq),
        in_specs=[
            pl.BlockSpec((4, DA_QK_DIM), lambda b, h, i: (0, 0)),
            pl.BlockSpec((TQ, LANES), lambda b, h, i: (b * nq + i, q_off + h)),
            pl.BlockSpec((S, LANES), lambda b, h, i: (b, k_off + h)),
            pl.BlockSpec((S, LANES), lambda b, h, i: (b, v_off + h)),
            pl.BlockSpec((1, 2 * TQ, 2 * TQ), lambda b, h, i: (h, 0, 0)),
            pl.BlockSpec((1, 1, LANES), lambda b, h, i: (h, 0, 0)),
        ],
        out_specs=pl.BlockSpec((TQ, LANES), lambda b, h, i: (b * nq + i, h)),
        out_shape=jax.ShapeDtypeStruct((B * S, DA_WIDTH), BF16),
        scratch_shapes=[
            pltpu.VMEM((2 * TQ, DA_V_DIM), F32),
            pltpu.VMEM((2 * TQ, LANES), F32),
            pltpu.VMEM((2 * TQ, LANES), F32),
        ],
        compiler_params=pltpu.CompilerParams(
            dimension_semantics=("parallel", "parallel", "arbitrary"), vmem_limit_bytes=VMEM_LIMIT),
        name="diff_attention",
    )(lam_vecs, proj, proj, proj, bias, norm_g)


def _prefix_sum(x, axis):
    n = x.shape[axis]
    idx = lax.broadcasted_iota(jnp.int32, x.shape, axis)
    d = 1
    while d < n:
        x = x + jnp.where(idx >= d, pltpu.roll(x, d, axis), 0.0)
        d *= 2
    return x


def _mlstm_kernel(qk_ref, v_ref, og_ref, g_ref, gt_ref, cw_ref, cb_ref, gb_ref, gbt_ref, ob_ref, ng_ref,
                  out_ref, prev_ref, cn_ref, m_ref):
    L = ML_CHUNK
    H = ML_HEADS
    QW = H * ML_QK_DIM

    @pl.when(pl.program_id(1) == 0)
    def _():
        prev_ref[...] = jnp.zeros_like(prev_ref)
        cn_ref[...] = jnp.zeros_like(cn_ref)
        m_ref[...] = jnp.zeros_like(m_ref)

    x = qk_ref[...].astype(F32)
    prev = prev_ref[...]
    row = lax.broadcasted_iota(jnp.int32, x.shape, 0)
    y = cb_ref[...] + cw_ref[ML_CONV - 1:ML_CONV, :] * x
    for d in range(1, ML_CONV):
        shifted = jnp.where(row < d, pltpu.roll(prev, d, 0), pltpu.roll(x, d, 0))
        y = y + cw_ref[ML_CONV - 1 - d:ML_CONV - d, :] * shifted
    prev_ref[...] = x
    qkc = y * jax.nn.sigmoid(y)
    q = qkc[:, :QW]
    k = qkc[:, QW:] * (ML_QK_DIM ** -0.5)
    kb = k.astype(BF16)

    G = g_ref[...]
    li_all = G[:, :LANES] + gb_ref[:, :LANES]
    b_all = _prefix_sum(_log_sigmoid(G[:, LANES:] + gb_ref[:, LANES:]), 0)
    GT = gt_ref[...] + gbt_ref[...]
    bT = _prefix_sum(_log_sigmoid(GT), 1)

    qlane = lax.broadcasted_iota(jnp.int32, (L, QW), 1) // ML_QK_DIM
    qz = jnp.concatenate([jnp.where(qlane == h, q, 0.0) for h in range(H)], axis=0).astype(BF16)
    s_all = lax.dot_general(qz, kb, (((1,), (1,)), ((), ())), preferred_element_type=F32)
    cn_old = cn_ref[...]
    qcn = jnp.dot(qz, cn_old.astype(BF16), preferred_element_type=F32)

    t_idx = lax.broadcasted_iota(jnp.int32, (L, L), 0)
    j_idx = lax.broadcasted_iota(jnp.int32, (L, L), 1)
    causal = j_idx <= t_idx
    m_old_row = m_ref[...]
    mlane = lax.broadcasted_iota(jnp.int32, (1, LANES), 1)
    crow = lax.broadcasted_iota(jnp.int32, (QW, 1), 0) // ML_QK_DIM
    m_new_row = m_old_row
    wk_exp = jnp.zeros((L, QW), F32)
    decay_col = jnp.zeros((QW, 1), F32)

    for h in range(H):
        li_c = li_all[:, h:h + 1]
        b_c = b_all[:, h:h + 1]
        li_r = GT[h:h + 1, :]
        b_r = bT[H + h:H + h + 1, :]
        m_prev = m_old_row[:, h:h + 1]

        dlog = jnp.where(causal, b_c - b_r + li_r, MASK_NEG)
        inter = b_c + m_prev
        m_t = jnp.maximum(jnp.max(dlog, axis=-1, keepdims=True), inter)
        w_intra = jnp.exp(dlog - m_t)
        w_inter = jnp.exp(inter - m_t)
        s = s_all[h * L:(h + 1) * L, :] * w_intra
        vh = v_ref[:, h * ML_V_DIM:(h + 1) * ML_V_DIM]
        num = (jnp.dot(s.astype(BF16), vh, preferred_element_type=F32)
               + w_inter * qcn[h * L:(h + 1) * L, :ML_V_DIM])
        den = jnp.sum(s, axis=-1, keepdims=True) + w_inter * qcn[h * L:(h + 1) * L, ML_V_DIM:]
        hval = num / jnp.maximum(jnp.abs(den), jnp.exp(-m_t))

        cols = slice(h * ML_V_DIM, (h + 1) * ML_V_DIM)
        gate = jax.nn.sigmoid(og_ref[:, cols].astype(F32) + ob_ref[:, cols])
        out_ref[:, cols] = (_rms(hval, ng_ref[:, cols]) * gate).astype(BF16)

        g_tot = b_c[L - 1:L, :]
        wlog = g_tot - b_c + li_c
        m_new = jnp.maximum(g_tot + m_prev, jnp.max(wlog, axis=0, keepdims=True))
        decay = jnp.exp(g_tot + m_prev - m_new)
        wk = jnp.exp(wlog - m_new)
        m_new_row = jnp.where(mlane == h, m_new, m_new_row)
        wk_exp = jnp.where(qlane == h, wk, wk_exp)
        decay_col = jnp.where(crow == h, decay, decay_col)

    kw = (k * wk_exp).astype(BF16)
    vext = jnp.concatenate([v_ref[...], jnp.ones((L, LANES), BF16)], axis=1)
    upd = lax.dot_general(kw, vext, (((0,), (0,)), ((), ())), preferred_element_type=F32)
    for h in range(H):
        rows = slice(h * ML_QK_DIM, (h + 1) * ML_QK_DIM)
        dh = decay_col[rows, :]
        cn_ref[rows, :ML_V_DIM] = dh * cn_old[rows, :ML_V_DIM] + upd[rows, h * ML_V_DIM:(h + 1) * ML_V_DIM]
        cn_ref[rows, ML_V_DIM:] = dh * cn_old[rows, ML_V_DIM:] + upd[rows, H * ML_V_DIM:]
    m_ref[...] = m_new_row


def _mlstm(proj, gates, gates_t, conv_w, conv_b, gate_b, gate_bt, o_b, norm_g, B, S):
    L = ML_CHUNK
    nc = S // L
    qk_blk = 3 * DA_WIDTH // ML_QK_COLS
    v_blk, o_blk = qk_blk + 1, qk_blk + 2
    full = lambda shape: pl.BlockSpec(shape, lambda b, c: (0,) * len(shape))
    return pl.pallas_call(
        _mlstm_kernel,
        grid=(B, nc),
        in_specs=[
            pl.BlockSpec((L, ML_QK_COLS), lambda b, c: (b * nc + c, qk_blk)),
            pl.BlockSpec((L, ML_WIDTH), lambda b, c: (b * nc + c, v_blk)),
            pl.BlockSpec((L, ML_WIDTH), lambda b, c: (b * nc + c, o_blk)),
            pl.BlockSpec((L, 2 * LANES), lambda b, c: (b * nc + c, 0)),
            pl.BlockSpec((2 * ML_HEADS, L), lambda b, c: (0, b * nc + c)),
            full((ML_CONV, ML_QK_COLS)),
            full((1, ML_QK_COLS)),
            full((1, 2 * LANES)),
            full((2 * ML_HEADS, 1)),
            full((1, ML_WIDTH)),
            full((1, ML_WIDTH)),
        ],
        out_specs=pl.BlockSpec((L, ML_WIDTH), lambda b, c: (b * nc + c, 0)),
        out_shape=jax.ShapeDtypeStruct((B * S, ML_WIDTH), BF16),
        scratch_shapes=[
            pltpu.VMEM((L, ML_QK_COLS), F32),
            pltpu.VMEM((ML_HEADS * ML_QK_DIM, 2 * ML_V_DIM), F32),
            pltpu.VMEM((1, LANES), F32),
        ],
        compiler_params=pltpu.CompilerParams(
            dimension_semantics=("parallel", "arbitrary"), vmem_limit_bytes=VMEM_LIMIT),
        name="mlstm",
    )(proj, proj, proj, gates, gates_t, conv_w, conv_b, gate_b, gate_bt, o_b, norm_g)


def _outproj_kernel(a_ref, m_ref, x_ref, wo_ref, g_ref, rw_ref, rb_ref, x1_ref, h2_ref, idx_ref, gate_ref):
    x1 = (x_ref[...]
          + jnp.dot(a_ref[...], wo_ref[:DA_WIDTH, :], preferred_element_type=F32)
          + jnp.dot(m_ref[...], wo_ref[DA_WIDTH:, :], preferred_element_type=F32))
    x1_ref[...] = x1
    h2 = _rms(x1, g_ref[...])
    for j, piece in enumerate(_pack_rows(h2)):
        h2_ref[j] = piece
    work = jnp.dot(h2.astype(BF16), rw_ref[...], preferred_element_type=F32) + rb_ref[...]
    lane = lax.broadcasted_iota(jnp.int32, work.shape, 1).astype(F32)
    idxs = jnp.zeros_like(work)
    vals = jnp.zeros_like(work)
    top = None
    for kk in range(TOP_K):
        mx = jnp.max(work, axis=-1, keepdims=True)
        am = jnp.min(jnp.where(work == mx, lane, float(LANES)), axis=-1, keepdims=True)
        idxs = jnp.where(lane == kk, am, idxs)
        vals = jnp.where(lane == kk, mx, vals)
        work = jnp.where(lane == am, -jnp.inf, work)
        if kk == 0:
            top = mx
    e = jnp.where(lane < TOP_K, jnp.exp(vals - top), 0.0)
    gate_ref[...] = e / jnp.sum(e, axis=-1, keepdims=True)
    idx_ref[...] = idxs.astype(jnp.int32)


def _outproj(a_out, m_out, x2, w_out, g, router_w, router_b):
    T = x2.shape[0]
    row = lambda w: pl.BlockSpec((TM_PROJ, w), lambda i: (i, 0))
    full = lambda r, c: pl.BlockSpec((r, c), lambda i: (0, 0))
    return pl.pallas_call(
        _outproj_kernel,
        grid=(T // TM_PROJ,),
        in_specs=[row(DA_WIDTH), row(ML_WIDTH), row(D_MODEL), full(D_MODEL, D_MODEL), full(1, D_MODEL),
                  full(D_MODEL, LANES), full(1, LANES)],
        out_specs=[row(D_MODEL), pl.BlockSpec((ROW_PARTS, TM_PROJ, PART_W), lambda i: (0, i, 0)),
                   row(LANES), row(LANES)],
        out_shape=[jax.ShapeDtypeStruct((T, D_MODEL), F32), jax.ShapeDtypeStruct((ROW_PARTS, T, PART_W), jnp.uint32),
                   jax.ShapeDtypeStruct((T, LANES), jnp.int32), jax.ShapeDtypeStruct((T, LANES), F32)],
        compiler_params=pltpu.CompilerParams(dimension_semantics=("parallel",), vmem_limit_bytes=VMEM_LIMIT),
        name="outproj_router",
    )(a_out, m_out, x2, w_out, g, router_w, router_b)


def _sc_mesh():
    return plsc.VectorSubcoreMesh(core_axis_name="core", subcore_axis_name="subcore")


def _sc_gather_rows(x, idx):
    n = idx.shape[1]

    @functools.partial(pl.kernel, out_type=jax.ShapeDtypeStruct((n, x.shape[1]), x.dtype), mesh=_sc_mesh())
    def gather_kernel(x_hbm, i_hbm, o_hbm):
        def body(i_vmem, o_vmem):
            pltpu.sync_copy(x_hbm.at[i_vmem.at[0]], o_vmem)

        pltpu.emit_pipeline(
            body,
            grid=(n // SC_WINDOW,),
            in_specs=[pl.BlockSpec((1, SC_WINDOW), lambda i: (0, i))],
            out_specs=[pl.BlockSpec((SC_WINDOW, x.shape[1]), lambda i: (i, 0))],
            core_axis_name=("core", "subcore"),
            dimension_semantics=(pltpu.PARALLEL,),
        )(i_hbm, o_hbm)

    return gather_kernel(x, idx)


def _sc_scatter_rows(y, idx):
    n = idx.shape[1]

    @functools.partial(pl.kernel, out_type=jax.ShapeDtypeStruct(y.shape, y.dtype), mesh=_sc_mesh())
    def scatter_kernel(y_hbm, i_hbm, o_hbm):
        def body(y_vmem, i_vmem):
            pltpu.sync_copy(y_vmem, o_hbm.at[i_vmem.at[0]])

        pltpu.emit_pipeline(
            body,
            grid=(n // SC_WINDOW,),
            in_specs=[pl.BlockSpec((SC_WINDOW, y.shape[1]), lambda i: (i, 0)),
                      pl.BlockSpec((1, SC_WINDOW), lambda i: (0, i))],
            out_specs=[],
            core_axis_name=("core", "subcore"),
            dimension_semantics=(pltpu.PARALLEL,),
        )(y_hbm, i_hbm)

    return scatter_kernel(y, idx)


def _expert_kernel(first_block, has_alias, be_ref, nu_ref, x_ref, wgu_ref, bgu_ref, wd_ref, bd_ref, *rest):
    y_ref, wgu_bf, wd_bf = rest[1:] if has_alias else rest
    step = pl.program_id(0)
    blk = first_block + step

    @pl.when(blk < nu_ref[0])
    def _():
        @pl.when((step == 0) | (be_ref[blk] != be_ref[jnp.maximum(blk - 1, 0)]))
        def _():
            wgu_bf[...] = wgu_ref[0].astype(BF16)
            wd_bf[...] = wd_ref[0].astype(BF16)

        xb = _unpack_rows([x_ref[j] for j in range(ROW_PARTS)]).astype(BF16)
        gu = jnp.dot(xb, wgu_bf[...], preferred_element_type=F32) + bgu_ref[0]
        glu = jnp.minimum(gu[:, :D_FF], SWIGLU_LIMIT)
        lin = jnp.clip(gu[:, D_FF:], -SWIGLU_LIMIT, SWIGLU_LIMIT)
        act = glu * jax.nn.sigmoid(SWIGLU_ALPHA * glu) * (lin + 1.0)
        y = jnp.dot(act.astype(BF16), wd_bf[...], preferred_element_type=F32) + bd_ref[0]
        for j, piece in enumerate(_pack_rows(y)):
            y_ref[j] = piece

    @pl.when(blk >= nu_ref[0])
    def _():
        y_ref[...] = jnp.zeros_like(y_ref)


def _experts(h2, block_expert, n_used, row_tok, row_dst, w_gu, b_gu, w_down, b_down):
    R = MOE_ROWS
    T = h2.shape[1]
    n_rows = row_tok.shape[0]
    g_rows = n_rows // MOE_GROUPS
    g_blocks = g_rows // R
    part = jnp.arange(ROW_PARTS, dtype=jnp.int32)[:, None]
    h2_flat = h2.reshape(ROW_PARTS * T, PART_W)
    xs = []
    for grp in range(MOE_GROUPS):
        tok = row_tok[grp * g_rows:(grp + 1) * g_rows]
        src_idx = (part * T + tok[None, :]).reshape(1, ROW_PARTS * g_rows)
        xs.append(_sc_gather_rows(h2_flat, src_idx).reshape(ROW_PARTS, g_rows, PART_W))

    def run_group(grp, ys):
        first = grp * g_blocks
        wspec = lambda shape: pl.BlockSpec(shape, lambda i, be, nu: (be[first + i], 0, 0))
        in_specs = [
            pl.BlockSpec((ROW_PARTS, R, PART_W), lambda i, be, nu: (0, i, 0)),
            wspec((1, D_MODEL, 2 * D_FF)), wspec((1, 1, 2 * D_FF)), wspec((1, D_FF, D_MODEL)), wspec((1, 1, D_MODEL)),
        ]
        operands = [block_expert, n_used, xs[grp], w_gu, b_gu, w_down, b_down]
        aliases = {}
        if ys is not None:
            in_specs.append(pl.BlockSpec(memory_space=pl.ANY))
            aliases = {len(operands): 0}
            operands.append(ys)
        return pl.pallas_call(
            functools.partial(_expert_kernel, first, ys is not None),
            grid_spec=pltpu.PrefetchScalarGridSpec(
                num_scalar_prefetch=2,
                grid=(g_blocks,),
                in_specs=in_specs,
                out_specs=pl.BlockSpec((ROW_PARTS, R, PART_W), lambda i, be, nu: (0, first + i, 0)),
                scratch_shapes=[pltpu.VMEM((D_MODEL, 2 * D_FF), BF16), pltpu.VMEM((D_FF, D_MODEL), BF16)],
            ),
            out_shape=jax.ShapeDtypeStruct((ROW_PARTS, n_rows, PART_W), jnp.uint32),
            input_output_aliases=aliases,
            compiler_params=pltpu.CompilerParams(dimension_semantics=("arbitrary",), vmem_limit_bytes=VMEM_LIMIT),
            name=f"experts_g{grp}",
        )(*operands)

    ys = None
    for grp in range(MOE_GROUPS):
        ys = run_group(grp, ys)
    dst_idx = (part * n_rows + row_dst[None, :]).reshape(1, ROW_PARTS * n_rows)
    out = _sc_scatter_rows(ys.reshape(ROW_PARTS * n_rows, PART_W), dst_idx)
    return out.reshape(ROW_PARTS, n_rows, PART_W)


def _combine_kernel(x1_ref, gate_ref, g_ref, *refs):
    o_refs, out_ref = refs[:TOP_K], refs[TOP_K]
    y = x1_ref[...]
    gt = gate_ref[...]
    for kk in range(TOP_K):
        y = y + gt[:, kk:kk + 1] * _unpack_rows([o_refs[kk][j] for j in range(ROW_PARTS)])
    out_ref[...] = _rms(y, g_ref[...])


def _combine(x1, o4, gates, g):
    T = x1.shape[0]
    tm = TM_COMBINE
    nt = T // tm
    slot_spec = lambda kk: pl.BlockSpec((ROW_PARTS, tm, PART_W), lambda i: (0, kk * nt + i, 0))
    return pl.pallas_call(
        _combine_kernel,
        grid=(nt,),
        in_specs=[
            pl.BlockSpec((tm, D_MODEL), lambda i: (i, 0)),
            pl.BlockSpec((tm, LANES), lambda i: (i, 0)),
            pl.BlockSpec((1, D_MODEL), lambda i: (0, 0)),
        ] + [slot_spec(kk) for kk in range(TOP_K)],
        out_specs=pl.BlockSpec((tm, D_MODEL), lambda i: (i, 0)),
        out_shape=jax.ShapeDtypeStruct((T, D_MODEL), F32),
        compiler_params=pltpu.CompilerParams(dimension_semantics=("parallel",), vmem_limit_bytes=VMEM_LIMIT),
        name="combine_norm",
    )(x1, gates, g, *([o4] * TOP_K))


def _dispatch_plan(top_idx, T):
    R = MOE_ROWS
    P = T * TOP_K
    e_flat = top_idx.reshape(P)
    order = jnp.argsort(e_flat).astype(jnp.int32)
    counts = jnp.sum(e_flat[:, None] == jnp.arange(N_EXPERTS, dtype=jnp.int32)[None, :], axis=0, dtype=jnp.int32)
    padded = (counts + R - 1) // R * R
    start = jnp.cumsum(counts) - counts
    pend = jnp.cumsum(padded)
    pstart = pend - padded
    n_blocks = P // R + N_EXPERTS
    block_row0 = jnp.arange(n_blocks, dtype=jnp.int32) * R
    block_expert = jnp.minimum(jnp.sum(pend[None, :] <= block_row0[:, None], axis=1, dtype=jnp.int32), N_EXPERTS - 1)
    within = block_row0 - pstart[block_expert]
    n_valid = jnp.where(block_row0 < pend[-1], jnp.clip(counts[block_expert] - within, 0, R), 0).astype(jnp.int32)
    r_in = jnp.arange(R, dtype=jnp.int32)[None, :]
    src = jnp.clip((start[block_expert] + within)[:, None] + r_in, 0, P - 1)
    is_valid = (r_in < n_valid[:, None]).reshape(-1)
    pair = jnp.where(is_valid, order[src].reshape(-1), 0)
    row_tok = pair // TOP_K
    pad_rank = jnp.cumsum(1 - is_valid.astype(jnp.int32)) - 1
    row_dst = jnp.where(is_valid, (pair % TOP_K) * T + row_tok, P + pad_rank)
    n_used = (pend[-1] // R).reshape(1).astype(jnp.int32)
    return block_expert, n_used, row_tok, row_dst


def _layer(x2, B, S, layer, norm_attn_g, w_in, rel_bias_table, lam_vecs, diff_norm_g, conv_w, conv_b, gate_b,
           o_b, ml_norm_g, w_out, norm_ffn_g, router_w, router_b, w_gu, b_gu, w_down, b_down):
    T = B * S
    H = ML_HEADS
    w_main = w_in[:, :N_MAIN].astype(BF16)
    wg = w_in[:, N_MAIN:]
    w_gate = jnp.zeros((D_MODEL, 2 * LANES), F32).at[:, :H].set(wg[:, :H]).at[:, LANES:LANES + H].set(wg[:, H:])
    proj, gates = _inproj(x2, norm_attn_g.reshape(1, D_MODEL), w_main, w_gate.astype(BF16))

    lam_init = 0.8 - 0.6 * math.exp(-0.3 * layer)
    bias = _relbias(rel_bias_table.astype(F32))
    a_out = _attention(proj, bias, lam_vecs, diff_norm_g.reshape(DA_HEADS, 1, DA_V_DIM), B, S, lam_init)

    gates_t = jnp.concatenate([gates[:, :H], gates[:, LANES:LANES + H]], axis=1).T
    gate_b_pad = jnp.zeros((1, 2 * LANES), F32).at[0, :H].set(gate_b[:H]).at[0, LANES:LANES + H].set(gate_b[H:])
    m_out = _mlstm(proj, gates, gates_t, conv_w, conv_b.reshape(1, -1), gate_b_pad, gate_b.reshape(2 * H, 1),
                   o_b.reshape(1, -1), ml_norm_g.reshape(1, -1), B, S)

    rw = jnp.zeros((D_MODEL, LANES), F32).at[:, :N_EXPERTS].set(router_w).astype(BF16)
    rb = jnp.full((1, LANES), MASK_NEG, F32).at[0, :N_EXPERTS].set(router_b)
    x1, h2, top_idx, top_gate = _outproj(a_out, m_out, x2, w_out.astype(BF16), norm_ffn_g.reshape(1, D_MODEL), rw, rb)

    block_expert, n_used, row_tok, row_dst = _dispatch_plan(top_idx[:, :TOP_K], T)
    o4 = _experts(h2, block_expert, n_used, row_tok, row_dst, w_gu, b_gu.reshape(N_EXPERTS, 1, 2 * D_FF),
                  w_down, b_down.reshape(N_EXPERTS, 1, D_MODEL))
    return x1, o4, top_gate


def kernel(x, norm_attn_g, w_in, rel_bias_table, lambda_q1, lambda_k1, lambda_q2, lambda_k2, diff_norm_g,
           mlstm_conv_w, mlstm_conv_b, mlstm_gate_b, mlstm_o_b, mlstm_norm_g, w_out, norm_ffn_g, router_w,
           router_b, expert_w_gu, expert_b_gu, expert_w_down, expert_b_down, final_norm_g):
    B, S, _ = x.shape
    depth = w_in.shape[0]
    assert depth == 1, "the combine kernel fuses the last layer's MoE residual with the final norm"
    assert S % TQ == 0 and S % ML_CHUNK == 0 and (B * S) % TM_PROJ == 0 and (B * S * TOP_K) % MOE_ROWS == 0
    x2 = x.reshape(B * S, D_MODEL)
    l = 0
    lam_vecs = jnp.stack([lambda_q1[l], lambda_k1[l], lambda_q2[l], lambda_k2[l]])
    x1, o4, top_gate = _layer(
        x2, B, S, l, norm_attn_g[l], w_in[l], rel_bias_table, lam_vecs, diff_norm_g[l], mlstm_conv_w[l],
        mlstm_conv_b[l], mlstm_gate_b[l], mlstm_o_b[l], mlstm_norm_g[l], w_out[l], norm_ffn_g[l], router_w[l],
        router_b[l], expert_w_gu[l], expert_b_gu[l], expert_w_down[l], expert_b_down[l])
    out = _combine(x1, o4, top_gate, final_norm_g.reshape(1, D_MODEL))
    return out.reshape(B, S, D_MODEL)
```

```python
import functools
import math

import numpy as np
import jax
import jax.numpy as jnp
from jax import lax
from jax.experimental import pallas as pl
from jax.experimental.pallas import tpu as pltpu
from jax.experimental.pallas import tpu_sc as plsc

F32 = jnp.float32
BF16 = jnp.bfloat16

D_MODEL = 1024
NORM_EPS = 1e-5
DA_HEADS = 4
DA_QK_DIM = 64
DA_V_DIM = 128
DA_WIDTH = DA_HEADS * DA_V_DIM
REL_BUCKETS = 32
REL_MAX_DIST = 128
ML_HEADS = 4
ML_QK_DIM = 64
ML_V_DIM = 128
ML_WIDTH = ML_HEADS * ML_V_DIM
ML_QK_COLS = 2 * ML_HEADS * ML_QK_DIM
ML_CONV = 4
N_EXPERTS = 32
TOP_K = 4
D_FF = D_MODEL
SWIGLU_LIMIT = 7.0
SWIGLU_ALPHA = 1.702

N_MAIN = 3 * DA_WIDTH + ML_QK_COLS + 2 * ML_WIDTH
LANES = 128
SUBLANES = 8
MASK_NEG = -1e30

TM_PROJ = 512
TQ = 256
ML_CHUNK = 256
MOE_ROWS = 512
TM_COMBINE = 256
ROW_PARTS = 2
PART_W = D_MODEL // (2 * ROW_PARTS)
SC_WINDOW = 128
MOE_GROUPS = 8
VMEM_LIMIT = 56 * 1024 * 1024


def _bucket_lower_bounds():
    n = np.arange(0, 4 * REL_MAX_DIST)
    max_exact = REL_BUCKETS // 2
    nf = np.maximum(n, 1).astype(np.float32)
    large = max_exact + (np.log(nf / np.float32(max_exact)) / np.float32(math.log(REL_MAX_DIST / max_exact))
                         * np.float32(REL_BUCKETS - max_exact)).astype(np.int32)
    large = np.minimum(large, REL_BUCKETS - 1)
    bucket = np.where(n < max_exact, n, large)
    return [int(np.argmax(bucket >= b)) for b in range(REL_BUCKETS)]


BUCKET_LO = _bucket_lower_bounds()
assert BUCKET_LO[-1] <= TQ, "keys two blocks away must all sit in the last bucket"


def _rms(x, g):
    return x * lax.rsqrt(jnp.mean(x * x, axis=-1, keepdims=True) + NORM_EPS) * g


def _pack_rows(v):
    bits = lax.bitcast_convert_type(v.astype(BF16).astype(F32), jnp.uint32)
    return [bits[:, (2 * j + 1) * PART_W:(2 * j + 2) * PART_W] | (bits[:, 2 * j * PART_W:(2 * j + 1) * PART_W] >> 16)
            for j in range(ROW_PARTS)]


def _unpack_rows(pieces):
    cols = []
    for w in pieces:
        cols.append(lax.bitcast_convert_type(w << 16, F32))
        cols.append(lax.bitcast_convert_type(w & jnp.uint32(0xFFFF0000), F32))
    return jnp.concatenate(cols, axis=1)


def _log_sigmoid(x):
    return jnp.minimum(x, 0.0) - jnp.log1p(jnp.exp(-jnp.abs(x)))


def _inproj_kernel(x_ref, g_ref, w_ref, wg_ref, o_ref, og_ref):
    hb = _rms(x_ref[...], g_ref[...]).astype(BF16)
    for n in range(N_MAIN // 512):
        cols = slice(n * 512, (n + 1) * 512)
        o_ref[:, cols] = jnp.dot(hb, w_ref[:, cols], preferred_element_type=F32).astype(BF16)
    og_ref[...] = jnp.dot(hb, wg_ref[...], preferred_element_type=F32)


def _inproj(x2, g, w_main, w_gate):
    T = x2.shape[0]
    return pl.pallas_call(
        _inproj_kernel,
        grid=(T // TM_PROJ,),
        in_specs=[
            pl.BlockSpec((TM_PROJ, D_MODEL), lambda i: (i, 0)),
            pl.BlockSpec((1, D_MODEL), lambda i: (0, 0)),
            pl.BlockSpec((D_MODEL, N_MAIN), lambda i: (0, 0)),
            pl.BlockSpec((D_MODEL, 2 * LANES), lambda i: (0, 0)),
        ],
        out_specs=[
            pl.BlockSpec((TM_PROJ, N_MAIN), lambda i: (i, 0)),
            pl.BlockSpec((TM_PROJ, 2 * LANES), lambda i: (i, 0)),
        ],
        out_shape=[jax.ShapeDtypeStruct((T, N_MAIN), BF16), jax.ShapeDtypeStruct((T, 2 * LANES), F32)],
        compiler_params=pltpu.CompilerParams(dimension_semantics=("parallel",), vmem_limit_bytes=VMEM_LIMIT),
        name="inproj",
    )(x2, g, w_main, w_gate)


def _relbias_kernel(tbl_ref, o_ref):
    h = pl.program_id(0)
    r = lax.broadcasted_iota(jnp.int32, (2 * TQ, 2 * TQ), 0) & (TQ - 1)
    c = lax.broadcasted_iota(jnp.int32, (2 * TQ, 2 * TQ), 1)
    rel = r + TQ - c
    val = jnp.full((2 * TQ, 2 * TQ), tbl_ref[0, h], F32)
    for b in range(1, REL_BUCKETS):
        val = jnp.where(rel >= BUCKET_LO[b], tbl_ref[b, h], val)
    o_ref[0] = jnp.where(rel >= 0, val - tbl_ref[REL_BUCKETS - 1, h], MASK_NEG)


def _relbias(table):
    return pl.pallas_call(
        _relbias_kernel,
        grid=(DA_HEADS,),
        in_specs=[pl.BlockSpec(memory_space=pltpu.SMEM)],
        out_specs=pl.BlockSpec((1, 2 * TQ, 2 * TQ), lambda h: (h, 0, 0)),
        out_shape=jax.ShapeDtypeStruct((DA_HEADS, 2 * TQ, 2 * TQ), F32),
        compiler_params=pltpu.CompilerParams(dimension_semantics=("parallel",)),
        name="relbias",
    )(table)


def _attn_kernel(lam_init, lam_ref, q_ref, k_ref, v_ref, bias_ref, g_ref, o_ref, acc_ref, m_ref, l_ref):
    i = pl.program_id(2)
    q = q_ref[...] * jnp.asarray(DA_QK_DIM ** -0.5, BF16)
    lane = lax.broadcasted_iota(jnp.int32, q.shape, 1)
    zero = jnp.zeros_like(q)
    qq = jnp.concatenate([jnp.where(lane < DA_QK_DIM, q, zero), jnp.where(lane >= DA_QK_DIM, q, zero)], axis=0)

    def rows(j, nblk):
        start = j * TQ if isinstance(j, int) else pl.multiple_of(j * TQ, TQ)
        return pl.ds(start, nblk * TQ)

    def update(j, nblk, bias):
        s = lax.dot_general(qq, k_ref[rows(j, nblk), :], (((1,), (1,)), ((), ())), preferred_element_type=F32)
        if bias is not None:
            s = s + bias
        tiles = [s[:, t * LANES:(t + 1) * LANES] for t in range(nblk * TQ // LANES)]
        m_old = m_ref[...]
        m_new = jnp.maximum(m_old, jnp.max(functools.reduce(jnp.maximum, tiles), axis=-1, keepdims=True))
        alpha = jnp.exp(m_old - m_new)
        ps = [jnp.exp(t - m_new) for t in tiles]
        p = jnp.concatenate(ps, axis=1).astype(BF16)
        l_ref[...] = alpha * l_ref[...] + jnp.sum(functools.reduce(jnp.add, ps), axis=-1, keepdims=True)
        acc_ref[...] = alpha * acc_ref[...] + jnp.dot(p, v_ref[rows(j, nblk), :], preferred_element_type=F32)
        m_ref[...] = m_new

    m_ref[...] = jnp.full_like(m_ref, MASK_NEG)
    l_ref[...] = jnp.zeros_like(l_ref)
    acc_ref[...] = jnp.zeros_like(acc_ref)

    @pl.when(i == 0)
    def _():
        update(0, 1, bias_ref[0, :, TQ:])

    @pl.when(i >= 1)
    def _():
        update(i - 1, 2, bias_ref[0])

    n_far = jnp.maximum(i - 1, 0)
    odd = n_far % 2

    @pl.when(odd == 1)
    def _():
        update(0, 1, None)

    def far_pair(jj, carry):
        update(odd + 2 * jj, 2, None)
        return carry

    lax.fori_loop(0, n_far // 2, far_pair, 0)

    lam_v = lam_ref[...]
    lam = (jnp.exp(jnp.sum(lam_v[0:1] * lam_v[1:2], axis=-1, keepdims=True))
           - jnp.exp(jnp.sum(lam_v[2:3] * lam_v[3:4], axis=-1, keepdims=True)) + lam_init)
    o = acc_ref[...] / l_ref[...]
    out = o[:TQ] - lam * o[TQ:]
    o_ref[...] = (_rms(out, g_ref[0]) * (1.0 - lam_init)).astype(BF16)


def _attention(proj, bias, lam_vecs, norm_g, B, S, lam_init):
    nq = S // TQ
    q_off, k_off, v_off = 0, DA_HEADS, 2 * DA_HEADS
    return pl.pallas_call(
        functools.partial(_attn_kernel, lam_init),
        grid=(B, DA_HEADS, nq),
        in_specs=[
            pl.BlockSpec((4, DA_QK_DIM), lambda b, h, i: (0, 0)),
            pl.BlockSpec((TQ, LANES), lambda b, h, i: (b * nq + i, q_off + h)),
            pl.BlockSpec((S, LANES), lambda b, h, i: (b, k_off + h)),
            pl.BlockSpec((S, LANES), lambda b, h, i: (b, v_off + h)),
            pl.BlockSpec((1, 2 * TQ, 2 * TQ), lambda b, h, i: (h, 0, 0)),
            pl.BlockSpec((1, 1, LANES), lambda b, h, i: (h, 0, 0)),
        ],
        out_specs=pl.BlockSpec((TQ, LANES), lambda b, h, i: (b * nq + i, h)),
        out_shape=jax.ShapeDtypeStruct((B * S, DA_WIDTH), BF16),
        scratch_shapes=[
            pltpu.VMEM((2 * TQ, DA_V_DIM), F32),
            pltpu.VMEM((2 * TQ, LANES), F32),
            pltpu.VMEM((2 * TQ, LANES), F32),
        ],
        compiler_params=pltpu.CompilerParams(
            dimension_semantics=("parallel", "parallel", "arbitrary"), vmem_limit_bytes=VMEM_LIMIT),
        name="diff_attention",
    )(lam_vecs, proj, proj, proj, bias, norm_g)


def _prefix_sum(x, axis):
    n = x.shape[axis]
    idx = lax.broadcasted_iota(jnp.int32, x.shape, axis)
    d = 1
    while d < n:
        x = x + jnp.where(idx >= d, pltpu.roll(x, d, axis), 0.0)
        d *= 2
    return x


def _mlstm_kernel(qk_ref, v_ref, og_ref, g_ref, gt_ref, cw_ref, cb_ref, gb_ref, gbt_ref, ob_ref, ng_ref,
                  out_ref, prev_ref, cn_ref, m_ref):
    L = ML_CHUNK
    H = ML_HEADS
    QW = H * ML_QK_DIM

    @pl.when(pl.program_id(1) == 0)
    def _():
        prev_ref[...] = jnp.zeros_like(prev_ref)
        cn_ref[...] = jnp.zeros_like(cn_ref)
        m_ref[...] = jnp.zeros_like(m_ref)

    x = qk_ref[...].astype(F32)
    prev = prev_ref[...]
    row = lax.broadcasted_iota(jnp.int32, x.shape, 0)
    y = cb_ref[...] + cw_ref[ML_CONV - 1:ML_CONV, :] * x
    for d in range(1, ML_CONV):
        shifted = jnp.where(row < d, pltpu.roll(prev, d, 0), pltpu.roll(x, d, 0))
        y = y + cw_ref[ML_CONV - 1 - d:ML_CONV - d, :] * shifted
    prev_ref[...] = x
    qkc = y * jax.nn.sigmoid(y)
    q = qkc[:, :QW]
    k = qkc[:, QW:] * (ML_QK_DIM ** -0.5)
    kb = k.astype(BF16)

    G = g_ref[...]
    li_all = G[:, :LANES] + gb_ref[:, :LANES]
    b_all = _prefix_sum(_log_sigmoid(G[:, LANES:] + gb_ref[:, LANES:]), 0)
    GT = gt_ref[...] + gbt_ref[...]
    bT = _prefix_sum(_log_sigmoid(GT), 1)

    qlane = lax.broadcasted_iota(jnp.int32, (L, QW), 1) // ML_QK_DIM
    qz = jnp.concatenate([jnp.where(qlane == h, q, 0.0) for h in range(H)], axis=0).astype(BF16)
    s_all = lax.dot_general(qz, kb, (((1,), (1,)), ((), ())), preferred_element_type=F32)
    cn_old = cn_ref[...]
    qcn = jnp.dot(qz, cn_old.astype(BF16), preferred_element_type=F32)

    t_idx = lax.broadcasted_iota(jnp.int32, (L, L), 0)
    j_idx = lax.broadcasted_iota(jnp.int32, (L, L), 1)
    causal = j_idx <= t_idx
    m_old_row = m_ref[...]
    mlane = lax.broadcasted_iota(jnp.int32, (1, LANES), 1)
    crow = lax.broadcasted_iota(jnp.int32, (QW, 1), 0) // ML_QK_DIM
    m_new_row = m_old_row
    wk_exp = jnp.zeros((L, QW), F32)
    decay_col = jnp.zeros((QW, 1), F32)

    for h in range(H):
        li_c = li_all[:, h:h + 1]
        b_c = b_all[:, h:h + 1]
        li_r = GT[h:h + 1, :]
        b_r = bT[H + h:H + h + 1, :]
        m_prev = m_old_row[:, h:h + 1]

        dlog = jnp.where(causal, b_c - b_r + li_r, MASK_NEG)
        inter = b_c + m_prev
        m_t = jnp.maximum(jnp.max(dlog, axis=-1, keepdims=True), inter)
        w_intra = jnp.exp(dlog - m_t)
        w_inter = jnp.exp(inter - m_t)
        s = s_all[h * L:(h + 1) * L, :] * w_intra
        vh = v_ref[:, h * ML_V_DIM:(h + 1) * ML_V_DIM]
        num = (jnp.dot(s.astype(BF16), vh, preferred_element_type=F32)
               + w_inter * qcn[h * L:(h + 1) * L, :ML_V_DIM])
        den = jnp.sum(s, axis=-1, keepdims=True) + w_inter * qcn[h * L:(h + 1) * L, ML_V_DIM:]
        hval = num / jnp.maximum(jnp.abs(den), jnp.exp(-m_t))

        cols = slice(h * ML_V_DIM, (h + 1) * ML_V_DIM)
        gate = jax.nn.sigmoid(og_ref[:, cols].astype(F32) + ob_ref[:, cols])
        out_ref[:, cols] = (_rms(hval, ng_ref[:, cols]) * gate).astype(BF16)

        g_tot = b_c[L - 1:L, :]
        wlog = g_tot - b_c + li_c
        m_new = jnp.maximum(g_tot + m_prev, jnp.max(wlog, axis=0, keepdims=True))
        decay = jnp.exp(g_tot + m_prev - m_new)
        wk = jnp.exp(wlog - m_new)
        m_new_row = jnp.where(mlane == h, m_new, m_new_row)
        wk_exp = jnp.where(qlane == h, wk, wk_exp)
        decay_col = jnp.where(crow == h, decay, decay_col)

    kw = (k * wk_exp).astype(BF16)
    vext = jnp.concatenate([v_ref[...], jnp.ones((L, LANES), BF16)], axis=1)
    upd = lax.dot_general(kw, vext, (((0,), (0,)), ((), ())), preferred_element_type=F32)
    for h in range(H):
        rows = slice(h * ML_QK_DIM, (h + 1) * ML_QK_DIM)
        dh = decay_col[rows, :]
        cn_ref[rows, :ML_V_DIM] = dh * cn_old[rows, :ML_V_DIM] + upd[rows, h * ML_V_DIM:(h + 1) * ML_V_DIM]
        cn_ref[rows, ML_V_DIM:] = dh * cn_old[rows, ML_V_DIM:] + upd[rows, H * ML_V_DIM:]
    m_ref[...] = m_new_row


def _mlstm(proj, gates, gates_t, conv_w, conv_b, gate_b, gate_bt, o_b, norm_g, B, S):
    L = ML_CHUNK
    nc = S // L
    qk_blk = 3 * DA_WIDTH // ML_QK_COLS
    v_blk, o_blk = qk_blk + 1, qk_blk + 2
    full = lambda shape: pl.BlockSpec(shape, lambda b, c: (0,) * len(shape))
    return pl.pallas_call(
        _mlstm_kernel,
        grid=(B, nc),
        in_specs=[
            pl.BlockSpec((L, ML_QK_COLS), lambda b, c: (b * nc + c, qk_blk)),
            pl.BlockSpec((L, ML_WIDTH), lambda b, c: (b * nc + c, v_blk)),
            pl.BlockSpec((L, ML_WIDTH), lambda b, c: (b * nc + c, o_blk)),
            pl.BlockSpec((L, 2 * LANES), lambda b, c: (b * nc + c, 0)),
            pl.BlockSpec((2 * ML_HEADS, L), lambda b, c: (0, b * nc + c)),
            full((ML_CONV, ML_QK_COLS)),
            full((1, ML_QK_COLS)),
            full((1, 2 * LANES)),
            full((2 * ML_HEADS, 1)),
            full((1, ML_WIDTH)),
            full((1, ML_WIDTH)),
        ],
        out_specs=pl.BlockSpec((L, ML_WIDTH), lambda b, c: (b * nc + c, 0)),
        out_shape=jax.ShapeDtypeStruct((B * S, ML_WIDTH), BF16),
        scratch_shapes=[
            pltpu.VMEM((L, ML_QK_COLS), F32),
            pltpu.VMEM((ML_HEADS * ML_QK_DIM, 2 * ML_V_DIM), F32),
            pltpu.VMEM((1, LANES), F32),
        ],
        compiler_params=pltpu.CompilerParams(
            dimension_semantics=("parallel", "arbitrary"), vmem_limit_bytes=VMEM_LIMIT),
        name="mlstm",
    )(proj, proj, proj, gates, gates_t, conv_w, conv_b, gate_b, gate_bt, o_b, norm_g)


def _outproj_kernel(a_ref, m_ref, x_ref, wo_ref, g_ref, rw_ref, rb_ref, x1_ref, h2_ref, idx_ref, gate_ref):
    x1 = (x_ref[...]
          + jnp.dot(a_ref[...], wo_ref[:DA_WIDTH, :], preferred_element_type=F32)
          + jnp.dot(m_ref[...], wo_ref[DA_WIDTH:, :], preferred_element_type=F32))
    x1_ref[...] = x1
    h2 = _rms(x1, g_ref[...])
    for j, piece in enumerate(_pack_rows(h2)):
        h2_ref[j] = piece
    work = jnp.dot(h2.astype(BF16), rw_ref[...], preferred_element_type=F32) + rb_ref[...]
    lane = lax.broadcasted_iota(jnp.int32, work.shape, 1).astype(F32)
    idxs = jnp.zeros_like(work)
    vals = jnp.zeros_like(work)
    top = None
    for kk in range(TOP_K):
        mx = jnp.max(work, axis=-1, keepdims=True)
        am = jnp.min(jnp.where(work == mx, lane, float(LANES)), axis=-1, keepdims=True)
        idxs = jnp.where(lane == kk, am, idxs)
        vals = jnp.where(lane == kk, mx, vals)
        work = jnp.where(lane == am, -jnp.inf, work)
        if kk == 0:
            top = mx
    e = jnp.where(lane < TOP_K, jnp.exp(vals - top), 0.0)
    gate_ref[...] = e / jnp.sum(e, axis=-1, keepdims=True)
    idx_ref[...] = idxs.astype(jnp.int32)


def _outproj(a_out, m_out, x2, w_out, g, router_w, router_b):
    T = x2.shape[0]
    row = lambda w: pl.BlockSpec((TM_PROJ, w), lambda i: (i, 0))
    full = lambda r, c: pl.BlockSpec((r, c), lambda i: (0, 0))
    return pl.pallas_call(
        _outproj_kernel,
        grid=(T // TM_PROJ,),
        in_specs=[row(DA_WIDTH), row(ML_WIDTH), row(D_MODEL), full(D_MODEL, D_MODEL), full(1, D_MODEL),
                  full(D_MODEL, LANES), full(1, LANES)],
        out_specs=[row(D_MODEL), pl.BlockSpec((ROW_PARTS, TM_PROJ, PART_W), lambda i: (0, i, 0)),
                   row(LANES), row(LANES)],
        out_shape=[jax.ShapeDtypeStruct((T, D_MODEL), F32), jax.ShapeDtypeStruct((ROW_PARTS, T, PART_W), jnp.uint32),
                   jax.ShapeDtypeStruct((T, LANES), jnp.int32), jax.ShapeDtypeStruct((T, LANES), F32)],
        compiler_params=pltpu.CompilerParams(dimension_semantics=("parallel",), vmem_limit_bytes=VMEM_LIMIT),
        name="outproj_router",
    )(a_out, m_out, x2, w_out, g, router_w, router_b)


def _sc_mesh():
    return plsc.VectorSubcoreMesh(core_axis_name="core", subcore_axis_name="subcore")


def _sc_gather_rows(x, idx):
    n = idx.shape[1]

    @functools.partial(pl.kernel, out_type=jax.ShapeDtypeStruct((n, x.shape[1]), x.dtype), mesh=_sc_mesh())
    def gather_kernel(x_hbm, i_hbm, o_hbm):
        def body(i_vmem, o_vmem):
            pltpu.sync_copy(x_hbm.at[i_vmem.at[0]], o_vmem)

        pltpu.emit_pipeline(
            body,
            grid=(n // SC_WINDOW,),
            in_specs=[pl.BlockSpec((1, SC_WINDOW), lambda i: (0, i))],
            out_specs=[pl.BlockSpec((SC_WINDOW, x.shape[1]), lambda i: (i, 0))],
            core_axis_name=("core", "subcore"),
            dimension_semantics=(pltpu.PARALLEL,),
        )(i_hbm, o_hbm)

    return gather_kernel(x, idx)


def _sc_scatter_rows(y, idx):
    n = idx.shape[1]

    @functools.partial(pl.kernel, out_type=jax.ShapeDtypeStruct(y.shape, y.dtype), mesh=_sc_mesh())
    def scatter_kernel(y_hbm, i_hbm, o_hbm):
        def body(y_vmem, i_vmem):
            pltpu.sync_copy(y_vmem, o_hbm.at[i_vmem.at[0]])

        pltpu.emit_pipeline(
            body,
            grid=(n // SC_WINDOW,),
            in_specs=[pl.BlockSpec((SC_WINDOW, y.shape[1]), lambda i: (i, 0)),
                      pl.BlockSpec((1, SC_WINDOW), lambda i: (0, i))],
            out_specs=[],
            core_axis_name=("core", "subcore"),
            dimension_semantics=(pltpu.PARALLEL,),
        )(y_hbm, i_hbm)

    return scatter_kernel(y, idx)


def _expert_kernel(first_block, has_alias, be_ref, nu_ref, x_ref, wgu_ref, bgu_ref, wd_ref, bd_ref, *rest):
    y_ref, wgu_bf, wd_bf = rest[1:] if has_alias else rest
    step = pl.program_id(0)
    blk = first_block + step

    @pl.when(blk < nu_ref[0])
    def _():
        @pl.when((step == 0) | (be_ref[blk] != be_ref[jnp.maximum(blk - 1, 0)]))
        def _():
            wgu_bf[...] = wgu_ref[0].astype(BF16)
            wd_bf[...] = wd_ref[0].astype(BF16)

        xb = _unpack_rows([x_ref[j] for j in range(ROW_PARTS)]).astype(BF16)
        gu = jnp.dot(xb, wgu_bf[...], preferred_element_type=F32) + bgu_ref[0]
        glu = jnp.minimum(gu[:, :D_FF], SWIGLU_LIMIT)
        lin = jnp.clip(gu[:, D_FF:], -SWIGLU_LIMIT, SWIGLU_LIMIT)
        act = glu * jax.nn.sigmoid(SWIGLU_ALPHA * glu) * (lin + 1.0)
        y = jnp.dot(act.astype(BF16), wd_bf[...], preferred_element_type=F32) + bd_ref[0]
        for j, piece in enumerate(_pack_rows(y)):
            y_ref[j] = piece

    @pl.when(blk >= nu_ref[0])
    def _():
        y_ref[...] = jnp.zeros_like(y_ref)


def _experts(h2, block_expert, n_used, row_tok, row_dst, w_gu, b_gu, w_down, b_down):
    R = MOE_ROWS
    T = h2.shape[1]
    n_rows = row_tok.shape[0]
    n_blocks = n_rows // R
    bounds = sorted({0, n_blocks // (4 * MOE_GROUPS), n_blocks // MOE_GROUPS}
                    | {n_blocks * g // MOE_GROUPS for g in range(2, MOE_GROUPS + 1)})
    groups = list(zip(bounds[:-1], bounds[1:]))
    part = jnp.arange(ROW_PARTS, dtype=jnp.int32)[:, None]
    h2_flat = h2.reshape(ROW_PARTS * T, PART_W)
    xs = []
    for b0, b1 in groups:
        tok = row_tok[b0 * R:b1 * R]
        src_idx = (part * T + tok[None, :]).reshape(1, ROW_PARTS * (b1 - b0) * R)
        xs.append(_sc_gather_rows(h2_flat, src_idx).reshape(ROW_PARTS, (b1 - b0) * R, PART_W))

    def run_group(grp, ys):
        first, g_blocks = groups[grp][0], groups[grp][1] - groups[grp][0]
        wspec = lambda shape: pl.BlockSpec(shape, lambda i, be, nu: (be[first + i], 0, 0))
        in_specs = [
            pl.BlockSpec((ROW_PARTS, R, PART_W), lambda i, be, nu: (0, i, 0)),
            wspec((1, D_MODEL, 2 * D_FF)), wspec((1, 1, 2 * D_FF)), wspec((1, D_FF, D_MODEL)), wspec((1, 1, D_MODEL)),
        ]
        operands = [block_expert, n_used, xs[grp], w_gu, b_gu, w_down, b_down]
        aliases = {}
        if ys is not None:
            in_specs.append(pl.BlockSpec(memory_space=pl.ANY))
            aliases = {len(operands): 0}
            operands.append(ys)
        return pl.pallas_call(
            functools.partial(_expert_kernel, first, ys is not None),
            grid_spec=pltpu.PrefetchScalarGridSpec(
                num_scalar_prefetch=2,
                grid=(g_blocks,),
                in_specs=in_specs,
                out_specs=pl.BlockSpec((ROW_PARTS, R, PART_W), lambda i, be, nu: (0, first + i, 0)),
                scratch_shapes=[pltpu.VMEM((D_MODEL, 2 * D_FF), BF16), pltpu.VMEM((D_FF, D_MODEL), BF16)],
            ),
            out_shape=jax.ShapeDtypeStruct((ROW_PARTS, n_rows, PART_W), jnp.uint32),
            input_output_aliases=aliases,
            compiler_params=pltpu.CompilerParams(dimension_semantics=("arbitrary",), vmem_limit_bytes=VMEM_LIMIT),
            name=f"experts_g{grp}",
        )(*operands)

    ys = None
    for grp in range(len(groups)):
        ys = run_group(grp, ys)
    dst_idx = (part * n_rows + row_dst[None, :]).reshape(1, ROW_PARTS * n_rows)
    out = _sc_scatter_rows(ys.reshape(ROW_PARTS * n_rows, PART_W), dst_idx)
    return out.reshape(ROW_PARTS, n_rows, PART_W)


def _combine_kernel(x1_ref, gate_ref, g_ref, *refs):
    o_refs, out_ref = refs[:TOP_K], refs[TOP_K]
    y = x1_ref[...]
    gt = gate_ref[...]
    for kk in range(TOP_K):
        y = y + gt[:, kk:kk + 1] * _unpack_rows([o_refs[kk][j] for j in range(ROW_PARTS)])
    out_ref[...] = _rms(y, g_ref[...])


def _combine(x1, o4, gates, g):
    T = x1.shape[0]
    tm = TM_COMBINE
    nt = T // tm
    slot_spec = lambda kk: pl.BlockSpec((ROW_PARTS, tm, PART_W), lambda i: (0, kk * nt + i, 0))
    return pl.pallas_call(
        _combine_kernel,
        grid=(nt,),
        in_specs=[
            pl.BlockSpec((tm, D_MODEL), lambda i: (i, 0)),
            pl.BlockSpec((tm, LANES), lambda i: (i, 0)),
            pl.BlockSpec((1, D_MODEL), lambda i: (0, 0)),
        ] + [slot_spec(kk) for kk in range(TOP_K)],
        out_specs=pl.BlockSpec((tm, D_MODEL), lambda i: (i, 0)),
        out_shape=jax.ShapeDtypeStruct((T, D_MODEL), F32),
        compiler_params=pltpu.CompilerParams(dimension_semantics=("parallel",), vmem_limit_bytes=VMEM_LIMIT),
        name="combine_norm",
    )(x1, gates, g, *([o4] * TOP_K))


def _dispatch_plan(top_idx, T):
    R = MOE_ROWS
    P = T * TOP_K
    e_flat = top_idx.reshape(P)
    assert N_EXPERTS * P < 2 ** 31
    order = jnp.sort(e_flat * P + jnp.arange(P, dtype=jnp.int32)) % P
    counts = jnp.sum(e_flat[:, None] == jnp.arange(N_EXPERTS, dtype=jnp.int32)[None, :], axis=0, dtype=jnp.int32)
    padded = (counts + R - 1) // R * R
    start = jnp.cumsum(counts) - counts
    pend = jnp.cumsum(padded)
    pstart = pend - padded
    n_blocks = P // R + N_EXPERTS
    block_row0 = jnp.arange(n_blocks, dtype=jnp.int32) * R
    block_expert = jnp.minimum(jnp.sum(pend[None, :] <= block_row0[:, None], axis=1, dtype=jnp.int32), N_EXPERTS - 1)
    within = block_row0 - pstart[block_expert]
    n_valid = jnp.where(block_row0 < pend[-1], jnp.clip(counts[block_expert] - within, 0, R), 0).astype(jnp.int32)
    r_in = jnp.arange(R, dtype=jnp.int32)[None, :]
    src = jnp.clip((start[block_expert] + within)[:, None] + r_in, 0, P - 1)
    is_valid = (r_in < n_valid[:, None]).reshape(-1)
    pair = jnp.where(is_valid, order[src].reshape(-1), 0)
    row_tok = jnp.where(is_valid, pair // TOP_K, jnp.arange(n_blocks * R, dtype=jnp.int32) % T)
    pad_rank = jnp.cumsum(1 - is_valid.astype(jnp.int32)) - 1
    row_dst = jnp.where(is_valid, (pair % TOP_K) * T + row_tok, P + pad_rank)
    n_used = (pend[-1] // R).reshape(1).astype(jnp.int32)
    return block_expert, n_used, row_tok, row_dst


def _layer(x2, B, S, layer, norm_attn_g, w_in, rel_bias_table, lam_vecs, diff_norm_g, conv_w, conv_b, gate_b,
           o_b, ml_norm_g, w_out, norm_ffn_g, router_w, router_b, w_gu, b_gu, w_down, b_down):
    T = B * S
    H = ML_HEADS
    w_main = w_in[:, :N_MAIN].astype(BF16)
    wg = w_in[:, N_MAIN:]
    w_gate = jnp.zeros((D_MODEL, 2 * LANES), F32).at[:, :H].set(wg[:, :H]).at[:, LANES:LANES + H].set(wg[:, H:])
    proj, gates = _inproj(x2, norm_attn_g.reshape(1, D_MODEL), w_main, w_gate.astype(BF16))

    lam_init = 0.8 - 0.6 * math.exp(-0.3 * layer)
    bias = _relbias(rel_bias_table.astype(F32))
    a_out = _attention(proj, bias, lam_vecs, diff_norm_g.reshape(DA_HEADS, 1, DA_V_DIM), B, S, lam_init)

    gates_t = jnp.concatenate([gates[:, :H], gates[:, LANES:LANES + H]], axis=1).T
    gate_b_pad = jnp.zeros((1, 2 * LANES), F32).at[0, :H].set(gate_b[:H]).at[0, LANES:LANES + H].set(gate_b[H:])
    m_out = _mlstm(proj, gates, gates_t, conv_w, conv_b.reshape(1, -1), gate_b_pad, gate_b.reshape(2 * H, 1),
                   o_b.reshape(1, -1), ml_norm_g.reshape(1, -1), B, S)

    rw = jnp.zeros((D_MODEL, LANES), F32).at[:, :N_EXPERTS].set(router_w).astype(BF16)
    rb = jnp.full((1, LANES), MASK_NEG, F32).at[0, :N_EXPERTS].set(router_b)
    x1, h2, top_idx, top_gate = _outproj(a_out, m_out, x2, w_out.astype(BF16), norm_ffn_g.reshape(1, D_MODEL), rw, rb)

    block_expert, n_used, row_tok, row_dst = _dispatch_plan(top_idx[:, :TOP_K], T)
    o4 = _experts(h2, block_expert, n_used, row_tok, row_dst, w_gu, b_gu.reshape(N_EXPERTS, 1, 2 * D_FF),
                  w_down, b_down.reshape(N_EXPERTS, 1, D_MODEL))
    return x1, o4, top_gate


def kernel(x, norm_attn_g, w_in, rel_bias_table, lambda_q1, lambda_k1, lambda_q2, lambda_k2, diff_norm_g,
           mlstm_conv_w, mlstm_conv_b, mlstm_gate_b, mlstm_o_b, mlstm_norm_g, w_out, norm_ffn_g, router_w,
           router_b, expert_w_gu, expert_b_gu, expert_w_down, expert_b_down, final_norm_g):
    B, S, _ = x.shape
    depth = w_in.shape[0]
    assert depth == 1, "the combine kernel fuses the last layer's MoE residual with the final norm"
    assert S % TQ == 0 and S % ML_CHUNK == 0 and (B * S) % TM_PROJ == 0 and (B * S * TOP_K) % MOE_ROWS == 0
    x2 = x.reshape(B * S, D_MODEL)
    l = 0
    lam_vecs = jnp.stack([lambda_q1[l], lambda_k1[l], lambda_q2[l], lambda_k2[l]])
    x1, o4, top_gate = _layer(
        x2, B, S, l, norm_attn_g[l], w_in[l], rel_bias_table, lam_vecs, diff_norm_g[l], mlstm_conv_w[l],
        mlstm_conv_b[l], mlstm_gate_b[l], mlstm_o_b[l], mlstm_norm_g[l], w_out[l], norm_ffn_g[l], router_w[l],
        router_b[l], expert_w_gu[l], expert_b_gu[l], expert_w_down[l], expert_b_down[l])
    out = _combine(x1, o4, top_gate, final_norm_g.reshape(1, D_MODEL))
    return out.reshape(B, S, D_MODEL)
```

```python
import functools
import math

import numpy as np
import jax
import jax.numpy as jnp
from jax import lax
from jax.experimental import pallas as pl
from jax.experimental.pallas import tpu as pltpu
from jax.experimental.pallas import tpu_sc as plsc

F32 = jnp.float32
BF16 = jnp.bfloat16

D_MODEL = 1024
NORM_EPS = 1e-5
DA_HEADS = 4
DA_QK_DIM = 64
DA_V_DIM = 128
DA_WIDTH = DA_HEADS * DA_V_DIM
REL_BUCKETS = 32
REL_MAX_DIST = 128
ML_HEADS = 4
ML_QK_DIM = 64
ML_V_DIM = 128
ML_WIDTH = ML_HEADS * ML_V_DIM
ML_QK_COLS = 2 * ML_HEADS * ML_QK_DIM
ML_CONV = 4
N_EXPERTS = 32
TOP_K = 4
D_FF = D_MODEL
SWIGLU_LIMIT = 7.0
SWIGLU_ALPHA = 1.702

N_MAIN = 3 * DA_WIDTH + ML_QK_COLS + 2 * ML_WIDTH
LANES = 128
SUBLANES = 8
MASK_NEG = -1e30

TM_PROJ = 512
TQ = 256
ATT_HEADS = 4
ML_CHUNK = 256
MOE_ROWS = 512
TM_COMBINE = 256
ROW_PARTS = 2
PART_W = D_MODEL // (2 * ROW_PARTS)
SC_WINDOW = 128
MOE_GROUPS = 8
VMEM_LIMIT = 56 * 1024 * 1024


def _bucket_lower_bounds():
    n = np.arange(0, 4 * REL_MAX_DIST)
    max_exact = REL_BUCKETS // 2
    nf = np.maximum(n, 1).astype(np.float32)
    large = max_exact + (np.log(nf / np.float32(max_exact)) / np.float32(math.log(REL_MAX_DIST / max_exact))
                         * np.float32(REL_BUCKETS - max_exact)).astype(np.int32)
    large = np.minimum(large, REL_BUCKETS - 1)
    bucket = np.where(n < max_exact, n, large)
    return [int(np.argmax(bucket >= b)) for b in range(REL_BUCKETS)]


BUCKET_LO = _bucket_lower_bounds()
assert BUCKET_LO[-1] <= TQ, "keys two blocks away must all sit in the last bucket"


def _rms(x, g):
    return x * lax.rsqrt(jnp.mean(x * x, axis=-1, keepdims=True) + NORM_EPS) * g


def _pack_rows(v):
    bits = lax.bitcast_convert_type(v.astype(BF16).astype(F32), jnp.uint32)
    return [bits[:, (2 * j + 1) * PART_W:(2 * j + 2) * PART_W] | (bits[:, 2 * j * PART_W:(2 * j + 1) * PART_W] >> 16)
            for j in range(ROW_PARTS)]


def _unpack_rows(pieces):
    cols = []
    for w in pieces:
        cols.append(lax.bitcast_convert_type(w << 16, F32))
        cols.append(lax.bitcast_convert_type(w & jnp.uint32(0xFFFF0000), F32))
    return jnp.concatenate(cols, axis=1)


def _log_sigmoid(x):
    return jnp.minimum(x, 0.0) - jnp.log1p(jnp.exp(-jnp.abs(x)))


def _inproj_kernel(x_ref, g_ref, w_ref, wg_ref, o_ref, og_ref):
    hb = _rms(x_ref[...], g_ref[...]).astype(BF16)
    for n in range(N_MAIN // 512):
        cols = slice(n * 512, (n + 1) * 512)
        o_ref[:, cols] = jnp.dot(hb, w_ref[:, cols], preferred_element_type=F32).astype(BF16)
    og_ref[...] = jnp.dot(hb, wg_ref[...], preferred_element_type=F32)


def _inproj(x2, g, w_main, w_gate):
    T = x2.shape[0]
    return pl.pallas_call(
        _inproj_kernel,
        grid=(T // TM_PROJ,),
        in_specs=[
            pl.BlockSpec((TM_PROJ, D_MODEL), lambda i: (i, 0)),
            pl.BlockSpec((1, D_MODEL), lambda i: (0, 0)),
            pl.BlockSpec((D_MODEL, N_MAIN), lambda i: (0, 0)),
            pl.BlockSpec((D_MODEL, 2 * LANES), lambda i: (0, 0)),
        ],
        out_specs=[
            pl.BlockSpec((TM_PROJ, N_MAIN), lambda i: (i, 0)),
            pl.BlockSpec((TM_PROJ, 2 * LANES), lambda i: (i, 0)),
        ],
        out_shape=[jax.ShapeDtypeStruct((T, N_MAIN), BF16), jax.ShapeDtypeStruct((T, 2 * LANES), F32)],
        compiler_params=pltpu.CompilerParams(dimension_semantics=("parallel",), vmem_limit_bytes=VMEM_LIMIT),
        name="inproj",
    )(x2, g, w_main, w_gate)


def _relbias_kernel(tbl_ref, o_ref):
    h = pl.program_id(0)
    r = lax.broadcasted_iota(jnp.int32, (2 * TQ, 2 * TQ), 0) & (TQ - 1)
    c = lax.broadcasted_iota(jnp.int32, (2 * TQ, 2 * TQ), 1)
    rel = r + TQ - c
    val = jnp.full((2 * TQ, 2 * TQ), tbl_ref[0, h], F32)
    for b in range(1, REL_BUCKETS):
        val = jnp.where(rel >= BUCKET_LO[b], tbl_ref[b, h], val)
    o_ref[0] = jnp.where(rel >= 0, val - tbl_ref[REL_BUCKETS - 1, h], MASK_NEG)


def _relbias(table):
    return pl.pallas_call(
        _relbias_kernel,
        grid=(DA_HEADS,),
        in_specs=[pl.BlockSpec(memory_space=pltpu.SMEM)],
        out_specs=pl.BlockSpec((1, 2 * TQ, 2 * TQ), lambda h: (h, 0, 0)),
        out_shape=jax.ShapeDtypeStruct((DA_HEADS, 2 * TQ, 2 * TQ), F32),
        compiler_params=pltpu.CompilerParams(dimension_semantics=("parallel",)),
        name="relbias",
    )(table)


def _attn_kernel(lam_init, lam_ref, q_ref, k_ref, v_ref, bias_ref, g_ref, o_ref, acc_ref, m_ref, l_ref):
    i = pl.program_id(2)
    lane = lax.broadcasted_iota(jnp.int32, (TQ, LANES), 1)
    qqs = []
    for hh in range(ATT_HEADS):
        q = q_ref[:, hh * LANES:(hh + 1) * LANES] * jnp.asarray(DA_QK_DIM ** -0.5, BF16)
        zero = jnp.zeros_like(q)
        qqs.append(jnp.concatenate([jnp.where(lane < DA_QK_DIM, q, zero), jnp.where(lane >= DA_QK_DIM, q, zero)],
                                   axis=0))

    def rows(j, nblk):
        start = j * TQ if isinstance(j, int) else pl.multiple_of(j * TQ, TQ)
        return pl.ds(start, nblk * TQ)

    def update(j, nblk, bias_cols):
        for hh in range(ATT_HEADS):
            hcols = slice(hh * LANES, (hh + 1) * LANES)
            s = lax.dot_general(qqs[hh], k_ref[rows(j, nblk), hcols], (((1,), (1,)), ((), ())),
                                preferred_element_type=F32)
            if bias_cols is not None:
                s = s + bias_ref[hh, :, bias_cols]
            tiles = [s[:, t * LANES:(t + 1) * LANES] for t in range(nblk * TQ // LANES)]
            m_old = m_ref[hh]
            m_new = jnp.maximum(m_old, jnp.max(functools.reduce(jnp.maximum, tiles), axis=-1, keepdims=True))
            alpha = jnp.exp(m_old - m_new)
            ps = [jnp.exp(t - m_new) for t in tiles]
            p = jnp.concatenate(ps, axis=1).astype(BF16)
            l_ref[hh] = alpha * l_ref[hh] + jnp.sum(functools.reduce(jnp.add, ps), axis=-1, keepdims=True)
            acc_ref[hh] = alpha * acc_ref[hh] + jnp.dot(p, v_ref[rows(j, nblk), hcols], preferred_element_type=F32)
            m_ref[hh] = m_new

    m_ref[...] = jnp.full_like(m_ref, MASK_NEG)
    l_ref[...] = jnp.zeros_like(l_ref)
    acc_ref[...] = jnp.zeros_like(acc_ref)

    @pl.when(i == 0)
    def _():
        update(0, 1, slice(TQ, 2 * TQ))

    @pl.when(i >= 1)
    def _():
        update(i - 1, 2, slice(0, 2 * TQ))

    n_far = jnp.maximum(i - 1, 0)
    odd = n_far % 2

    @pl.when(odd == 1)
    def _():
        update(0, 1, None)

    def far_pair(jj, carry):
        update(odd + 2 * jj, 2, None)
        return carry

    lax.fori_loop(0, n_far // 2, far_pair, 0)

    lam_v = lam_ref[...]
    lam = (jnp.exp(jnp.sum(lam_v[0:1] * lam_v[1:2], axis=-1, keepdims=True))
           - jnp.exp(jnp.sum(lam_v[2:3] * lam_v[3:4], axis=-1, keepdims=True)) + lam_init)
    for hh in range(ATT_HEADS):
        o = acc_ref[hh] / l_ref[hh]
        out = o[:TQ] - lam * o[TQ:]
        o_ref[:, hh * LANES:(hh + 1) * LANES] = (_rms(out, g_ref[hh]) * (1.0 - lam_init)).astype(BF16)


def _attention(proj, bias, lam_vecs, norm_g, B, S, lam_init):
    nq = S // TQ
    hw = ATT_HEADS * LANES
    q_off, k_off, v_off = 0, DA_WIDTH // hw, 2 * DA_WIDTH // hw
    return pl.pallas_call(
        functools.partial(_attn_kernel, lam_init),
        grid=(B, DA_HEADS // ATT_HEADS, nq),
        in_specs=[
            pl.BlockSpec((4, DA_QK_DIM), lambda b, h, i: (0, 0)),
            pl.BlockSpec((TQ, hw), lambda b, h, i: (b * nq + i, q_off + h)),
            pl.BlockSpec((S, hw), lambda b, h, i: (b, k_off + h)),
            pl.BlockSpec((S, hw), lambda b, h, i: (b, v_off + h)),
            pl.BlockSpec((ATT_HEADS, 2 * TQ, 2 * TQ), lambda b, h, i: (h, 0, 0)),
            pl.BlockSpec((ATT_HEADS, 1, LANES), lambda b, h, i: (h, 0, 0)),
        ],
        out_specs=pl.BlockSpec((TQ, hw), lambda b, h, i: (b * nq + i, h)),
        out_shape=jax.ShapeDtypeStruct((B * S, DA_WIDTH), BF16),
        scratch_shapes=[
            pltpu.VMEM((ATT_HEADS, 2 * TQ, DA_V_DIM), F32),
            pltpu.VMEM((ATT_HEADS, 2 * TQ, LANES), F32),
            pltpu.VMEM((ATT_HEADS, 2 * TQ, LANES), F32),
        ],
        compiler_params=pltpu.CompilerParams(
            dimension_semantics=("parallel", "parallel", "arbitrary"), vmem_limit_bytes=VMEM_LIMIT),
        name="diff_attention",
    )(lam_vecs, proj, proj, proj, bias, norm_g)


def _prefix_sum(x, axis):
    n = x.shape[axis]
    idx = lax.broadcasted_iota(jnp.int32, x.shape, axis)
    d = 1
    while d < n:
        x = x + jnp.where(idx >= d, pltpu.roll(x, d, axis), 0.0)
        d *= 2
    return x


def _mlstm_kernel(qk_ref, v_ref, og_ref, g_ref, gt_ref, cw_ref, cb_ref, gb_ref, gbt_ref, ob_ref, ng_ref,
                  out_ref, prev_ref, cn_ref, m_ref):
    L = ML_CHUNK
    H = ML_HEADS
    QW = H * ML_QK_DIM

    @pl.when(pl.program_id(1) == 0)
    def _():
        prev_ref[...] = jnp.zeros_like(prev_ref)
        cn_ref[...] = jnp.zeros_like(cn_ref)
        m_ref[...] = jnp.zeros_like(m_ref)

    x = qk_ref[...].astype(F32)
    prev = prev_ref[...]
    row = lax.broadcasted_iota(jnp.int32, x.shape, 0)
    y = cb_ref[...] + cw_ref[ML_CONV - 1:ML_CONV, :] * x
    for d in range(1, ML_CONV):
        shifted = jnp.where(row < d, pltpu.roll(prev, d, 0), pltpu.roll(x, d, 0))
        y = y + cw_ref[ML_CONV - 1 - d:ML_CONV - d, :] * shifted
    prev_ref[...] = x
    qkc = y * jax.nn.sigmoid(y)
    q = qkc[:, :QW]
    k = qkc[:, QW:] * (ML_QK_DIM ** -0.5)
    kb = k.astype(BF16)

    G = g_ref[...]
    li_all = G[:, :LANES] + gb_ref[:, :LANES]
    b_all = _prefix_sum(_log_sigmoid(G[:, LANES:] + gb_ref[:, LANES:]), 0)
    GT = gt_ref[...] + gbt_ref[...]
    bT = _prefix_sum(_log_sigmoid(GT), 1)

    qlane = lax.broadcasted_iota(jnp.int32, (L, QW), 1) // ML_QK_DIM
    qz = jnp.concatenate([jnp.where(qlane == h, q, 0.0) for h in range(H)], axis=0).astype(BF16)
    s_all = lax.dot_general(qz, kb, (((1,), (1,)), ((), ())), preferred_element_type=F32)
    cn_old = cn_ref[...]
    qcn = jnp.dot(qz, cn_old.astype(BF16), preferred_element_type=F32)

    t_idx = lax.broadcasted_iota(jnp.int32, (L, L), 0)
    j_idx = lax.broadcasted_iota(jnp.int32, (L, L), 1)
    causal = j_idx <= t_idx
    m_old_row = m_ref[...]
    mlane = lax.broadcasted_iota(jnp.int32, (1, LANES), 1)
    crow = lax.broadcasted_iota(jnp.int32, (QW, 1), 0) // ML_QK_DIM
    m_new_row = m_old_row
    wk_exp = jnp.zeros((L, QW), F32)
    decay_col = jnp.zeros((QW, 1), F32)

    for h in range(H):
        li_c = li_all[:, h:h + 1]
        b_c = b_all[:, h:h + 1]
        li_r = GT[h:h + 1, :]
        b_r = bT[H + h:H + h + 1, :]
        m_prev = m_old_row[:, h:h + 1]

        dlog = jnp.where(causal, b_c - b_r + li_r, MASK_NEG)
        inter = b_c + m_prev
        m_t = jnp.maximum(jnp.max(dlog, axis=-1, keepdims=True), inter)
        w_intra = jnp.exp(dlog - m_t)
        w_inter = jnp.exp(inter - m_t)
        s = s_all[h * L:(h + 1) * L, :] * w_intra
        vh = v_ref[:, h * ML_V_DIM:(h + 1) * ML_V_DIM]
        num = (jnp.dot(s.astype(BF16), vh, preferred_element_type=F32)
               + w_inter * qcn[h * L:(h + 1) * L, :ML_V_DIM])
        den = jnp.sum(s, axis=-1, keepdims=True) + w_inter * qcn[h * L:(h + 1) * L, ML_V_DIM:]
        hval = num / jnp.maximum(jnp.abs(den), jnp.exp(-m_t))

        cols = slice(h * ML_V_DIM, (h + 1) * ML_V_DIM)
        gate = jax.nn.sigmoid(og_ref[:, cols].astype(F32) + ob_ref[:, cols])
        out_ref[:, cols] = (_rms(hval, ng_ref[:, cols]) * gate).astype(BF16)

        g_tot = b_c[L - 1:L, :]
        wlog = g_tot - b_c + li_c
        m_new = jnp.maximum(g_tot + m_prev, jnp.max(wlog, axis=0, keepdims=True))
        decay = jnp.exp(g_tot + m_prev - m_new)
        wk = jnp.exp(wlog - m_new)
        m_new_row = jnp.where(mlane == h, m_new, m_new_row)
        wk_exp = jnp.where(qlane == h, wk, wk_exp)
        decay_col = jnp.where(crow == h, decay, decay_col)

    kw = (k * wk_exp).astype(BF16)
    vext = jnp.concatenate([v_ref[...], jnp.ones((L, LANES), BF16)], axis=1)
    upd = lax.dot_general(kw, vext, (((0,), (0,)), ((), ())), preferred_element_type=F32)
    for h in range(H):
        rows = slice(h * ML_QK_DIM, (h + 1) * ML_QK_DIM)
        dh = decay_col[rows, :]
        cn_ref[rows, :ML_V_DIM] = dh * cn_old[rows, :ML_V_DIM] + upd[rows, h * ML_V_DIM:(h + 1) * ML_V_DIM]
        cn_ref[rows, ML_V_DIM:] = dh * cn_old[rows, ML_V_DIM:] + upd[rows, H * ML_V_DIM:]
    m_ref[...] = m_new_row


def _mlstm(proj, gates, gates_t, conv_w, conv_b, gate_b, gate_bt, o_b, norm_g, B, S):
    L = ML_CHUNK
    nc = S // L
    qk_blk = 3 * DA_WIDTH // ML_QK_COLS
    v_blk, o_blk = qk_blk + 1, qk_blk + 2
    full = lambda shape: pl.BlockSpec(shape, lambda b, c: (0,) * len(shape))
    return pl.pallas_call(
        _mlstm_kernel,
        grid=(B, nc),
        in_specs=[
            pl.BlockSpec((L, ML_QK_COLS), lambda b, c: (b * nc + c, qk_blk)),
            pl.BlockSpec((L, ML_WIDTH), lambda b, c: (b * nc + c, v_blk)),
            pl.BlockSpec((L, ML_WIDTH), lambda b, c: (b * nc + c, o_blk)),
            pl.BlockSpec((L, 2 * LANES), lambda b, c: (b * nc + c, 0)),
            pl.BlockSpec((2 * ML_HEADS, L), lambda b, c: (0, b * nc + c)),
            full((ML_CONV, ML_QK_COLS)),
            full((1, ML_QK_COLS)),
            full((1, 2 * LANES)),
            full((2 * ML_HEADS, 1)),
            full((1, ML_WIDTH)),
            full((1, ML_WIDTH)),
        ],
        out_specs=pl.BlockSpec((L, ML_WIDTH), lambda b, c: (b * nc + c, 0)),
        out_shape=jax.ShapeDtypeStruct((B * S, ML_WIDTH), BF16),
        scratch_shapes=[
            pltpu.VMEM((L, ML_QK_COLS), F32),
            pltpu.VMEM((ML_HEADS * ML_QK_DIM, 2 * ML_V_DIM), F32),
            pltpu.VMEM((1, LANES), F32),
        ],
        compiler_params=pltpu.CompilerParams(
            dimension_semantics=("parallel", "arbitrary"), vmem_limit_bytes=VMEM_LIMIT),
        name="mlstm",
    )(proj, proj, proj, gates, gates_t, conv_w, conv_b, gate_b, gate_bt, o_b, norm_g)


def _outproj_kernel(a_ref, m_ref, x_ref, wo_ref, g_ref, rw_ref, rb_ref, x1_ref, h2_ref, idx_ref, gate_ref):
    x1 = (x_ref[...]
          + jnp.dot(a_ref[...], wo_ref[:DA_WIDTH, :], preferred_element_type=F32)
          + jnp.dot(m_ref[...], wo_ref[DA_WIDTH:, :], preferred_element_type=F32))
    x1_ref[...] = x1
    h2 = _rms(x1, g_ref[...])
    for j, piece in enumerate(_pack_rows(h2)):
        h2_ref[j] = piece
    work = jnp.dot(h2.astype(BF16), rw_ref[...], preferred_element_type=F32) + rb_ref[...]
    lane = lax.broadcasted_iota(jnp.int32, work.shape, 1).astype(F32)
    idxs = jnp.zeros_like(work)
    vals = jnp.zeros_like(work)
    top = None
    for kk in range(TOP_K):
        mx = jnp.max(work, axis=-1, keepdims=True)
        am = jnp.min(jnp.where(work == mx, lane, float(LANES)), axis=-1, keepdims=True)
        idxs = jnp.where(lane == kk, am, idxs)
        vals = jnp.where(lane == kk, mx, vals)
        work = jnp.where(lane == am, -jnp.inf, work)
        if kk == 0:
            top = mx
    e = jnp.where(lane < TOP_K, jnp.exp(vals - top), 0.0)
    gate_ref[...] = e / jnp.sum(e, axis=-1, keepdims=True)
    idx_ref[...] = idxs.astype(jnp.int32)


def _outproj(a_out, m_out, x2, w_out, g, router_w, router_b):
    T = x2.shape[0]
    row = lambda w: pl.BlockSpec((TM_PROJ, w), lambda i: (i, 0))
    full = lambda r, c: pl.BlockSpec((r, c), lambda i: (0, 0))
    return pl.pallas_call(
        _outproj_kernel,
        grid=(T // TM_PROJ,),
        in_specs=[row(DA_WIDTH), row(ML_WIDTH), row(D_MODEL), full(D_MODEL, D_MODEL), full(1, D_MODEL),
                  full(D_MODEL, LANES), full(1, LANES)],
        out_specs=[row(D_MODEL), pl.BlockSpec((ROW_PARTS, TM_PROJ, PART_W), lambda i: (0, i, 0)),
                   row(LANES), row(LANES)],
        out_shape=[jax.ShapeDtypeStruct((T, D_MODEL), F32), jax.ShapeDtypeStruct((ROW_PARTS, T, PART_W), jnp.uint32),
                   jax.ShapeDtypeStruct((T, LANES), jnp.int32), jax.ShapeDtypeStruct((T, LANES), F32)],
        compiler_params=pltpu.CompilerParams(dimension_semantics=("parallel",), vmem_limit_bytes=VMEM_LIMIT),
        name="outproj_router",
    )(a_out, m_out, x2, w_out, g, router_w, router_b)


def _sc_mesh():
    return plsc.VectorSubcoreMesh(core_axis_name="core", subcore_axis_name="subcore")


def _sc_gather_rows(x, idx):
    n = idx.shape[1]

    @functools.partial(pl.kernel, out_type=jax.ShapeDtypeStruct((n, x.shape[1]), x.dtype), mesh=_sc_mesh())
    def gather_kernel(x_hbm, i_hbm, o_hbm):
        def body(i_vmem, o_vmem):
            pltpu.sync_copy(x_hbm.at[i_vmem.at[0]], o_vmem)

        pltpu.emit_pipeline(
            body,
            grid=(n // SC_WINDOW,),
            in_specs=[pl.BlockSpec((1, SC_WINDOW), lambda i: (0, i))],
            out_specs=[pl.BlockSpec((SC_WINDOW, x.shape[1]), lambda i: (i, 0))],
            core_axis_name=("core", "subcore"),
            dimension_semantics=(pltpu.PARALLEL,),
        )(i_hbm, o_hbm)

    return gather_kernel(x, idx)


def _sc_scatter_rows(y, idx):
    n = idx.shape[1]

    @functools.partial(pl.kernel, out_type=jax.ShapeDtypeStruct(y.shape, y.dtype), mesh=_sc_mesh())
    def scatter_kernel(y_hbm, i_hbm, o_hbm):
        def body(y_vmem, i_vmem):
            pltpu.sync_copy(y_vmem, o_hbm.at[i_vmem.at[0]])

        pltpu.emit_pipeline(
            body,
            grid=(n // SC_WINDOW,),
            in_specs=[pl.BlockSpec((SC_WINDOW, y.shape[1]), lambda i: (i, 0)),
                      pl.BlockSpec((1, SC_WINDOW), lambda i: (0, i))],
            out_specs=[],
            core_axis_name=("core", "subcore"),
            dimension_semantics=(pltpu.PARALLEL,),
        )(y_hbm, i_hbm)

    return scatter_kernel(y, idx)


def _expert_kernel(first_block, has_alias, be_ref, nu_ref, x_ref, wgu_ref, bgu_ref, wd_ref, bd_ref, *rest):
    y_ref, wgu_bf, wd_bf = rest[1:] if has_alias else rest
    step = pl.program_id(0)
    blk = first_block + step

    @pl.when(blk < nu_ref[0])
    def _():
        @pl.when((step == 0) | (be_ref[blk] != be_ref[jnp.maximum(blk - 1, 0)]))
        def _():
            wgu_bf[...] = wgu_ref[0].astype(BF16)
            wd_bf[...] = wd_ref[0].astype(BF16)

        xb = _unpack_rows([x_ref[j] for j in range(ROW_PARTS)]).astype(BF16)
        gu = jnp.dot(xb, wgu_bf[...], preferred_element_type=F32) + bgu_ref[0]
        glu = jnp.minimum(gu[:, :D_FF], SWIGLU_LIMIT)
        lin = jnp.clip(gu[:, D_FF:], -SWIGLU_LIMIT, SWIGLU_LIMIT)
        act = glu * jax.nn.sigmoid(SWIGLU_ALPHA * glu) * (lin + 1.0)
        y = jnp.dot(act.astype(BF16), wd_bf[...], preferred_element_type=F32) + bd_ref[0]
        for j, piece in enumerate(_pack_rows(y)):
            y_ref[j] = piece

    @pl.when(blk >= nu_ref[0])
    def _():
        y_ref[...] = jnp.zeros_like(y_ref)


def _experts(h2, block_expert, n_used, row_tok, row_dst, w_gu, b_gu, w_down, b_down):
    R = MOE_ROWS
    T = h2.shape[1]
    n_rows = row_tok.shape[0]
    n_blocks = n_rows // R
    bounds = sorted({0, n_blocks // (4 * MOE_GROUPS), n_blocks // MOE_GROUPS}
                    | {n_blocks * g // MOE_GROUPS for g in range(2, MOE_GROUPS + 1)})
    groups = list(zip(bounds[:-1], bounds[1:]))
    part = jnp.arange(ROW_PARTS, dtype=jnp.int32)[:, None]
    h2_flat = h2.reshape(ROW_PARTS * T, PART_W)
    xs = []
    for b0, b1 in groups:
        tok = row_tok[b0 * R:b1 * R]
        src_idx = (part * T + tok[None, :]).reshape(1, ROW_PARTS * (b1 - b0) * R)
        xs.append(_sc_gather_rows(h2_flat, src_idx).reshape(ROW_PARTS, (b1 - b0) * R, PART_W))

    def run_group(grp, ys):
        first, g_blocks = groups[grp][0], groups[grp][1] - groups[grp][0]
        wspec = lambda shape: pl.BlockSpec(shape, lambda i, be, nu: (be[first + i], 0, 0))
        in_specs = [
            pl.BlockSpec((ROW_PARTS, R, PART_W), lambda i, be, nu: (0, i, 0)),
            wspec((1, D_MODEL, 2 * D_FF)), wspec((1, 1, 2 * D_FF)), wspec((1, D_FF, D_MODEL)), wspec((1, 1, D_MODEL)),
        ]
        operands = [block_expert, n_used, xs[grp], w_gu, b_gu, w_down, b_down]
        aliases = {}
        if ys is not None:
            in_specs.append(pl.BlockSpec(memory_space=pl.ANY))
            aliases = {len(operands): 0}
            operands.append(ys)
        return pl.pallas_call(
            functools.partial(_expert_kernel, first, ys is not None),
            grid_spec=pltpu.PrefetchScalarGridSpec(
                num_scalar_prefetch=2,
                grid=(g_blocks,),
                in_specs=in_specs,
                out_specs=pl.BlockSpec((ROW_PARTS, R, PART_W), lambda i, be, nu: (0, first + i, 0)),
                scratch_shapes=[pltpu.VMEM((D_MODEL, 2 * D_FF), BF16), pltpu.VMEM((D_FF, D_MODEL), BF16)],
            ),
            out_shape=jax.ShapeDtypeStruct((ROW_PARTS, n_rows, PART_W), jnp.uint32),
            input_output_aliases=aliases,
            compiler_params=pltpu.CompilerParams(dimension_semantics=("arbitrary",), vmem_limit_bytes=VMEM_LIMIT),
            name=f"experts_g{grp}",
        )(*operands)

    ys = None
    for grp in range(len(groups)):
        ys = run_group(grp, ys)
    dst_idx = (part * n_rows + row_dst[None, :]).reshape(1, ROW_PARTS * n_rows)
    out = _sc_scatter_rows(ys.reshape(ROW_PARTS * n_rows, PART_W), dst_idx)
    return out.reshape(ROW_PARTS, n_rows, PART_W)


def _combine_kernel(x1_ref, gate_ref, g_ref, *refs):
    o_refs, out_ref = refs[:TOP_K], refs[TOP_K]
    y = x1_ref[...]
    gt = gate_ref[...]
    for kk in range(TOP_K):
        y = y + gt[:, kk:kk + 1] * _unpack_rows([o_refs[kk][j] for j in range(ROW_PARTS)])
    out_ref[...] = _rms(y, g_ref[...])


def _combine(x1, o4, gates, g):
    T = x1.shape[0]
    tm = TM_COMBINE
    nt = T // tm
    slot_spec = lambda kk: pl.BlockSpec((ROW_PARTS, tm, PART_W), lambda i: (0, kk * nt + i, 0))
    return pl.pallas_call(
        _combine_kernel,
        grid=(nt,),
        in_specs=[
            pl.BlockSpec((tm, D_MODEL), lambda i: (i, 0)),
            pl.BlockSpec((tm, LANES), lambda i: (i, 0)),
            pl.BlockSpec((1, D_MODEL), lambda i: (0, 0)),
        ] + [slot_spec(kk) for kk in range(TOP_K)],
        out_specs=pl.BlockSpec((tm, D_MODEL), lambda i: (i, 0)),
        out_shape=jax.ShapeDtypeStruct((T, D_MODEL), F32),
        compiler_params=pltpu.CompilerParams(dimension_semantics=("parallel",), vmem_limit_bytes=VMEM_LIMIT),
        name="combine_norm",
    )(x1, gates, g, *([o4] * TOP_K))


def _dispatch_plan(top_idx, T):
    R = MOE_ROWS
    P = T * TOP_K
    e_flat = top_idx.reshape(P)
    assert N_EXPERTS * P < 2 ** 31
    order = jnp.sort(e_flat * P + jnp.arange(P, dtype=jnp.int32)) % P
    counts = jnp.sum(e_flat[:, None] == jnp.arange(N_EXPERTS, dtype=jnp.int32)[None, :], axis=0, dtype=jnp.int32)
    padded = (counts + R - 1) // R * R
    start = jnp.cumsum(counts) - counts
    pend = jnp.cumsum(padded)
    pstart = pend - padded
    n_blocks = P // R + N_EXPERTS
    block_row0 = jnp.arange(n_blocks, dtype=jnp.int32) * R
    block_expert = jnp.minimum(jnp.sum(pend[None, :] <= block_row0[:, None], axis=1, dtype=jnp.int32), N_EXPERTS - 1)
    within = block_row0 - pstart[block_expert]
    n_valid = jnp.where(block_row0 < pend[-1], jnp.clip(counts[block_expert] - within, 0, R), 0).astype(jnp.int32)
    r_in = jnp.arange(R, dtype=jnp.int32)[None, :]
    src = jnp.clip((start[block_expert] + within)[:, None] + r_in, 0, P - 1)
    is_valid = (r_in < n_valid[:, None]).reshape(-1)
    pair = jnp.where(is_valid, order[src].reshape(-1), 0)
    row_tok = jnp.where(is_valid, pair // TOP_K, jnp.arange(n_blocks * R, dtype=jnp.int32) % T)
    pad_rank = jnp.cumsum(1 - is_valid.astype(jnp.int32)) - 1
    row_dst = jnp.where(is_valid, (pair % TOP_K) * T + row_tok, P + pad_rank)
    n_used = (pend[-1] // R).reshape(1).astype(jnp.int32)
    return block_expert, n_used, row_tok, row_dst


def _layer(x2, B, S, layer, norm_attn_g, w_in, rel_bias_table, lam_vecs, diff_norm_g, conv_w, conv_b, gate_b,
           o_b, ml_norm_g, w_out, norm_ffn_g, router_w, router_b, w_gu, b_gu, w_down, b_down):
    T = B * S
    H = ML_HEADS
    w_main = w_in[:, :N_MAIN].astype(BF16)
    wg = w_in[:, N_MAIN:]
    w_gate = jnp.zeros((D_MODEL, 2 * LANES), F32).at[:, :H].set(wg[:, :H]).at[:, LANES:LANES + H].set(wg[:, H:])
    proj, gates = _inproj(x2, norm_attn_g.reshape(1, D_MODEL), w_main, w_gate.astype(BF16))

    lam_init = 0.8 - 0.6 * math.exp(-0.3 * layer)
    bias = _relbias(rel_bias_table.astype(F32))
    a_out = _attention(proj, bias, lam_vecs, diff_norm_g.reshape(DA_HEADS, 1, DA_V_DIM), B, S, lam_init)

    gates_t = jnp.concatenate([gates[:, :H], gates[:, LANES:LANES + H]], axis=1).T
    gate_b_pad = jnp.zeros((1, 2 * LANES), F32).at[0, :H].set(gate_b[:H]).at[0, LANES:LANES + H].set(gate_b[H:])
    m_out = _mlstm(proj, gates, gates_t, conv_w, conv_b.reshape(1, -1), gate_b_pad, gate_b.reshape(2 * H, 1),
                   o_b.reshape(1, -1), ml_norm_g.reshape(1, -1), B, S)

    rw = jnp.zeros((D_MODEL, LANES), F32).at[:, :N_EXPERTS].set(router_w).astype(BF16)
    rb = jnp.full((1, LANES), MASK_NEG, F32).at[0, :N_EXPERTS].set(router_b)
    x1, h2, top_idx, top_gate = _outproj(a_out, m_out, x2, w_out.astype(BF16), norm_ffn_g.reshape(1, D_MODEL), rw, rb)

    block_expert, n_used, row_tok, row_dst = _dispatch_plan(top_idx[:, :TOP_K], T)
    o4 = _experts(h2, block_expert, n_used, row_tok, row_dst, w_gu, b_gu.reshape(N_EXPERTS, 1, 2 * D_FF),
                  w_down, b_down.reshape(N_EXPERTS, 1, D_MODEL))
    return x1, o4, top_gate


def kernel(x, norm_attn_g, w_in, rel_bias_table, lambda_q1, lambda_k1, lambda_q2, lambda_k2, diff_norm_g,
           mlstm_conv_w, mlstm_conv_b, mlstm_gate_b, mlstm_o_b, mlstm_norm_g, w_out, norm_ffn_g, router_w,
           router_b, expert_w_gu, expert_b_gu, expert_w_down, expert_b_down, final_norm_g):
    B, S, _ = x.shape
    depth = w_in.shape[0]
    assert depth == 1, "the combine kernel fuses the last layer's MoE residual with the final norm"
    assert S % TQ == 0 and S % ML_CHUNK == 0 and (B * S) % TM_PROJ == 0 and (B * S * TOP_K) % MOE_ROWS == 0
    x2 = x.reshape(B * S, D_MODEL)
    l = 0
    lam_vecs = jnp.stack([lambda_q1[l], lambda_k1[l], lambda_q2[l], lambda_k2[l]])
    x1, o4, top_gate = _layer(
        x2, B, S, l, norm_attn_g[l], w_in[l], rel_bias_table, lam_vecs, diff_norm_g[l], mlstm_conv_w[l],
        mlstm_conv_b[l], mlstm_gate_b[l], mlstm_o_b[l], mlstm_norm_g[l], w_out[l], norm_ffn_g[l], router_w[l],
        router_b[l], expert_w_gu[l], expert_b_gu[l], expert_w_down[l], expert_b_down[l])
    out = _combine(x1, o4, top_gate, final_norm_g.reshape(1, D_MODEL))
    return out.reshape(B, S, D_MODEL)
```

```python
import functools
import math

import numpy as np
import jax
import jax.numpy as jnp
from jax import lax
from jax.experimental import pallas as pl
from jax.experimental.pallas import tpu as pltpu
from jax.experimental.pallas import tpu_sc as plsc

F32 = jnp.float32
BF16 = jnp.bfloat16

D_MODEL = 1024
NORM_EPS = 1e-5
DA_HEADS = 4
DA_QK_DIM = 64
DA_V_DIM = 128
DA_WIDTH = DA_HEADS * DA_V_DIM
REL_BUCKETS = 32
REL_MAX_DIST = 128
ML_HEADS = 4
ML_QK_DIM = 64
ML_V_DIM = 128
ML_WIDTH = ML_HEADS * ML_V_DIM
ML_QK_COLS = 2 * ML_HEADS * ML_QK_DIM
ML_CONV = 4
N_EXPERTS = 32
TOP_K = 4
D_FF = D_MODEL
SWIGLU_LIMIT = 7.0
SWIGLU_ALPHA = 1.702

N_MAIN = 3 * DA_WIDTH + ML_QK_COLS + 2 * ML_WIDTH
LANES = 128
SUBLANES = 8
MASK_NEG = -1e30

TM_PROJ = 512
TQ = 256
ATT_HEADS = 4
FAR_BLOCKS = 4
ML_CHUNK = 256
MOE_ROWS = 512
TM_COMBINE = 512
ROW_PARTS = 2
PART_W = D_MODEL // (2 * ROW_PARTS)
SC_WINDOW = 128
MOE_GROUP_BOUNDS = (0, 1, 3, 7, 12, 17, 22, 27, 30, 32)
VMEM_LIMIT = 56 * 1024 * 1024


def _bucket_lower_bounds():
    n = np.arange(0, 4 * REL_MAX_DIST)
    max_exact = REL_BUCKETS // 2
    nf = np.maximum(n, 1).astype(np.float32)
    large = max_exact + (np.log(nf / np.float32(max_exact)) / np.float32(math.log(REL_MAX_DIST / max_exact))
                         * np.float32(REL_BUCKETS - max_exact)).astype(np.int32)
    large = np.minimum(large, REL_BUCKETS - 1)
    bucket = np.where(n < max_exact, n, large)
    return [int(np.argmax(bucket >= b)) for b in range(REL_BUCKETS)]


BUCKET_LO = _bucket_lower_bounds()
assert BUCKET_LO[-1] <= TQ, "keys two blocks away must all sit in the last bucket"


def _rms(x, g):
    return x * lax.rsqrt(jnp.mean(x * x, axis=-1, keepdims=True) + NORM_EPS) * g


def _pack_rows(v):
    bits = lax.bitcast_convert_type(v.astype(BF16).astype(F32), jnp.uint32)
    return [bits[:, (2 * j + 1) * PART_W:(2 * j + 2) * PART_W] | (bits[:, 2 * j * PART_W:(2 * j + 1) * PART_W] >> 16)
            for j in range(ROW_PARTS)]


def _unpack_rows(pieces):
    cols = []
    for w in pieces:
        cols.append(lax.bitcast_convert_type(w << 16, F32))
        cols.append(lax.bitcast_convert_type(w & jnp.uint32(0xFFFF0000), F32))
    return jnp.concatenate(cols, axis=1)


def _log_sigmoid(x):
    return jnp.minimum(x, 0.0) - jnp.log1p(jnp.exp(-jnp.abs(x)))


def _inproj_kernel(x_ref, g_ref, w_ref, wg_ref, o_ref, og_ref, vx_ref):
    hb = _rms(x_ref[...], g_ref[...]).astype(BF16)
    for n in range(N_MAIN // DA_WIDTH):
        cols = slice(n * DA_WIDTH, (n + 1) * DA_WIDTH)
        chunk = jnp.dot(hb, w_ref[:, cols], preferred_element_type=F32).astype(BF16)
        o_ref[:, cols] = chunk
        if n == 2:
            ones = jnp.ones((chunk.shape[0], LANES), BF16)
            for h in range(DA_HEADS):
                vx_ref[:, 2 * h * LANES:(2 * h + 1) * LANES] = chunk[:, h * DA_V_DIM:(h + 1) * DA_V_DIM]
                vx_ref[:, (2 * h + 1) * LANES:(2 * h + 2) * LANES] = ones
    og_ref[...] = jnp.dot(hb, wg_ref[...], preferred_element_type=F32)


def _inproj(x2, g, w_main, w_gate):
    T = x2.shape[0]
    return pl.pallas_call(
        _inproj_kernel,
        grid=(T // TM_PROJ,),
        in_specs=[
            pl.BlockSpec((TM_PROJ, D_MODEL), lambda i: (i, 0)),
            pl.BlockSpec((1, D_MODEL), lambda i: (0, 0)),
            pl.BlockSpec((D_MODEL, N_MAIN), lambda i: (0, 0)),
            pl.BlockSpec((D_MODEL, 2 * LANES), lambda i: (0, 0)),
        ],
        out_specs=[
            pl.BlockSpec((TM_PROJ, N_MAIN), lambda i: (i, 0)),
            pl.BlockSpec((TM_PROJ, 2 * LANES), lambda i: (i, 0)),
            pl.BlockSpec((TM_PROJ, 2 * DA_WIDTH), lambda i: (i, 0)),
        ],
        out_shape=[jax.ShapeDtypeStruct((T, N_MAIN), BF16), jax.ShapeDtypeStruct((T, 2 * LANES), F32),
                   jax.ShapeDtypeStruct((T, 2 * DA_WIDTH), BF16)],
        compiler_params=pltpu.CompilerParams(dimension_semantics=("parallel",), vmem_limit_bytes=VMEM_LIMIT),
        name="inproj",
    )(x2, g, w_main, w_gate)


def _relbias_kernel(tbl_ref, o_ref):
    h = pl.program_id(0)
    r = lax.broadcasted_iota(jnp.int32, (2 * TQ, 2 * TQ), 0) & (TQ - 1)
    c = lax.broadcasted_iota(jnp.int32, (2 * TQ, 2 * TQ), 1)
    rel = r + TQ - c
    val = jnp.full((2 * TQ, 2 * TQ), tbl_ref[0, h], F32)
    for b in range(1, REL_BUCKETS):
        val = jnp.where(rel >= BUCKET_LO[b], tbl_ref[b, h], val)
    o_ref[0] = jnp.where(rel >= 0, val - tbl_ref[REL_BUCKETS - 1, h], MASK_NEG)


def _relbias(table):
    return pl.pallas_call(
        _relbias_kernel,
        grid=(DA_HEADS,),
        in_specs=[pl.BlockSpec(memory_space=pltpu.SMEM)],
        out_specs=pl.BlockSpec((1, 2 * TQ, 2 * TQ), lambda h: (h, 0, 0)),
        out_shape=jax.ShapeDtypeStruct((DA_HEADS, 2 * TQ, 2 * TQ), F32),
        compiler_params=pltpu.CompilerParams(dimension_semantics=("parallel",)),
        name="relbias",
    )(table)


def _attn_kernel(lam_init, lam_ref, q_ref, k_ref, v_ref, bias_ref, g_ref, o_ref, acc_ref, m_ref):
    i = pl.program_id(2)
    lane = lax.broadcasted_iota(jnp.int32, (TQ, LANES), 1)
    qqs = []
    for hh in range(ATT_HEADS):
        q = q_ref[:, hh * LANES:(hh + 1) * LANES] * jnp.asarray(DA_QK_DIM ** -0.5, BF16)
        zero = jnp.zeros_like(q)
        qqs.append(jnp.concatenate([jnp.where(lane < DA_QK_DIM, q, zero), jnp.where(lane >= DA_QK_DIM, q, zero)],
                                   axis=0))

    def rows(j, nblk):
        start = j * TQ if isinstance(j, int) else pl.multiple_of(j * TQ, TQ)
        return pl.ds(start, nblk * TQ)

    def update(j, nblk, bias_cols):
        for hh in range(ATT_HEADS):
            hcols = slice(hh * LANES, (hh + 1) * LANES)
            s = lax.dot_general(qqs[hh], k_ref[rows(j, nblk), hcols], (((1,), (1,)), ((), ())),
                                preferred_element_type=F32)
            if bias_cols is not None:
                s = s + bias_ref[hh, :, bias_cols]
            tiles = [s[:, t * LANES:(t + 1) * LANES] for t in range(nblk * TQ // LANES)]
            m_old = m_ref[hh]
            m_new = jnp.maximum(m_old, jnp.max(functools.reduce(jnp.maximum, tiles), axis=-1, keepdims=True))
            alpha = jnp.exp(m_old - m_new)
            p = jnp.concatenate([jnp.exp(t - m_new) for t in tiles], axis=1).astype(BF16)
            pv = jnp.dot(p, v_ref[rows(j, nblk), 2 * hh * LANES:(2 * hh + 2) * LANES], preferred_element_type=F32)
            for half in (slice(0, LANES), slice(LANES, 2 * LANES)):
                acc_ref[hh, :, half] = alpha * acc_ref[hh, :, half] + pv[:, half]
            m_ref[hh] = m_new

    m_ref[...] = jnp.full_like(m_ref, MASK_NEG)
    acc_ref[...] = jnp.zeros_like(acc_ref)

    @pl.when(i == 0)
    def _():
        update(0, 1, slice(TQ, 2 * TQ))

    @pl.when(i >= 1)
    def _():
        update(i - 1, 2, slice(0, 2 * TQ))

    n_far = jnp.maximum(i - 1, 0)
    rem = n_far % FAR_BLOCKS

    @pl.when(rem % 2 == 1)
    def _():
        update(0, 1, None)

    @pl.when(rem >= 2)
    def _():
        update(rem % 2, 2, None)

    def far_step(jj, carry):
        update(rem + FAR_BLOCKS * jj, FAR_BLOCKS, None)
        return carry

    lax.fori_loop(0, n_far // FAR_BLOCKS, far_step, 0)

    lam_v = lam_ref[...]
    lam = (jnp.exp(jnp.sum(lam_v[0:1] * lam_v[1:2], axis=-1, keepdims=True))
           - jnp.exp(jnp.sum(lam_v[2:3] * lam_v[3:4], axis=-1, keepdims=True)) + lam_init)
    for hh in range(ATT_HEADS):
        o = acc_ref[hh, :, :LANES] / acc_ref[hh, :, LANES:]
        out = o[:TQ] - lam * o[TQ:]
        o_ref[:, hh * LANES:(hh + 1) * LANES] = (_rms(out, g_ref[hh]) * (1.0 - lam_init)).astype(BF16)


def _attention(proj, v_ext, bias, lam_vecs, norm_g, B, S, lam_init):
    nq = S // TQ
    hw = ATT_HEADS * LANES
    q_off, k_off = 0, DA_WIDTH // hw
    return pl.pallas_call(
        functools.partial(_attn_kernel, lam_init),
        grid=(B, DA_HEADS // ATT_HEADS, nq),
        in_specs=[
            pl.BlockSpec((4, DA_QK_DIM), lambda b, h, i: (0, 0)),
            pl.BlockSpec((TQ, hw), lambda b, h, i: (b * nq + i, q_off + h)),
            pl.BlockSpec((S, hw), lambda b, h, i: (b, k_off + h)),
            pl.BlockSpec((S, 2 * hw), lambda b, h, i: (b, h)),
            pl.BlockSpec((ATT_HEADS, 2 * TQ, 2 * TQ), lambda b, h, i: (h, 0, 0)),
            pl.BlockSpec((ATT_HEADS, 1, LANES), lambda b, h, i: (h, 0, 0)),
        ],
        out_specs=pl.BlockSpec((TQ, hw), lambda b, h, i: (b * nq + i, h)),
        out_shape=jax.ShapeDtypeStruct((B * S, DA_WIDTH), BF16),
        scratch_shapes=[
            pltpu.VMEM((ATT_HEADS, 2 * TQ, 2 * DA_V_DIM), F32),
            pltpu.VMEM((ATT_HEADS, 2 * TQ, LANES), F32),
        ],
        compiler_params=pltpu.CompilerParams(
            dimension_semantics=("parallel", "parallel", "arbitrary"), vmem_limit_bytes=VMEM_LIMIT),
        name="diff_attention",
    )(lam_vecs, proj, proj, v_ext, bias, norm_g)


def _prefix_scan(x, axis, op, identity):
    n = x.shape[axis]
    idx = lax.broadcasted_iota(jnp.int32, x.shape, axis)
    d = 1
    while d < n:
        x = op(x, jnp.where(idx >= d, pltpu.roll(x, d, axis), identity))
        d *= 2
    return x


def _prefix_sum(x, axis):
    return _prefix_scan(x, axis, jnp.add, 0.0)


def _prefix_max(x, axis):
    return _prefix_scan(x, axis, jnp.maximum, -jnp.inf)


def _mlstm_kernel(qk_ref, v_ref, og_ref, g_ref, gt_ref, cw_ref, cb_ref, gb_ref, gbt_ref, ob_ref, ng_ref,
                  out_ref, prev_ref, cn_ref, m_ref):
    L = ML_CHUNK
    H = ML_HEADS
    QW = H * ML_QK_DIM

    @pl.when(pl.program_id(1) == 0)
    def _():
        prev_ref[...] = jnp.zeros_like(prev_ref)
        cn_ref[...] = jnp.zeros_like(cn_ref)
        m_ref[...] = jnp.zeros_like(m_ref)

    x = qk_ref[...].astype(F32)
    prev = prev_ref[...]
    row = lax.broadcasted_iota(jnp.int32, x.shape, 0)
    y = cb_ref[...] + cw_ref[ML_CONV - 1:ML_CONV, :] * x
    for d in range(1, ML_CONV):
        shifted = pltpu.roll(jnp.where(row >= L - d, prev, x), d, 0)
        y = y + cw_ref[ML_CONV - 1 - d:ML_CONV - d, :] * shifted
    prev_ref[...] = x
    qkc = y * jax.nn.sigmoid(y)
    q = qkc[:, :QW]
    k = qkc[:, QW:] * (ML_QK_DIM ** -0.5)
    kb = k.astype(BF16)

    G = g_ref[...]
    li_all = G[:, :LANES] + gb_ref[:, :LANES]
    b_all = _prefix_sum(_log_sigmoid(G[:, LANES:] + gb_ref[:, LANES:]), 0)
    GT = gt_ref[...] + gbt_ref[...]
    bT = _prefix_sum(_log_sigmoid(GT), 1)
    m_intra_all = b_all + _prefix_max(li_all - b_all, 0)

    qlane = lax.broadcasted_iota(jnp.int32, (L, QW), 1) // ML_QK_DIM
    qz = jnp.concatenate([jnp.where(qlane == h, q, 0.0) for h in range(H)], axis=0).astype(BF16)
    s_all = lax.dot_general(qz, kb, (((1,), (1,)), ((), ())), preferred_element_type=F32)
    cn_old = cn_ref[...]
    qcn = jnp.dot(qz, cn_old.astype(BF16), preferred_element_type=F32)

    t_idx = lax.broadcasted_iota(jnp.int32, (L, L), 0)
    j_idx = lax.broadcasted_iota(jnp.int32, (L, L), 1)
    causal = j_idx <= t_idx
    m_old_row = m_ref[...]
    mlane = lax.broadcasted_iota(jnp.int32, (1, LANES), 1)
    crow = lax.broadcasted_iota(jnp.int32, (QW, 1), 0) // ML_QK_DIM
    m_new_row = m_old_row
    wk_exp = jnp.zeros((L, QW), F32)
    decay_col = jnp.zeros((QW, 1), F32)

    for h in range(H):
        li_c = li_all[:, h:h + 1]
        b_c = b_all[:, h:h + 1]
        li_r = GT[h:h + 1, :]
        b_r = bT[H + h:H + h + 1, :]
        m_prev = m_old_row[:, h:h + 1]

        inter = b_c + m_prev
        m_t = jnp.maximum(m_intra_all[:, h:h + 1], inter)
        w_intra = jnp.exp(jnp.where(causal, (b_c - m_t) + (li_r - b_r), MASK_NEG))
        w_inter = jnp.exp(inter - m_t)
        s = (s_all[h * L:(h + 1) * L, :] * w_intra).astype(BF16)
        vh1 = jnp.concatenate([v_ref[:, h * ML_V_DIM:(h + 1) * ML_V_DIM], jnp.ones((L, LANES), BF16)], axis=1)
        nd = jnp.dot(s, vh1, preferred_element_type=F32) + w_inter * qcn[h * L:(h + 1) * L, :]
        hval = nd[:, :ML_V_DIM] / jnp.maximum(jnp.abs(nd[:, ML_V_DIM:]), jnp.exp(-m_t))

        cols = slice(h * ML_V_DIM, (h + 1) * ML_V_DIM)
        gate = jax.nn.sigmoid(og_ref[:, cols].astype(F32) + ob_ref[:, cols])
        out_ref[:, cols] = (_rms(hval, ng_ref[:, cols]) * gate).astype(BF16)

        g_tot = b_c[L - 1:L, :]
        wlog = g_tot - b_c + li_c
        m_new = jnp.maximum(g_tot + m_prev, jnp.max(wlog, axis=0, keepdims=True))
        decay = jnp.exp(g_tot + m_prev - m_new)
        wk = jnp.exp(wlog - m_new)
        m_new_row = jnp.where(mlane == h, m_new, m_new_row)
        wk_exp = jnp.where(qlane == h, wk, wk_exp)
        decay_col = jnp.where(crow == h, decay, decay_col)

    kw = (k * wk_exp).astype(BF16)
    vext = jnp.concatenate([v_ref[...], jnp.ones((L, LANES), BF16)], axis=1)
    upd = lax.dot_general(kw, vext, (((0,), (0,)), ((), ())), preferred_element_type=F32)
    for h in range(H):
        rows = slice(h * ML_QK_DIM, (h + 1) * ML_QK_DIM)
        dh = decay_col[rows, :]
        cn_ref[rows, :ML_V_DIM] = dh * cn_old[rows, :ML_V_DIM] + upd[rows, h * ML_V_DIM:(h + 1) * ML_V_DIM]
        cn_ref[rows, ML_V_DIM:] = dh * cn_old[rows, ML_V_DIM:] + upd[rows, H * ML_V_DIM:]
    m_ref[...] = m_new_row


def _mlstm(proj, gates, gates_t, conv_w, conv_b, gate_b, gate_bt, o_b, norm_g, B, S):
    L = ML_CHUNK
    nc = S // L
    qk_blk = 3 * DA_WIDTH // ML_QK_COLS
    v_blk, o_blk = qk_blk + 1, qk_blk + 2
    full = lambda shape: pl.BlockSpec(shape, lambda b, c: (0,) * len(shape))
    return pl.pallas_call(
        _mlstm_kernel,
        grid=(B, nc),
        in_specs=[
            pl.BlockSpec((L, ML_QK_COLS), lambda b, c: (b * nc + c, qk_blk)),
            pl.BlockSpec((L, ML_WIDTH), lambda b, c: (b * nc + c, v_blk)),
            pl.BlockSpec((L, ML_WIDTH), lambda b, c: (b * nc + c, o_blk)),
            pl.BlockSpec((L, 2 * LANES), lambda b, c: (b * nc + c, 0)),
            pl.BlockSpec((2 * ML_HEADS, L), lambda b, c: (0, b * nc + c)),
            full((ML_CONV, ML_QK_COLS)),
            full((1, ML_QK_COLS)),
            full((1, 2 * LANES)),
            full((2 * ML_HEADS, 1)),
            full((1, ML_WIDTH)),
            full((1, ML_WIDTH)),
        ],
        out_specs=pl.BlockSpec((L, ML_WIDTH), lambda b, c: (b * nc + c, 0)),
        out_shape=jax.ShapeDtypeStruct((B * S, ML_WIDTH), BF16),
        scratch_shapes=[
            pltpu.VMEM((L, ML_QK_COLS), F32),
            pltpu.VMEM((ML_HEADS * ML_QK_DIM, 2 * ML_V_DIM), F32),
            pltpu.VMEM((1, LANES), F32),
        ],
        compiler_params=pltpu.CompilerParams(
            dimension_semantics=("parallel", "arbitrary"), vmem_limit_bytes=VMEM_LIMIT),
        name="mlstm",
    )(proj, proj, proj, gates, gates_t, conv_w, conv_b, gate_b, gate_bt, o_b, norm_g)


def _outproj_kernel(a_ref, m_ref, x_ref, wo_ref, g_ref, rw_ref, rb_ref, x1_ref, h2_ref, idx_ref, gate_ref):
    x1 = (x_ref[...]
          + jnp.dot(a_ref[...], wo_ref[:DA_WIDTH, :], preferred_element_type=F32)
          + jnp.dot(m_ref[...], wo_ref[DA_WIDTH:, :], preferred_element_type=F32))
    x1_ref[...] = x1
    h2 = _rms(x1, g_ref[...])
    for j, piece in enumerate(_pack_rows(h2)):
        h2_ref[j] = piece
    work = jnp.dot(h2.astype(BF16), rw_ref[...], preferred_element_type=F32) + rb_ref[...]
    lane = lax.broadcasted_iota(jnp.int32, work.shape, 1).astype(F32)
    idxs = jnp.zeros_like(work)
    vals = jnp.zeros_like(work)
    top = None
    for kk in range(TOP_K):
        mx = jnp.max(work, axis=-1, keepdims=True)
        am = jnp.min(jnp.where(work == mx, lane, float(LANES)), axis=-1, keepdims=True)
        idxs = jnp.where(lane == kk, am, idxs)
        vals = jnp.where(lane == kk, mx, vals)
        work = jnp.where(lane == am, -jnp.inf, work)
        if kk == 0:
            top = mx
    e = jnp.where(lane < TOP_K, jnp.exp(vals - top), 0.0)
    gate_ref[...] = e / jnp.sum(e, axis=-1, keepdims=True)
    idx_ref[...] = jnp.transpose(idxs)[:SUBLANES].astype(jnp.int32)


def _outproj(a_out, m_out, x2, w_out, g, router_w, router_b):
    T = x2.shape[0]
    row = lambda w: pl.BlockSpec((TM_PROJ, w), lambda i: (i, 0))
    full = lambda r, c: pl.BlockSpec((r, c), lambda i: (0, 0))
    return pl.pallas_call(
        _outproj_kernel,
        grid=(T // TM_PROJ,),
        in_specs=[row(DA_WIDTH), row(ML_WIDTH), row(D_MODEL), full(D_MODEL, D_MODEL), full(1, D_MODEL),
                  full(D_MODEL, LANES), full(1, LANES)],
        out_specs=[row(D_MODEL), pl.BlockSpec((ROW_PARTS, TM_PROJ, PART_W), lambda i: (0, i, 0)),
                   pl.BlockSpec((SUBLANES, TM_PROJ), lambda i: (0, i)), row(LANES)],
        out_shape=[jax.ShapeDtypeStruct((T, D_MODEL), F32), jax.ShapeDtypeStruct((ROW_PARTS, T, PART_W), jnp.uint32),
                   jax.ShapeDtypeStruct((SUBLANES, T), jnp.int32), jax.ShapeDtypeStruct((T, LANES), F32)],
        compiler_params=pltpu.CompilerParams(dimension_semantics=("parallel",), vmem_limit_bytes=VMEM_LIMIT),
        name="outproj_router",
    )(a_out, m_out, x2, w_out, g, router_w, router_b)


def _sc_mesh():
    return plsc.VectorSubcoreMesh(core_axis_name="core", subcore_axis_name="subcore")


def _sc_gather_rows(x, idx, start, n):
    w0 = start // SC_WINDOW

    @functools.partial(pl.kernel, out_type=jax.ShapeDtypeStruct((n, x.shape[1]), x.dtype), mesh=_sc_mesh())
    def gather_kernel(x_hbm, i_hbm, o_hbm):
        def body(i_vmem, o_vmem):
            pltpu.sync_copy(x_hbm.at[i_vmem.at[0]], o_vmem)

        pltpu.emit_pipeline(
            body,
            grid=(n // SC_WINDOW,),
            in_specs=[pl.BlockSpec((1, SC_WINDOW), lambda i: (0, w0 + i))],
            out_specs=[pl.BlockSpec((SC_WINDOW, x.shape[1]), lambda i: (i, 0))],
            core_axis_name=("core", "subcore"),
            dimension_semantics=(pltpu.PARALLEL,),
        )(i_hbm, o_hbm)

    return gather_kernel(x, idx)


def _sc_scatter_rows(y, idx, start, n_out, out_ref=None):
    n = y.shape[0]
    w0 = start // SC_WINDOW
    out_type = () if out_ref is not None else jax.ShapeDtypeStruct((n_out, y.shape[1]), y.dtype)

    @functools.partial(pl.kernel, out_type=out_type, mesh=_sc_mesh())
    def scatter_kernel(y_hbm, i_hbm, o_hbm):
        def body(y_vmem, i_vmem):
            pltpu.sync_copy(y_vmem, o_hbm.at[i_vmem.at[0]])

        pltpu.emit_pipeline(
            body,
            grid=(n // SC_WINDOW,),
            in_specs=[pl.BlockSpec((SC_WINDOW, y.shape[1]), lambda i: (i, 0)),
                      pl.BlockSpec((1, SC_WINDOW), lambda i: (0, w0 + i))],
            out_specs=[],
            core_axis_name=("core", "subcore"),
            dimension_semantics=(pltpu.PARALLEL,),
        )(y_hbm, i_hbm)

    if out_ref is None:
        return scatter_kernel(y, idx)
    scatter_kernel(y, idx, out_ref)
    return out_ref


def _expert_kernel(first_block, be_ref, nu_ref, x_ref, wgu_ref, bgu_ref, wd_ref, bd_ref, y_ref, wgu_bf, wd_bf):
    step = pl.program_id(0)
    blk = first_block + step

    @pl.when(blk < nu_ref[0])
    def _():
        @pl.when((step == 0) | (be_ref[blk] != be_ref[jnp.maximum(blk - 1, 0)]))
        def _():
            wgu_bf[...] = wgu_ref[0].astype(BF16)
            wd_bf[...] = wd_ref[0].astype(BF16)

        xb = _unpack_rows([x_ref[j] for j in range(ROW_PARTS)]).astype(BF16)
        gu = jnp.dot(xb, wgu_bf[...], preferred_element_type=F32) + bgu_ref[0]
        glu = jnp.minimum(gu[:, :D_FF], SWIGLU_LIMIT)
        lin = jnp.clip(gu[:, D_FF:], -SWIGLU_LIMIT, SWIGLU_LIMIT)
        act = glu * jax.nn.sigmoid(SWIGLU_ALPHA * glu) * (lin + 1.0)
        y = jnp.dot(act.astype(BF16), wd_bf[...], preferred_element_type=F32) + bd_ref[0]
        for j, piece in enumerate(_pack_rows(y)):
            y_ref[j] = piece

    @pl.when(blk >= nu_ref[0])
    def _():
        y_ref[...] = jnp.zeros_like(y_ref)


def _experts(h2, block_expert, n_used, row_tok, row_dst, w_gu, b_gu, w_down, b_down):
    R = MOE_ROWS
    T = h2.shape[1]
    n_rows = row_tok.shape[0]
    n_blocks = n_rows // R
    bounds = sorted({n_blocks * f // MOE_GROUP_BOUNDS[-1] for f in MOE_GROUP_BOUNDS})
    groups = list(zip(bounds[:-1], bounds[1:]))
    part = jnp.arange(ROW_PARTS, dtype=jnp.int32)[:, None]
    h2_flat = h2.reshape(ROW_PARTS * T, PART_W)

    def group_major(per_row, stride):
        return jnp.concatenate([(part * stride + per_row[None, b0 * R:b1 * R]).reshape(-1) for b0, b1 in groups])[None]

    src_idx = group_major(row_tok, T)
    dst_idx = group_major(row_dst, n_rows)
    xs = [_sc_gather_rows(h2_flat, src_idx, ROW_PARTS * b0 * R, ROW_PARTS * (b1 - b0) * R)
          .reshape(ROW_PARTS, (b1 - b0) * R, PART_W) for b0, b1 in groups]

    def run_group(grp):
        first, g_blocks = groups[grp][0], groups[grp][1] - groups[grp][0]
        wspec = lambda shape: pl.BlockSpec(shape, lambda i, be, nu: (be[first + i], 0, 0))
        rows_spec = pl.BlockSpec((ROW_PARTS, R, PART_W), lambda i, be, nu: (0, i, 0))
        return pl.pallas_call(
            functools.partial(_expert_kernel, first),
            grid_spec=pltpu.PrefetchScalarGridSpec(
                num_scalar_prefetch=2,
                grid=(g_blocks,),
                in_specs=[rows_spec, wspec((1, D_MODEL, 2 * D_FF)), wspec((1, 1, 2 * D_FF)),
                          wspec((1, D_FF, D_MODEL)), wspec((1, 1, D_MODEL))],
                out_specs=rows_spec,
                scratch_shapes=[pltpu.VMEM((D_MODEL, 2 * D_FF), BF16), pltpu.VMEM((D_FF, D_MODEL), BF16)],
            ),
            out_shape=jax.ShapeDtypeStruct((ROW_PARTS, g_blocks * R, PART_W), jnp.uint32),
            compiler_params=pltpu.CompilerParams(dimension_semantics=("arbitrary",), vmem_limit_bytes=VMEM_LIMIT),
            name=f"experts_g{grp}",
        )(block_expert, n_used, xs[grp], w_gu, b_gu, w_down, b_down)

    out = None
    for grp, (b0, b1) in enumerate(groups):
        ys = run_group(grp)
        scattered = _sc_scatter_rows(ys.reshape(ROW_PARTS * (b1 - b0) * R, PART_W), dst_idx, ROW_PARTS * b0 * R,
                                     ROW_PARTS * n_rows, out)
        out = jax.new_ref(scattered) if out is None else scattered
    return jax.freeze(out).reshape(ROW_PARTS, n_rows, PART_W)


def _combine_kernel(x1_ref, gate_ref, g_ref, *refs):
    o_refs, out_ref = refs[:TOP_K], refs[TOP_K]
    y = x1_ref[...]
    gt = gate_ref[...]
    for kk in range(TOP_K):
        y = y + gt[:, kk:kk + 1] * _unpack_rows([o_refs[kk][j] for j in range(ROW_PARTS)])
    out_ref[...] = _rms(y, g_ref[...])


def _combine(x1, o4, gates, g):
    T = x1.shape[0]
    tm = TM_COMBINE
    nt = T // tm
    slot_spec = lambda kk: pl.BlockSpec((ROW_PARTS, tm, PART_W), lambda i: (0, kk * nt + i, 0))
    return pl.pallas_call(
        _combine_kernel,
        grid=(nt,),
        in_specs=[
            pl.BlockSpec((tm, D_MODEL), lambda i: (i, 0)),
            pl.BlockSpec((tm, LANES), lambda i: (i, 0)),
            pl.BlockSpec((1, D_MODEL), lambda i: (0, 0)),
        ] + [slot_spec(kk) for kk in range(TOP_K)],
        out_specs=pl.BlockSpec((tm, D_MODEL), lambda i: (i, 0)),
        out_shape=jax.ShapeDtypeStruct((T, D_MODEL), F32),
        compiler_params=pltpu.CompilerParams(dimension_semantics=("parallel",), vmem_limit_bytes=VMEM_LIMIT),
        name="combine_norm",
    )(x1, gates, g, *([o4] * TOP_K))


def _dispatch_plan(idx_t, T):
    R = MOE_ROWS
    P = T * TOP_K
    n_blocks = P // R + N_EXPERTS
    n_rows = n_blocks * R
    key_span = 2 * P
    assert R <= P and N_EXPERTS * key_span < 2 ** 31 - 1
    experts = jnp.arange(N_EXPERTS, dtype=jnp.int32)
    counts = jnp.sum(idx_t[:, :, None] == experts, axis=(0, 1), dtype=jnp.int32)
    padded = (counts + R - 1) // R * R
    pend = jnp.cumsum(padded)
    pair_keys = (idx_t * key_span + jnp.arange(P, dtype=jnp.int32).reshape(TOP_K, T)).reshape(P)
    j = jnp.arange(R, dtype=jnp.int32)[None, :]
    unused = jnp.iinfo(jnp.int32).max
    pad_keys = jnp.where(j < (padded - counts)[:, None], experts[:, None] * key_span + P + j, unused)
    keys = jnp.sort(jnp.concatenate([pair_keys, pad_keys.reshape(-1)]), stable=False)
    val = keys % key_span
    is_valid = (keys != unused) & (val < P)
    row = jnp.arange(n_rows, dtype=jnp.int32)
    row_tok = jnp.where(is_valid, val % T, row % T)
    pad_rank = jnp.cumsum(1 - is_valid.astype(jnp.int32)) - 1
    row_dst = jnp.where(is_valid, val, P + pad_rank)
    block_row0 = jnp.arange(n_blocks, dtype=jnp.int32) * R
    block_expert = jnp.minimum(jnp.sum(pend[None, :] <= block_row0[:, None], axis=1, dtype=jnp.int32), N_EXPERTS - 1)
    n_used = (pend[-1] // R).reshape(1).astype(jnp.int32)
    return block_expert, n_used, row_tok, row_dst


def _layer(x2, B, S, layer, norm_attn_g, w_in, rel_bias_table, lam_vecs, diff_norm_g, conv_w, conv_b, gate_b,
           o_b, ml_norm_g, w_out, norm_ffn_g, router_w, router_b, w_gu, b_gu, w_down, b_down):
    T = B * S
    H = ML_HEADS
    w_main = w_in[:, :N_MAIN].astype(BF16)
    wg = w_in[:, N_MAIN:]
    w_gate = jnp.zeros((D_MODEL, 2 * LANES), F32).at[:, :H].set(wg[:, :H]).at[:, LANES:LANES + H].set(wg[:, H:])
    proj, gates, v_ext = _inproj(x2, norm_attn_g.reshape(1, D_MODEL), w_main, w_gate.astype(BF16))

    lam_init = 0.8 - 0.6 * math.exp(-0.3 * layer)
    bias = _relbias(rel_bias_table.astype(F32))
    a_out = _attention(proj, v_ext, bias, lam_vecs, diff_norm_g.reshape(DA_HEADS, 1, DA_V_DIM), B, S, lam_init)

    gates_t = jnp.concatenate([gates[:, :H], gates[:, LANES:LANES + H]], axis=1).T
    gate_b_pad = jnp.zeros((1, 2 * LANES), F32).at[0, :H].set(gate_b[:H]).at[0, LANES:LANES + H].set(gate_b[H:])
    m_out = _mlstm(proj, gates, gates_t, conv_w, conv_b.reshape(1, -1), gate_b_pad, gate_b.reshape(2 * H, 1),
                   o_b.reshape(1, -1), ml_norm_g.reshape(1, -1), B, S)

    rw = jnp.zeros((D_MODEL, LANES), F32).at[:, :N_EXPERTS].set(router_w).astype(BF16)
    rb = jnp.full((1, LANES), MASK_NEG, F32).at[0, :N_EXPERTS].set(router_b)
    x1, h2, top_idx, top_gate = _outproj(a_out, m_out, x2, w_out.astype(BF16), norm_ffn_g.reshape(1, D_MODEL), rw, rb)

    block_expert, n_used, row_tok, row_dst = _dispatch_plan(top_idx[:TOP_K], T)
    o4 = _experts(h2, block_expert, n_used, row_tok, row_dst, w_gu, b_gu.reshape(N_EXPERTS, 1, 2 * D_FF),
                  w_down, b_down.reshape(N_EXPERTS, 1, D_MODEL))
    return x1, o4, top_gate


def kernel(x, norm_attn_g, w_in, rel_bias_table, lambda_q1, lambda_k1, lambda_q2, lambda_k2, diff_norm_g,
           mlstm_conv_w, mlstm_conv_b, mlstm_gate_b, mlstm_o_b, mlstm_norm_g, w_out, norm_ffn_g, router_w,
           router_b, expert_w_gu, expert_b_gu, expert_w_down, expert_b_down, final_norm_g):
    B, S, _ = x.shape
    depth = w_in.shape[0]
    assert depth == 1, "the combine kernel fuses the last layer's MoE residual with the final norm"
    assert S % TQ == 0 and S % ML_CHUNK == 0 and (B * S) % TM_PROJ == 0 and (B * S * TOP_K) % MOE_ROWS == 0
    x2 = x.reshape(B * S, D_MODEL)
    l = 0
    lam_vecs = jnp.stack([lambda_q1[l], lambda_k1[l], lambda_q2[l], lambda_k2[l]])
    x1, o4, top_gate = _layer(
        x2, B, S, l, norm_attn_g[l], w_in[l], rel_bias_table, lam_vecs, diff_norm_g[l], mlstm_conv_w[l],
        mlstm_conv_b[l], mlstm_gate_b[l], mlstm_o_b[l], mlstm_norm_g[l], w_out[l], norm_ffn_g[l], router_w[l],
        router_b[l], expert_w_gu[l], expert_b_gu[l], expert_w_down[l], expert_b_down[l])
    out = _combine(x1, o4, top_gate, final_norm_g.reshape(1, D_MODEL))
    return out.reshape(B, S, D_MODEL)
```

```python
import functools
import math

import numpy as np
import jax
import jax.numpy as jnp
from jax import lax
from jax.experimental import pallas as pl
from jax.experimental.pallas import tpu as pltpu
from jax.experimental.pallas import tpu_sc as plsc

F32 = jnp.float32
BF16 = jnp.bfloat16

D_MODEL = 1024
NORM_EPS = 1e-5
DA_HEADS = 4
DA_QK_DIM = 64
DA_V_DIM = 128
DA_WIDTH = DA_HEADS * DA_V_DIM
REL_BUCKETS = 32
REL_MAX_DIST = 128
ML_HEADS = 4
ML_QK_DIM = 64
ML_V_DIM = 128
ML_WIDTH = ML_HEADS * ML_V_DIM
ML_QK_COLS = 2 * ML_HEADS * ML_QK_DIM
ML_CONV = 4
N_EXPERTS = 32
TOP_K = 4
D_FF = D_MODEL
SWIGLU_LIMIT = 7.0
SWIGLU_ALPHA = 1.702

N_MAIN = 3 * DA_WIDTH + ML_QK_COLS + 2 * ML_WIDTH
LANES = 128
SUBLANES = 8
MASK_NEG = -1e30

TM_PROJ = 512
ROUTE_ROWS = 256
TQ = 256
ATT_HEADS = 4
FAR_BLOCKS = 4
ML_CHUNK = 256
MOE_ROWS = 512
TM_COMBINE = 512
ROW_PARTS = 2
PART_W = D_MODEL // (2 * ROW_PARTS)
SC_WINDOW = 128
MOE_GROUP_BOUNDS = (0, 1, 3, 7, 12, 17, 22, 27, 30, 32)
VMEM_LIMIT = 56 * 1024 * 1024


def _bucket_lower_bounds():
    n = np.arange(0, 4 * REL_MAX_DIST)
    max_exact = REL_BUCKETS // 2
    nf = np.maximum(n, 1).astype(np.float32)
    large = max_exact + (np.log(nf / np.float32(max_exact)) / np.float32(math.log(REL_MAX_DIST / max_exact))
                         * np.float32(REL_BUCKETS - max_exact)).astype(np.int32)
    large = np.minimum(large, REL_BUCKETS - 1)
    bucket = np.where(n < max_exact, n, large)
    return [int(np.argmax(bucket >= b)) for b in range(REL_BUCKETS)]


BUCKET_LO = _bucket_lower_bounds()
assert BUCKET_LO[-1] <= TQ, "keys two blocks away must all sit in the last bucket"


def _rms(x, g):
    return x * lax.rsqrt(jnp.mean(x * x, axis=-1, keepdims=True) + NORM_EPS) * g


def _pack_rows(v):
    bits = lax.bitcast_convert_type(v.astype(BF16).astype(F32), jnp.uint32)
    return [bits[:, (2 * j + 1) * PART_W:(2 * j + 2) * PART_W] | (bits[:, 2 * j * PART_W:(2 * j + 1) * PART_W] >> 16)
            for j in range(ROW_PARTS)]


def _unpack_rows(pieces):
    cols = []
    for w in pieces:
        cols.append(lax.bitcast_convert_type(w << 16, F32))
        cols.append(lax.bitcast_convert_type(w & jnp.uint32(0xFFFF0000), F32))
    return jnp.concatenate(cols, axis=1)


def _log_sigmoid(x):
    return jnp.minimum(x, 0.0) - jnp.log1p(jnp.exp(-jnp.abs(x)))


def _inproj_kernel(x_ref, g_ref, w_ref, wg_ref, o_ref, og_ref, vx_ref):
    hb = _rms(x_ref[...], g_ref[...]).astype(BF16)
    for n in range(N_MAIN // DA_WIDTH):
        cols = slice(n * DA_WIDTH, (n + 1) * DA_WIDTH)
        chunk = jnp.dot(hb, w_ref[:, cols], preferred_element_type=F32).astype(BF16)
        o_ref[:, cols] = chunk
        if n == 2:
            ones = jnp.ones((chunk.shape[0], LANES), BF16)
            for h in range(DA_HEADS):
                vx_ref[:, 2 * h * LANES:(2 * h + 1) * LANES] = chunk[:, h * DA_V_DIM:(h + 1) * DA_V_DIM]
                vx_ref[:, (2 * h + 1) * LANES:(2 * h + 2) * LANES] = ones
    og_ref[...] = jnp.dot(hb, wg_ref[...], preferred_element_type=F32)


def _inproj(x2, g, w_main, w_gate):
    T = x2.shape[0]
    return pl.pallas_call(
        _inproj_kernel,
        grid=(T // TM_PROJ,),
        in_specs=[
            pl.BlockSpec((TM_PROJ, D_MODEL), lambda i: (i, 0)),
            pl.BlockSpec((1, D_MODEL), lambda i: (0, 0)),
            pl.BlockSpec((D_MODEL, N_MAIN), lambda i: (0, 0)),
            pl.BlockSpec((D_MODEL, 2 * LANES), lambda i: (0, 0)),
        ],
        out_specs=[
            pl.BlockSpec((TM_PROJ, N_MAIN), lambda i: (i, 0)),
            pl.BlockSpec((TM_PROJ, 2 * LANES), lambda i: (i, 0)),
            pl.BlockSpec((TM_PROJ, 2 * DA_WIDTH), lambda i: (i, 0)),
        ],
        out_shape=[jax.ShapeDtypeStruct((T, N_MAIN), BF16), jax.ShapeDtypeStruct((T, 2 * LANES), F32),
                   jax.ShapeDtypeStruct((T, 2 * DA_WIDTH), BF16)],
        compiler_params=pltpu.CompilerParams(dimension_semantics=("parallel",), vmem_limit_bytes=VMEM_LIMIT),
        name="inproj",
    )(x2, g, w_main, w_gate)


def _relbias_kernel(tbl_ref, o_ref):
    h = pl.program_id(0)
    r = lax.broadcasted_iota(jnp.int32, (2 * TQ, 2 * TQ), 0) & (TQ - 1)
    c = lax.broadcasted_iota(jnp.int32, (2 * TQ, 2 * TQ), 1)
    rel = r + TQ - c
    val = jnp.full((2 * TQ, 2 * TQ), tbl_ref[0, h], F32)
    for b in range(1, REL_BUCKETS):
        val = jnp.where(rel >= BUCKET_LO[b], tbl_ref[b, h], val)
    o_ref[0] = jnp.where(rel >= 0, val - tbl_ref[REL_BUCKETS - 1, h], MASK_NEG)


def _relbias(table):
    return pl.pallas_call(
        _relbias_kernel,
        grid=(DA_HEADS,),
        in_specs=[pl.BlockSpec(memory_space=pltpu.SMEM)],
        out_specs=pl.BlockSpec((1, 2 * TQ, 2 * TQ), lambda h: (h, 0, 0)),
        out_shape=jax.ShapeDtypeStruct((DA_HEADS, 2 * TQ, 2 * TQ), F32),
        compiler_params=pltpu.CompilerParams(dimension_semantics=("parallel",)),
        name="relbias",
    )(table)


def _attn_kernel(lam_init, lam_ref, q_ref, k_ref, v_ref, bias_ref, g_ref, o_ref, acc_ref, m_ref):
    i = pl.program_id(2)
    lane = lax.broadcasted_iota(jnp.int32, (TQ, LANES), 1)
    qqs = []
    for hh in range(ATT_HEADS):
        q = q_ref[:, hh * LANES:(hh + 1) * LANES] * jnp.asarray(DA_QK_DIM ** -0.5, BF16)
        zero = jnp.zeros_like(q)
        qqs.append(jnp.concatenate([jnp.where(lane < DA_QK_DIM, q, zero), jnp.where(lane >= DA_QK_DIM, q, zero)],
                                   axis=0))

    def rows(j, nblk):
        start = j * TQ if isinstance(j, int) else pl.multiple_of(j * TQ, TQ)
        return pl.ds(start, nblk * TQ)

    def update(j, nblk, bias_cols):
        for hh in range(ATT_HEADS):
            hcols = slice(hh * LANES, (hh + 1) * LANES)
            s = lax.dot_general(qqs[hh], k_ref[rows(j, nblk), hcols], (((1,), (1,)), ((), ())),
                                preferred_element_type=F32)
            if bias_cols is not None:
                s = s + bias_ref[hh, :, bias_cols]
            tiles = [s[:, t * LANES:(t + 1) * LANES] for t in range(nblk * TQ // LANES)]
            m_old = m_ref[hh]
            m_new = jnp.maximum(m_old, jnp.max(functools.reduce(jnp.maximum, tiles), axis=-1, keepdims=True))
            alpha = jnp.exp(m_old - m_new)
            p = jnp.concatenate([jnp.exp(t - m_new) for t in tiles], axis=1).astype(BF16)
            pv = jnp.dot(p, v_ref[rows(j, nblk), 2 * hh * LANES:(2 * hh + 2) * LANES], preferred_element_type=F32)
            for half in (slice(0, LANES), slice(LANES, 2 * LANES)):
                acc_ref[hh, :, half] = alpha * acc_ref[hh, :, half] + pv[:, half]
            m_ref[hh] = m_new

    m_ref[...] = jnp.full_like(m_ref, MASK_NEG)
    acc_ref[...] = jnp.zeros_like(acc_ref)

    @pl.when(i == 0)
    def _():
        update(0, 1, slice(TQ, 2 * TQ))

    @pl.when(i >= 1)
    def _():
        update(i - 1, 2, slice(0, 2 * TQ))

    n_far = jnp.maximum(i - 1, 0)
    rem = n_far % FAR_BLOCKS

    @pl.when(rem % 2 == 1)
    def _():
        update(0, 1, None)

    @pl.when(rem >= 2)
    def _():
        update(rem % 2, 2, None)

    def far_step(jj, carry):
        update(rem + FAR_BLOCKS * jj, FAR_BLOCKS, None)
        return carry

    lax.fori_loop(0, n_far // FAR_BLOCKS, far_step, 0)

    lam_v = lam_ref[...]
    lam = (jnp.exp(jnp.sum(lam_v[0:1] * lam_v[1:2], axis=-1, keepdims=True))
           - jnp.exp(jnp.sum(lam_v[2:3] * lam_v[3:4], axis=-1, keepdims=True)) + lam_init)
    for hh in range(ATT_HEADS):
        o = acc_ref[hh, :, :LANES] / acc_ref[hh, :, LANES:]
        out = o[:TQ] - lam * o[TQ:]
        o_ref[:, hh * LANES:(hh + 1) * LANES] = (_rms(out, g_ref[hh]) * (1.0 - lam_init)).astype(BF16)


def _attention(proj, v_ext, bias, lam_vecs, norm_g, B, S, lam_init):
    nq = S // TQ
    hw = ATT_HEADS * LANES
    q_off, k_off = 0, DA_WIDTH // hw
    return pl.pallas_call(
        functools.partial(_attn_kernel, lam_init),
        grid=(B, DA_HEADS // ATT_HEADS, nq),
        in_specs=[
            pl.BlockSpec((4, DA_QK_DIM), lambda b, h, i: (0, 0)),
            pl.BlockSpec((TQ, hw), lambda b, h, i: (b * nq + i, q_off + h)),
            pl.BlockSpec((S, hw), lambda b, h, i: (b, k_off + h)),
            pl.BlockSpec((S, 2 * hw), lambda b, h, i: (b, h)),
            pl.BlockSpec((ATT_HEADS, 2 * TQ, 2 * TQ), lambda b, h, i: (h, 0, 0)),
            pl.BlockSpec((ATT_HEADS, 1, LANES), lambda b, h, i: (h, 0, 0)),
        ],
        out_specs=pl.BlockSpec((TQ, hw), lambda b, h, i: (b * nq + i, h)),
        out_shape=jax.ShapeDtypeStruct((B * S, DA_WIDTH), BF16),
        scratch_shapes=[
            pltpu.VMEM((ATT_HEADS, 2 * TQ, 2 * DA_V_DIM), F32),
            pltpu.VMEM((ATT_HEADS, 2 * TQ, LANES), F32),
        ],
        compiler_params=pltpu.CompilerParams(
            dimension_semantics=("parallel", "parallel", "arbitrary"), vmem_limit_bytes=VMEM_LIMIT),
        name="diff_attention",
    )(lam_vecs, proj, proj, v_ext, bias, norm_g)


def _prefix_scan(x, axis, op, identity):
    n = x.shape[axis]
    idx = lax.broadcasted_iota(jnp.int32, x.shape, axis)
    d = 1
    while d < n:
        x = op(x, jnp.where(idx >= d, pltpu.roll(x, d, axis), identity))
        d *= 2
    return x


def _prefix_sum(x, axis):
    return _prefix_scan(x, axis, jnp.add, 0.0)


def _prefix_max(x, axis):
    return _prefix_scan(x, axis, jnp.maximum, -jnp.inf)


def _mlstm_kernel(qk_ref, v_ref, og_ref, g_ref, gt_ref, cw_ref, cb_ref, gb_ref, gbt_ref, ob_ref, ng_ref,
                  out_ref, prev_ref, cn_ref, m_ref):
    L = ML_CHUNK
    H = ML_HEADS
    QW = H * ML_QK_DIM

    @pl.when(pl.program_id(1) == 0)
    def _():
        prev_ref[...] = jnp.zeros_like(prev_ref)
        cn_ref[...] = jnp.zeros_like(cn_ref)
        m_ref[...] = jnp.zeros_like(m_ref)

    x = qk_ref[...].astype(F32)
    prev = prev_ref[...]
    row = lax.broadcasted_iota(jnp.int32, x.shape, 0)
    y = cb_ref[...] + cw_ref[ML_CONV - 1:ML_CONV, :] * x
    for d in range(1, ML_CONV):
        shifted = pltpu.roll(jnp.where(row >= L - d, prev, x), d, 0)
        y = y + cw_ref[ML_CONV - 1 - d:ML_CONV - d, :] * shifted
    prev_ref[...] = x
    qkc = y * jax.nn.sigmoid(y)
    q = qkc[:, :QW]
    k = qkc[:, QW:] * (ML_QK_DIM ** -0.5)
    kb = k.astype(BF16)

    G = g_ref[...]
    li_all = G[:, :LANES] + gb_ref[:, :LANES]
    b_all = _prefix_sum(_log_sigmoid(G[:, LANES:] + gb_ref[:, LANES:]), 0)
    GT = gt_ref[...] + gbt_ref[...]
    bT = _prefix_sum(_log_sigmoid(GT), 1)
    m_intra_all = b_all + _prefix_max(li_all - b_all, 0)

    qlane = lax.broadcasted_iota(jnp.int32, (L, QW), 1) // ML_QK_DIM
    qz = jnp.concatenate([jnp.where(qlane == h, q, 0.0) for h in range(H)], axis=0).astype(BF16)
    s_all = lax.dot_general(qz, kb, (((1,), (1,)), ((), ())), preferred_element_type=F32)
    cn_old = cn_ref[...]
    qcn = jnp.dot(qz, cn_old.astype(BF16), preferred_element_type=F32)

    t_idx = lax.broadcasted_iota(jnp.int32, (L, L), 0)
    j_idx = lax.broadcasted_iota(jnp.int32, (L, L), 1)
    causal = j_idx <= t_idx
    m_old_row = m_ref[...]
    mlane = lax.broadcasted_iota(jnp.int32, (1, LANES), 1)
    crow = lax.broadcasted_iota(jnp.int32, (QW, 1), 0) // ML_QK_DIM
    m_new_row = m_old_row
    wk_exp = jnp.zeros((L, QW), F32)
    decay_col = jnp.zeros((QW, 1), F32)

    for h in range(H):
        li_c = li_all[:, h:h + 1]
        b_c = b_all[:, h:h + 1]
        li_r = GT[h:h + 1, :]
        b_r = bT[H + h:H + h + 1, :]
        m_prev = m_old_row[:, h:h + 1]

        inter = b_c + m_prev
        m_t = jnp.maximum(m_intra_all[:, h:h + 1], inter)
        w_intra = jnp.exp(jnp.where(causal, (b_c - m_t) + (li_r - b_r), MASK_NEG))
        w_inter = jnp.exp(inter - m_t)
        s = (s_all[h * L:(h + 1) * L, :] * w_intra).astype(BF16)
        vh1 = jnp.concatenate([v_ref[:, h * ML_V_DIM:(h + 1) * ML_V_DIM], jnp.ones((L, LANES), BF16)], axis=1)
        nd = jnp.dot(s, vh1, preferred_element_type=F32) + w_inter * qcn[h * L:(h + 1) * L, :]
        hval = nd[:, :ML_V_DIM] / jnp.maximum(jnp.abs(nd[:, ML_V_DIM:]), jnp.exp(-m_t))

        cols = slice(h * ML_V_DIM, (h + 1) * ML_V_DIM)
        gate = jax.nn.sigmoid(og_ref[:, cols].astype(F32) + ob_ref[:, cols])
        out_ref[:, cols] = (_rms(hval, ng_ref[:, cols]) * gate).astype(BF16)

        g_tot = b_c[L - 1:L, :]
        wlog = g_tot - b_c + li_c
        m_new = jnp.maximum(g_tot + m_prev, jnp.max(wlog, axis=0, keepdims=True))
        decay = jnp.exp(g_tot + m_prev - m_new)
        wk = jnp.exp(wlog - m_new)
        m_new_row = jnp.where(mlane == h, m_new, m_new_row)
        wk_exp = jnp.where(qlane == h, wk, wk_exp)
        decay_col = jnp.where(crow == h, decay, decay_col)

    kw = (k * wk_exp).astype(BF16)
    vext = jnp.concatenate([v_ref[...], jnp.ones((L, LANES), BF16)], axis=1)
    upd = lax.dot_general(kw, vext, (((0,), (0,)), ((), ())), preferred_element_type=F32)
    for h in range(H):
        rows = slice(h * ML_QK_DIM, (h + 1) * ML_QK_DIM)
        dh = decay_col[rows, :]
        cn_ref[rows, :ML_V_DIM] = dh * cn_old[rows, :ML_V_DIM] + upd[rows, h * ML_V_DIM:(h + 1) * ML_V_DIM]
        cn_ref[rows, ML_V_DIM:] = dh * cn_old[rows, ML_V_DIM:] + upd[rows, H * ML_V_DIM:]
    m_ref[...] = m_new_row


def _mlstm(proj, gates, gates_t, conv_w, conv_b, gate_b, gate_bt, o_b, norm_g, B, S):
    L = ML_CHUNK
    nc = S // L
    qk_blk = 3 * DA_WIDTH // ML_QK_COLS
    v_blk, o_blk = qk_blk + 1, qk_blk + 2
    full = lambda shape: pl.BlockSpec(shape, lambda b, c: (0,) * len(shape))
    return pl.pallas_call(
        _mlstm_kernel,
        grid=(B, nc),
        in_specs=[
            pl.BlockSpec((L, ML_QK_COLS), lambda b, c: (b * nc + c, qk_blk)),
            pl.BlockSpec((L, ML_WIDTH), lambda b, c: (b * nc + c, v_blk)),
            pl.BlockSpec((L, ML_WIDTH), lambda b, c: (b * nc + c, o_blk)),
            pl.BlockSpec((L, 2 * LANES), lambda b, c: (b * nc + c, 0)),
            pl.BlockSpec((2 * ML_HEADS, L), lambda b, c: (0, b * nc + c)),
            full((ML_CONV, ML_QK_COLS)),
            full((1, ML_QK_COLS)),
            full((1, 2 * LANES)),
            full((2 * ML_HEADS, 1)),
            full((1, ML_WIDTH)),
            full((1, ML_WIDTH)),
        ],
        out_specs=pl.BlockSpec((L, ML_WIDTH), lambda b, c: (b * nc + c, 0)),
        out_shape=jax.ShapeDtypeStruct((B * S, ML_WIDTH), BF16),
        scratch_shapes=[
            pltpu.VMEM((L, ML_QK_COLS), F32),
            pltpu.VMEM((ML_HEADS * ML_QK_DIM, 2 * ML_V_DIM), F32),
            pltpu.VMEM((1, LANES), F32),
        ],
        compiler_params=pltpu.CompilerParams(
            dimension_semantics=("parallel", "arbitrary"), vmem_limit_bytes=VMEM_LIMIT),
        name="mlstm",
    )(proj, proj, proj, gates, gates_t, conv_w, conv_b, gate_b, gate_bt, o_b, norm_g)


def _outproj_kernel(a_ref, m_ref, x_ref, wo_ref, g_ref, rw_ref, rb_ref, x1_ref, h2_ref, idx_ref, gate_ref):
    x1 = (x_ref[...]
          + jnp.dot(a_ref[...], wo_ref[:DA_WIDTH, :], preferred_element_type=F32)
          + jnp.dot(m_ref[...], wo_ref[DA_WIDTH:, :], preferred_element_type=F32))
    x1_ref[...] = x1
    for c in range(x1.shape[0] // ROUTE_ROWS):
        r = slice(c * ROUTE_ROWS, (c + 1) * ROUTE_ROWS)
        h2 = _rms(x1[r], g_ref[...])
        for j, piece in enumerate(_pack_rows(h2)):
            h2_ref[j, r, :] = piece
        work = jnp.dot(h2.astype(BF16), rw_ref[...], preferred_element_type=F32) + rb_ref[...]
        lane = lax.broadcasted_iota(jnp.int32, work.shape, 1).astype(F32)
        idxs = jnp.zeros_like(work)
        vals = jnp.zeros_like(work)
        top = None
        for kk in range(TOP_K):
            mx = jnp.max(work, axis=-1, keepdims=True)
            am = jnp.min(jnp.where(work == mx, lane, float(LANES)), axis=-1, keepdims=True)
            idxs = jnp.where(lane == kk, am, idxs)
            vals = jnp.where(lane == kk, mx, vals)
            work = jnp.where(lane == am, -jnp.inf, work)
            if kk == 0:
                top = mx
        e = jnp.where(lane < TOP_K, jnp.exp(vals - top), 0.0)
        gate_ref[r, :] = e / jnp.sum(e, axis=-1, keepdims=True)
        idx_ref[:, r] = jnp.transpose(idxs)[:SUBLANES].astype(jnp.int32)


def _outproj(a_out, m_out, x2, w_out, g, router_w, router_b):
    T = x2.shape[0]
    row = lambda w: pl.BlockSpec((TM_PROJ, w), lambda i: (i, 0))
    full = lambda r, c: pl.BlockSpec((r, c), lambda i: (0, 0))
    return pl.pallas_call(
        _outproj_kernel,
        grid=(T // TM_PROJ,),
        in_specs=[row(DA_WIDTH), row(ML_WIDTH), row(D_MODEL), full(D_MODEL, D_MODEL), full(1, D_MODEL),
                  full(D_MODEL, LANES), full(1, LANES)],
        out_specs=[row(D_MODEL), pl.BlockSpec((ROW_PARTS, TM_PROJ, PART_W), lambda i: (0, i, 0)),
                   pl.BlockSpec((SUBLANES, TM_PROJ), lambda i: (0, i)), row(LANES)],
        out_shape=[jax.ShapeDtypeStruct((T, D_MODEL), F32), jax.ShapeDtypeStruct((ROW_PARTS, T, PART_W), jnp.uint32),
                   jax.ShapeDtypeStruct((SUBLANES, T), jnp.int32), jax.ShapeDtypeStruct((T, LANES), F32)],
        compiler_params=pltpu.CompilerParams(dimension_semantics=("parallel",), vmem_limit_bytes=VMEM_LIMIT),
        name="outproj_router",
    )(a_out, m_out, x2, w_out, g, router_w, router_b)


def _sc_mesh():
    return plsc.VectorSubcoreMesh(core_axis_name="core", subcore_axis_name="subcore")


def _sc_gather_rows(x, idx, start, n):
    w0 = start // SC_WINDOW

    @functools.partial(pl.kernel, out_type=jax.ShapeDtypeStruct((n, x.shape[1]), x.dtype), mesh=_sc_mesh())
    def gather_kernel(x_hbm, i_hbm, o_hbm):
        def body(i_vmem, o_vmem):
            pltpu.sync_copy(x_hbm.at[i_vmem.at[0]], o_vmem)

        pltpu.emit_pipeline(
            body,
            grid=(n // SC_WINDOW,),
            in_specs=[pl.BlockSpec((1, SC_WINDOW), lambda i: (0, w0 + i))],
            out_specs=[pl.BlockSpec((SC_WINDOW, x.shape[1]), lambda i: (i, 0))],
            core_axis_name=("core", "subcore"),
            dimension_semantics=(pltpu.PARALLEL,),
        )(i_hbm, o_hbm)

    return gather_kernel(x, idx)


def _sc_scatter_rows(y, idx, start, n_out, out_ref=None):
    n = y.shape[0]
    w0 = start // SC_WINDOW
    out_type = () if out_ref is not None else jax.ShapeDtypeStruct((n_out, y.shape[1]), y.dtype)

    @functools.partial(pl.kernel, out_type=out_type, mesh=_sc_mesh())
    def scatter_kernel(y_hbm, i_hbm, o_hbm):
        def body(y_vmem, i_vmem):
            pltpu.sync_copy(y_vmem, o_hbm.at[i_vmem.at[0]])

        pltpu.emit_pipeline(
            body,
            grid=(n // SC_WINDOW,),
            in_specs=[pl.BlockSpec((SC_WINDOW, y.shape[1]), lambda i: (i, 0)),
                      pl.BlockSpec((1, SC_WINDOW), lambda i: (0, w0 + i))],
            out_specs=[],
            core_axis_name=("core", "subcore"),
            dimension_semantics=(pltpu.PARALLEL,),
        )(y_hbm, i_hbm)

    if out_ref is None:
        return scatter_kernel(y, idx)
    scatter_kernel(y, idx, out_ref)
    return out_ref


def _expert_kernel(first_block, be_ref, nu_ref, x_ref, wgu_ref, bgu_ref, wd_ref, bd_ref, y_ref, wgu_bf, wd_bf):
    step = pl.program_id(0)
    blk = first_block + step

    @pl.when(blk < nu_ref[0])
    def _():
        @pl.when((step == 0) | (be_ref[blk] != be_ref[jnp.maximum(blk - 1, 0)]))
        def _():
            wgu_bf[...] = wgu_ref[0].astype(BF16)
            wd_bf[...] = wd_ref[0].astype(BF16)

        xb = _unpack_rows([x_ref[j] for j in range(ROW_PARTS)]).astype(BF16)
        gu = jnp.dot(xb, wgu_bf[...], preferred_element_type=F32) + bgu_ref[0]
        glu = jnp.minimum(gu[:, :D_FF], SWIGLU_LIMIT)
        lin = jnp.clip(gu[:, D_FF:], -SWIGLU_LIMIT, SWIGLU_LIMIT)
        act = glu * jax.nn.sigmoid(SWIGLU_ALPHA * glu) * (lin + 1.0)
        y = jnp.dot(act.astype(BF16), wd_bf[...], preferred_element_type=F32) + bd_ref[0]
        for j, piece in enumerate(_pack_rows(y)):
            y_ref[j] = piece

    @pl.when(blk >= nu_ref[0])
    def _():
        y_ref[...] = jnp.zeros_like(y_ref)


def _experts(h2, block_expert, n_used, row_tok, row_dst, w_gu, b_gu, w_down, b_down):
    R = MOE_ROWS
    T = h2.shape[1]
    n_rows = row_tok.shape[0]
    n_blocks = n_rows // R
    bounds = sorted({n_blocks * f // MOE_GROUP_BOUNDS[-1] for f in MOE_GROUP_BOUNDS})
    groups = list(zip(bounds[:-1], bounds[1:]))
    part = jnp.arange(ROW_PARTS, dtype=jnp.int32)[:, None]
    h2_flat = h2.reshape(ROW_PARTS * T, PART_W)

    def group_major(per_row, stride):
        return jnp.concatenate([(part * stride + per_row[None, b0 * R:b1 * R]).reshape(-1) for b0, b1 in groups])[None]

    src_idx = group_major(row_tok, T)
    dst_idx = group_major(row_dst, n_rows)
    xs = [_sc_gather_rows(h2_flat, src_idx, ROW_PARTS * b0 * R, ROW_PARTS * (b1 - b0) * R)
          .reshape(ROW_PARTS, (b1 - b0) * R, PART_W) for b0, b1 in groups]

    def run_group(grp):
        first, g_blocks = groups[grp][0], groups[grp][1] - groups[grp][0]
        wspec = lambda shape: pl.BlockSpec(shape, lambda i, be, nu: (be[first + i], 0, 0))
        rows_spec = pl.BlockSpec((ROW_PARTS, R, PART_W), lambda i, be, nu: (0, i, 0))
        return pl.pallas_call(
            functools.partial(_expert_kernel, first),
            grid_spec=pltpu.PrefetchScalarGridSpec(
                num_scalar_prefetch=2,
                grid=(g_blocks,),
                in_specs=[rows_spec, wspec((1, D_MODEL, 2 * D_FF)), wspec((1, 1, 2 * D_FF)),
                          wspec((1, D_FF, D_MODEL)), wspec((1, 1, D_MODEL))],
                out_specs=rows_spec,
                scratch_shapes=[pltpu.VMEM((D_MODEL, 2 * D_FF), BF16), pltpu.VMEM((D_FF, D_MODEL), BF16)],
            ),
            out_shape=jax.ShapeDtypeStruct((ROW_PARTS, g_blocks * R, PART_W), jnp.uint32),
            compiler_params=pltpu.CompilerParams(dimension_semantics=("arbitrary",), vmem_limit_bytes=VMEM_LIMIT),
            name=f"experts_g{grp}",
        )(block_expert, n_used, xs[grp], w_gu, b_gu, w_down, b_down)

    out = None
    for grp, (b0, b1) in enumerate(groups):
        ys = run_group(grp)
        scattered = _sc_scatter_rows(ys.reshape(ROW_PARTS * (b1 - b0) * R, PART_W), dst_idx, ROW_PARTS * b0 * R,
                                     ROW_PARTS * n_rows, out)
        out = jax.new_ref(scattered) if out is None else scattered
    return jax.freeze(out).reshape(ROW_PARTS, n_rows, PART_W)


def _combine_kernel(x1_ref, gate_ref, g_ref, *refs):
    o_refs, out_ref = refs[:TOP_K], refs[TOP_K]
    y = x1_ref[...]
    gt = gate_ref[...]
    for kk in range(TOP_K):
        y = y + gt[:, kk:kk + 1] * _unpack_rows([o_refs[kk][j] for j in range(ROW_PARTS)])
    out_ref[...] = _rms(y, g_ref[...])


def _combine(x1, o4, gates, g):
    T = x1.shape[0]
    tm = TM_COMBINE
    nt = T // tm
    slot_spec = lambda kk: pl.BlockSpec((ROW_PARTS, tm, PART_W), lambda i: (0, kk * nt + i, 0))
    return pl.pallas_call(
        _combine_kernel,
        grid=(nt,),
        in_specs=[
            pl.BlockSpec((tm, D_MODEL), lambda i: (i, 0)),
            pl.BlockSpec((tm, LANES), lambda i: (i, 0)),
            pl.BlockSpec((1, D_MODEL), lambda i: (0, 0)),
        ] + [slot_spec(kk) for kk in range(TOP_K)],
        out_specs=pl.BlockSpec((tm, D_MODEL), lambda i: (i, 0)),
        out_shape=jax.ShapeDtypeStruct((T, D_MODEL), F32),
        compiler_params=pltpu.CompilerParams(dimension_semantics=("parallel",), vmem_limit_bytes=VMEM_LIMIT),
        name="combine_norm",
    )(x1, gates, g, *([o4] * TOP_K))


def _dispatch_plan(idx_t, T):
    R = MOE_ROWS
    P = T * TOP_K
    n_blocks = P // R + N_EXPERTS
    n_rows = n_blocks * R
    key_span = 2 * P
    assert R <= P and N_EXPERTS * key_span < 2 ** 31 - 1
    experts = jnp.arange(N_EXPERTS, dtype=jnp.int32)
    counts = jnp.sum(idx_t[:, :, None] == experts, axis=(0, 1), dtype=jnp.int32)
    padded = (counts + R - 1) // R * R
    pend = jnp.cumsum(padded)
    pair_keys = (idx_t * key_span + jnp.arange(P, dtype=jnp.int32).reshape(TOP_K, T)).reshape(P)
    j = jnp.arange(R, dtype=jnp.int32)[None, :]
    unused = jnp.iinfo(jnp.int32).max
    pad_keys = jnp.where(j < (padded - counts)[:, None], experts[:, None] * key_span + P + j, unused)
    keys = jnp.sort(jnp.concatenate([pair_keys, pad_keys.reshape(-1)]), stable=False)
    val = keys % key_span
    is_valid = (keys != unused) & (val < P)
    row = jnp.arange(n_rows, dtype=jnp.int32)
    row_tok = jnp.where(is_valid, val % T, row % T)
    pad_rank = jnp.cumsum(1 - is_valid.astype(jnp.int32)) - 1
    row_dst = jnp.where(is_valid, val, P + pad_rank)
    block_row0 = jnp.arange(n_blocks, dtype=jnp.int32) * R
    block_expert = jnp.minimum(jnp.sum(pend[None, :] <= block_row0[:, None], axis=1, dtype=jnp.int32), N_EXPERTS - 1)
    n_used = (pend[-1] // R).reshape(1).astype(jnp.int32)
    return block_expert, n_used, row_tok, row_dst


def _layer(x2, B, S, layer, norm_attn_g, w_in, rel_bias_table, lam_vecs, diff_norm_g, conv_w, conv_b, gate_b,
           o_b, ml_norm_g, w_out, norm_ffn_g, router_w, router_b, w_gu, b_gu, w_down, b_down):
    T = B * S
    H = ML_HEADS
    w_main = w_in[:, :N_MAIN].astype(BF16)
    wg = w_in[:, N_MAIN:]
    w_gate = jnp.zeros((D_MODEL, 2 * LANES), F32).at[:, :H].set(wg[:, :H]).at[:, LANES:LANES + H].set(wg[:, H:])
    proj, gates, v_ext = _inproj(x2, norm_attn_g.reshape(1, D_MODEL), w_main, w_gate.astype(BF16))

    lam_init = 0.8 - 0.6 * math.exp(-0.3 * layer)
    bias = _relbias(rel_bias_table.astype(F32))
    a_out = _attention(proj, v_ext, bias, lam_vecs, diff_norm_g.reshape(DA_HEADS, 1, DA_V_DIM), B, S, lam_init)

    gates_t = jnp.concatenate([gates[:, :H], gates[:, LANES:LANES + H]], axis=1).T
    gate_b_pad = jnp.zeros((1, 2 * LANES), F32).at[0, :H].set(gate_b[:H]).at[0, LANES:LANES + H].set(gate_b[H:])
    m_out = _mlstm(proj, gates, gates_t, conv_w, conv_b.reshape(1, -1), gate_b_pad, gate_b.reshape(2 * H, 1),
                   o_b.reshape(1, -1), ml_norm_g.reshape(1, -1), B, S)

    rw = jnp.zeros((D_MODEL, LANES), F32).at[:, :N_EXPERTS].set(router_w).astype(BF16)
    rb = jnp.full((1, LANES), MASK_NEG, F32).at[0, :N_EXPERTS].set(router_b)
    x1, h2, top_idx, top_gate = _outproj(a_out, m_out, x2, w_out.astype(BF16), norm_ffn_g.reshape(1, D_MODEL), rw, rb)

    block_expert, n_used, row_tok, row_dst = _dispatch_plan(top_idx[:TOP_K], T)
    o4 = _experts(h2, block_expert, n_used, row_tok, row_dst, w_gu, b_gu.reshape(N_EXPERTS, 1, 2 * D_FF),
                  w_down, b_down.reshape(N_EXPERTS, 1, D_MODEL))
    return x1, o4, top_gate


def kernel(x, norm_attn_g, w_in, rel_bias_table, lambda_q1, lambda_k1, lambda_q2, lambda_k2, diff_norm_g,
           mlstm_conv_w, mlstm_conv_b, mlstm_gate_b, mlstm_o_b, mlstm_norm_g, w_out, norm_ffn_g, router_w,
           router_b, expert_w_gu, expert_b_gu, expert_w_down, expert_b_down, final_norm_g):
    B, S, _ = x.shape
    depth = w_in.shape[0]
    assert depth == 1, "the combine kernel fuses the last layer's MoE residual with the final norm"
    assert S % TQ == 0 and S % ML_CHUNK == 0 and (B * S) % TM_PROJ == 0 and (B * S * TOP_K) % MOE_ROWS == 0
    x2 = x.reshape(B * S, D_MODEL)
    l = 0
    lam_vecs = jnp.stack([lambda_q1[l], lambda_k1[l], lambda_q2[l], lambda_k2[l]])
    x1, o4, top_gate = _layer(
        x2, B, S, l, norm_attn_g[l], w_in[l], rel_bias_table, lam_vecs, diff_norm_g[l], mlstm_conv_w[l],
        mlstm_conv_b[l], mlstm_gate_b[l], mlstm_o_b[l], mlstm_norm_g[l], w_out[l], norm_ffn_g[l], router_w[l],
        router_b[l], expert_w_gu[l], expert_b_gu[l], expert_w_down[l], expert_b_down[l])
    out = _combine(x1, o4, top_gate, final_norm_g.reshape(1, D_MODEL))
    return out.reshape(B, S, D_MODEL)
```

```python
import functools
import math

import numpy as np
import jax
import jax.numpy as jnp
from jax import lax
from jax.experimental import pallas as pl
from jax.experimental.pallas import tpu as pltpu
from jax.experimental.pallas import tpu_sc as plsc

F32 = jnp.float32
BF16 = jnp.bfloat16

D_MODEL = 1024
NORM_EPS = 1e-5
DA_HEADS = 4
DA_QK_DIM = 64
DA_V_DIM = 128
DA_WIDTH = DA_HEADS * DA_V_DIM
REL_BUCKETS = 32
REL_MAX_DIST = 128
ML_HEADS = 4
ML_QK_DIM = 64
ML_V_DIM = 128
ML_WIDTH = ML_HEADS * ML_V_DIM
ML_QK_COLS = 2 * ML_HEADS * ML_QK_DIM
ML_CONV = 4
N_EXPERTS = 32
TOP_K = 4
D_FF = D_MODEL
SWIGLU_LIMIT = 7.0
SWIGLU_ALPHA = 1.702

N_MAIN = 3 * DA_WIDTH + ML_QK_COLS + 2 * ML_WIDTH
LANES = 128
SUBLANES = 8
MASK_NEG = -1e30

TM_PROJ = 512
ROUTE_ROWS = 256
TQ = 256
ATT_HEADS = 4
FAR_BLOCKS = 4
ML_CHUNK = 256
MOE_ROWS = 512
TM_COMBINE = 512
ROW_PARTS = 2
PART_W = D_MODEL // (2 * ROW_PARTS)
SC_WINDOW = 128
MOE_GROUP_BOUNDS = (0, 1, 3, 7, 12, 17, 22, 27, 30, 32)
VMEM_LIMIT = 56 * 1024 * 1024


def _bucket_lower_bounds():
    n = np.arange(0, 4 * REL_MAX_DIST)
    max_exact = REL_BUCKETS // 2
    nf = np.maximum(n, 1).astype(np.float32)
    large = max_exact + (np.log(nf / np.float32(max_exact)) / np.float32(math.log(REL_MAX_DIST / max_exact))
                         * np.float32(REL_BUCKETS - max_exact)).astype(np.int32)
    large = np.minimum(large, REL_BUCKETS - 1)
    bucket = np.where(n < max_exact, n, large)
    return [int(np.argmax(bucket >= b)) for b in range(REL_BUCKETS)]


BUCKET_LO = _bucket_lower_bounds()
assert BUCKET_LO[-1] <= TQ, "keys two blocks away must all sit in the last bucket"


def _rms(x, g):
    return x * lax.rsqrt(jnp.mean(x * x, axis=-1, keepdims=True) + NORM_EPS) * g


def _pack_rows(v):
    bits = lax.bitcast_convert_type(v.astype(BF16).astype(F32), jnp.uint32)
    return [bits[:, (2 * j + 1) * PART_W:(2 * j + 2) * PART_W] | (bits[:, 2 * j * PART_W:(2 * j + 1) * PART_W] >> 16)
            for j in range(ROW_PARTS)]


def _unpack_rows(pieces):
    cols = []
    for w in pieces:
        cols.append(lax.bitcast_convert_type(w << 16, F32))
        cols.append(lax.bitcast_convert_type(w & jnp.uint32(0xFFFF0000), F32))
    return jnp.concatenate(cols, axis=1)


def _log_sigmoid(x):
    return jnp.minimum(x, 0.0) - jnp.log1p(jnp.exp(-jnp.abs(x)))


def _inproj_kernel(xp_ref, xn_ref, g_ref, w_ref, wg_ref, o_ref, og_ref, vx_ref, hba_ref, hbb_ref):
    tm = TM_PROJ

    def normed(rows):
        return _rms(rows, g_ref[...]).astype(BF16)

    def project(hb, rows):
        for n in range(N_MAIN // DA_WIDTH):
            cols = slice(n * DA_WIDTH, (n + 1) * DA_WIDTH)
            chunk = jnp.dot(hb, w_ref[:, cols], preferred_element_type=F32).astype(BF16)
            o_ref[rows, cols] = chunk
            if n == 2:
                ones = jnp.ones((chunk.shape[0], LANES), BF16)
                for h in range(DA_HEADS):
                    vx_ref[rows, 2 * h * LANES:(2 * h + 1) * LANES] = chunk[:, h * DA_V_DIM:(h + 1) * DA_V_DIM]
                    vx_ref[rows, (2 * h + 1) * LANES:(2 * h + 2) * LANES] = ones
        og_ref[rows, :] = jnp.dot(hb, wg_ref[...], preferred_element_type=F32)

    @pl.when(pl.program_id(0) == 0)
    def _():
        hba_ref[...] = normed(xp_ref[:tm, :])

    hbb_ref[...] = normed(xp_ref[tm:, :])
    project(hba_ref[...], slice(0, tm))
    hba_ref[...] = normed(xn_ref[...])
    project(hbb_ref[...], slice(tm, 2 * tm))


def _inproj(x2, g, w_main, w_gate):
    T = x2.shape[0]
    n_steps = T // (2 * TM_PROJ)
    pair = lambda w: pl.BlockSpec((2 * TM_PROJ, w), lambda i: (i, 0))
    return pl.pallas_call(
        _inproj_kernel,
        grid=(n_steps,),
        in_specs=[
            pair(D_MODEL),
            pl.BlockSpec((TM_PROJ, D_MODEL), lambda i: (jnp.minimum(2 * i + 2, 2 * n_steps - 1), 0)),
            pl.BlockSpec((1, D_MODEL), lambda i: (0, 0)),
            pl.BlockSpec((D_MODEL, N_MAIN), lambda i: (0, 0)),
            pl.BlockSpec((D_MODEL, 2 * LANES), lambda i: (0, 0)),
        ],
        out_specs=[pair(N_MAIN), pair(2 * LANES), pair(2 * DA_WIDTH)],
        out_shape=[jax.ShapeDtypeStruct((T, N_MAIN), BF16), jax.ShapeDtypeStruct((T, 2 * LANES), F32),
                   jax.ShapeDtypeStruct((T, 2 * DA_WIDTH), BF16)],
        scratch_shapes=[pltpu.VMEM((TM_PROJ, D_MODEL), BF16), pltpu.VMEM((TM_PROJ, D_MODEL), BF16)],
        compiler_params=pltpu.CompilerParams(dimension_semantics=("arbitrary",), vmem_limit_bytes=VMEM_LIMIT),
        name="inproj",
    )(x2, x2, g, w_main, w_gate)


def _relbias_kernel(tbl_ref, o_ref):
    h = pl.program_id(0)
    r = lax.broadcasted_iota(jnp.int32, (2 * TQ, 2 * TQ), 0) & (TQ - 1)
    c = lax.broadcasted_iota(jnp.int32, (2 * TQ, 2 * TQ), 1)
    rel = r + TQ - c
    val = jnp.full((2 * TQ, 2 * TQ), tbl_ref[0, h], F32)
    for b in range(1, REL_BUCKETS):
        val = jnp.where(rel >= BUCKET_LO[b], tbl_ref[b, h], val)
    o_ref[0] = jnp.where(rel >= 0, val - tbl_ref[REL_BUCKETS - 1, h], MASK_NEG)


def _relbias(table):
    return pl.pallas_call(
        _relbias_kernel,
        grid=(DA_HEADS,),
        in_specs=[pl.BlockSpec(memory_space=pltpu.SMEM)],
        out_specs=pl.BlockSpec((1, 2 * TQ, 2 * TQ), lambda h: (h, 0, 0)),
        out_shape=jax.ShapeDtypeStruct((DA_HEADS, 2 * TQ, 2 * TQ), F32),
        compiler_params=pltpu.CompilerParams(dimension_semantics=("parallel",)),
        name="relbias",
    )(table)


def _attn_kernel(lam_init, lam_ref, q_ref, k_ref, v_ref, bias_ref, g_ref, o_ref, acc_ref, m_ref):
    i = pl.program_id(2)
    lane = lax.broadcasted_iota(jnp.int32, (TQ, LANES), 1)
    qqs = []
    for hh in range(ATT_HEADS):
        q = q_ref[:, hh * LANES:(hh + 1) * LANES] * jnp.asarray(DA_QK_DIM ** -0.5, BF16)
        zero = jnp.zeros_like(q)
        qqs.append(jnp.concatenate([jnp.where(lane < DA_QK_DIM, q, zero), jnp.where(lane >= DA_QK_DIM, q, zero)],
                                   axis=0))

    def rows(j, nblk):
        start = j * TQ if isinstance(j, int) else pl.multiple_of(j * TQ, TQ)
        return pl.ds(start, nblk * TQ)

    def update(j, nblk, bias_cols):
        for hh in range(ATT_HEADS):
            hcols = slice(hh * LANES, (hh + 1) * LANES)
            s = lax.dot_general(qqs[hh], k_ref[rows(j, nblk), hcols], (((1,), (1,)), ((), ())),
                                preferred_element_type=F32)
            if bias_cols is not None:
                s = s + bias_ref[hh, :, bias_cols]
            tiles = [s[:, t * LANES:(t + 1) * LANES] for t in range(nblk * TQ // LANES)]
            m_old = m_ref[hh]
            m_new = jnp.maximum(m_old, jnp.max(functools.reduce(jnp.maximum, tiles), axis=-1, keepdims=True))
            alpha = jnp.exp(m_old - m_new)
            p = jnp.concatenate([jnp.exp(t - m_new) for t in tiles], axis=1).astype(BF16)
            pv = jnp.dot(p, v_ref[rows(j, nblk), 2 * hh * LANES:(2 * hh + 2) * LANES], preferred_element_type=F32)
            for half in (slice(0, LANES), slice(LANES, 2 * LANES)):
                acc_ref[hh, :, half] = alpha * acc_ref[hh, :, half] + pv[:, half]
            m_ref[hh] = m_new

    m_ref[...] = jnp.full_like(m_ref, MASK_NEG)
    acc_ref[...] = jnp.zeros_like(acc_ref)

    @pl.when(i == 0)
    def _():
        update(0, 1, slice(TQ, 2 * TQ))

    @pl.when(i >= 1)
    def _():
        update(i - 1, 2, slice(0, 2 * TQ))

    n_far = jnp.maximum(i - 1, 0)
    rem = n_far % FAR_BLOCKS

    @pl.when(rem % 2 == 1)
    def _():
        update(0, 1, None)

    @pl.when(rem >= 2)
    def _():
        update(rem % 2, 2, None)

    def far_step(jj, carry):
        update(rem + FAR_BLOCKS * jj, FAR_BLOCKS, None)
        return carry

    lax.fori_loop(0, n_far // FAR_BLOCKS, far_step, 0)

    lam_v = lam_ref[...]
    lam = (jnp.exp(jnp.sum(lam_v[0:1] * lam_v[1:2], axis=-1, keepdims=True))
           - jnp.exp(jnp.sum(lam_v[2:3] * lam_v[3:4], axis=-1, keepdims=True)) + lam_init)
    for hh in range(ATT_HEADS):
        o = acc_ref[hh, :, :LANES] / acc_ref[hh, :, LANES:]
        out = o[:TQ] - lam * o[TQ:]
        o_ref[:, hh * LANES:(hh + 1) * LANES] = (_rms(out, g_ref[hh]) * (1.0 - lam_init)).astype(BF16)


def _attention(proj, v_ext, bias, lam_vecs, norm_g, B, S, lam_init):
    nq = S // TQ
    hw = ATT_HEADS * LANES
    q_off, k_off = 0, DA_WIDTH // hw
    return pl.pallas_call(
        functools.partial(_attn_kernel, lam_init),
        grid=(B, DA_HEADS // ATT_HEADS, nq),
        in_specs=[
            pl.BlockSpec((4, DA_QK_DIM), lambda b, h, i: (0, 0)),
            pl.BlockSpec((TQ, hw), lambda b, h, i: (b * nq + i, q_off + h)),
            pl.BlockSpec((S, hw), lambda b, h, i: (b, k_off + h)),
            pl.BlockSpec((S, 2 * hw), lambda b, h, i: (b, h)),
            pl.BlockSpec((ATT_HEADS, 2 * TQ, 2 * TQ), lambda b, h, i: (h, 0, 0)),
            pl.BlockSpec((ATT_HEADS, 1, LANES), lambda b, h, i: (h, 0, 0)),
        ],
        out_specs=pl.BlockSpec((TQ, hw), lambda b, h, i: (b * nq + i, h)),
        out_shape=jax.ShapeDtypeStruct((B * S, DA_WIDTH), BF16),
        scratch_shapes=[
            pltpu.VMEM((ATT_HEADS, 2 * TQ, 2 * DA_V_DIM), F32),
            pltpu.VMEM((ATT_HEADS, 2 * TQ, LANES), F32),
        ],
        compiler_params=pltpu.CompilerParams(
            dimension_semantics=("parallel", "parallel", "arbitrary"), vmem_limit_bytes=VMEM_LIMIT),
        name="diff_attention",
    )(lam_vecs, proj, proj, v_ext, bias, norm_g)


def _prefix_scan(x, axis, op, identity):
    n = x.shape[axis]
    idx = lax.broadcasted_iota(jnp.int32, x.shape, axis)
    d = 1
    while d < n:
        x = op(x, jnp.where(idx >= d, pltpu.roll(x, d, axis), identity))
        d *= 2
    return x


def _prefix_sum(x, axis):
    return _prefix_scan(x, axis, jnp.add, 0.0)


def _prefix_max(x, axis):
    return _prefix_scan(x, axis, jnp.maximum, -jnp.inf)


def _mlstm_kernel(qk_ref, v_ref, og_ref, g_ref, gt_ref, cw_ref, cb_ref, gb_ref, gbt_ref, ob_ref, ng_ref,
                  out_ref, prev_ref, cn_ref, m_ref):
    L = ML_CHUNK
    H = ML_HEADS
    QW = H * ML_QK_DIM

    @pl.when(pl.program_id(1) == 0)
    def _():
        prev_ref[...] = jnp.zeros_like(prev_ref)
        cn_ref[...] = jnp.zeros_like(cn_ref)
        m_ref[...] = jnp.zeros_like(m_ref)

    x = qk_ref[...].astype(F32)
    prev = prev_ref[...]
    row = lax.broadcasted_iota(jnp.int32, x.shape, 0)
    y = cb_ref[...] + cw_ref[ML_CONV - 1:ML_CONV, :] * x
    for d in range(1, ML_CONV):
        shifted = pltpu.roll(jnp.where(row >= L - d, prev, x), d, 0)
        y = y + cw_ref[ML_CONV - 1 - d:ML_CONV - d, :] * shifted
    prev_ref[...] = x
    qkc = y * jax.nn.sigmoid(y)
    q = qkc[:, :QW]
    k = qkc[:, QW:] * (ML_QK_DIM ** -0.5)
    kb = k.astype(BF16)

    G = g_ref[...]
    li_all = G[:, :LANES] + gb_ref[:, :LANES]
    b_all = _prefix_sum(_log_sigmoid(G[:, LANES:] + gb_ref[:, LANES:]), 0)
    GT = gt_ref[...] + gbt_ref[...]
    bT = _prefix_sum(_log_sigmoid(GT), 1)
    m_intra_all = b_all + _prefix_max(li_all - b_all, 0)

    qlane = lax.broadcasted_iota(jnp.int32, (L, QW), 1) // ML_QK_DIM
    qz = jnp.concatenate([jnp.where(qlane == h, q, 0.0) for h in range(H)], axis=0).astype(BF16)
    s_all = lax.dot_general(qz, kb, (((1,), (1,)), ((), ())), preferred_element_type=F32)
    cn_old = cn_ref[...]
    qcn = jnp.dot(qz, cn_old.astype(BF16), preferred_element_type=F32)

    t_idx = lax.broadcasted_iota(jnp.int32, (L, L), 0)
    j_idx = lax.broadcasted_iota(jnp.int32, (L, L), 1)
    causal = j_idx <= t_idx
    m_old_row = m_ref[...]
    mlane = lax.broadcasted_iota(jnp.int32, (1, LANES), 1)
    crow = lax.broadcasted_iota(jnp.int32, (QW, 1), 0) // ML_QK_DIM
    m_new_row = m_old_row
    wk_exp = jnp.zeros((L, QW), F32)
    decay_col = jnp.zeros((QW, 1), F32)

    for h in range(H):
        li_c = li_all[:, h:h + 1]
        b_c = b_all[:, h:h + 1]
        li_r = GT[h:h + 1, :]
        b_r = bT[H + h:H + h + 1, :]
        m_prev = m_old_row[:, h:h + 1]

        inter = b_c + m_prev
        m_t = jnp.maximum(m_intra_all[:, h:h + 1], inter)
        w_intra = jnp.exp(jnp.where(causal, (b_c - m_t) + (li_r - b_r), MASK_NEG))
        w_inter = jnp.exp(inter - m_t)
        s = (s_all[h * L:(h + 1) * L, :] * w_intra).astype(BF16)
        vh1 = jnp.concatenate([v_ref[:, h * ML_V_DIM:(h + 1) * ML_V_DIM], jnp.ones((L, LANES), BF16)], axis=1)
        nd = jnp.dot(s, vh1, preferred_element_type=F32) + w_inter * qcn[h * L:(h + 1) * L, :]
        hval = nd[:, :ML_V_DIM] / jnp.maximum(jnp.abs(nd[:, ML_V_DIM:]), jnp.exp(-m_t))

        cols = slice(h * ML_V_DIM, (h + 1) * ML_V_DIM)
        gate = jax.nn.sigmoid(og_ref[:, cols].astype(F32) + ob_ref[:, cols])
        out_ref[:, cols] = (_rms(hval, ng_ref[:, cols]) * gate).astype(BF16)

        g_tot = b_c[L - 1:L, :]
        wlog = g_tot - b_c + li_c
        m_new = jnp.maximum(g_tot + m_prev, jnp.max(wlog, axis=0, keepdims=True))
        decay = jnp.exp(g_tot + m_prev - m_new)
        wk = jnp.exp(wlog - m_new)
        m_new_row = jnp.where(mlane == h, m_new, m_new_row)
        wk_exp = jnp.where(qlane == h, wk, wk_exp)
        decay_col = jnp.where(crow == h, decay, decay_col)

    kw = (k * wk_exp).astype(BF16)
    vext = jnp.concatenate([v_ref[...], jnp.ones((L, LANES), BF16)], axis=1)
    upd = lax.dot_general(kw, vext, (((0,), (0,)), ((), ())), preferred_element_type=F32)
    for h in range(H):
        rows = slice(h * ML_QK_DIM, (h + 1) * ML_QK_DIM)
        dh = decay_col[rows, :]
        cn_ref[rows, :ML_V_DIM] = dh * cn_old[rows, :ML_V_DIM] + upd[rows, h * ML_V_DIM:(h + 1) * ML_V_DIM]
        cn_ref[rows, ML_V_DIM:] = dh * cn_old[rows, ML_V_DIM:] + upd[rows, H * ML_V_DIM:]
    m_ref[...] = m_new_row


def _mlstm(proj, gates, gates_t, conv_w, conv_b, gate_b, gate_bt, o_b, norm_g, B, S):
    L = ML_CHUNK
    nc = S // L
    qk_blk = 3 * DA_WIDTH // ML_QK_COLS
    v_blk, o_blk = qk_blk + 1, qk_blk + 2
    full = lambda shape: pl.BlockSpec(shape, lambda b, c: (0,) * len(shape))
    return pl.pallas_call(
        _mlstm_kernel,
        grid=(B, nc),
        in_specs=[
            pl.BlockSpec((L, ML_QK_COLS), lambda b, c: (b * nc + c, qk_blk)),
            pl.BlockSpec((L, ML_WIDTH), lambda b, c: (b * nc + c, v_blk)),
            pl.BlockSpec((L, ML_WIDTH), lambda b, c: (b * nc + c, o_blk)),
            pl.BlockSpec((L, 2 * LANES), lambda b, c: (b * nc + c, 0)),
            pl.BlockSpec((2 * ML_HEADS, L), lambda b, c: (0, b * nc + c)),
            full((ML_CONV, ML_QK_COLS)),
            full((1, ML_QK_COLS)),
            full((1, 2 * LANES)),
            full((2 * ML_HEADS, 1)),
            full((1, ML_WIDTH)),
            full((1, ML_WIDTH)),
        ],
        out_specs=pl.BlockSpec((L, ML_WIDTH), lambda b, c: (b * nc + c, 0)),
        out_shape=jax.ShapeDtypeStruct((B * S, ML_WIDTH), BF16),
        scratch_shapes=[
            pltpu.VMEM((L, ML_QK_COLS), F32),
            pltpu.VMEM((ML_HEADS * ML_QK_DIM, 2 * ML_V_DIM), F32),
            pltpu.VMEM((1, LANES), F32),
        ],
        compiler_params=pltpu.CompilerParams(
            dimension_semantics=("parallel", "arbitrary"), vmem_limit_bytes=VMEM_LIMIT),
        name="mlstm",
    )(proj, proj, proj, gates, gates_t, conv_w, conv_b, gate_b, gate_bt, o_b, norm_g)


def _outproj_kernel(a_ref, m_ref, x_ref, wo_ref, g_ref, rw_ref, rb_ref, x1_ref, h2_ref, idx_ref, gate_ref):
    x1 = (x_ref[...]
          + jnp.dot(a_ref[...], wo_ref[:DA_WIDTH, :], preferred_element_type=F32)
          + jnp.dot(m_ref[...], wo_ref[DA_WIDTH:, :], preferred_element_type=F32))
    x1_ref[...] = x1
    for c in range(x1.shape[0] // ROUTE_ROWS):
        r = slice(c * ROUTE_ROWS, (c + 1) * ROUTE_ROWS)
        h2 = _rms(x1[r], g_ref[...])
        for j, piece in enumerate(_pack_rows(h2)):
            h2_ref[j, r, :] = piece
        work = jnp.dot(h2.astype(BF16), rw_ref[...], preferred_element_type=F32) + rb_ref[...]
        lane = lax.broadcasted_iota(jnp.int32, work.shape, 1).astype(F32)
        idxs = jnp.zeros_like(work)
        vals = jnp.zeros_like(work)
        top = None
        for kk in range(TOP_K):
            mx = jnp.max(work, axis=-1, keepdims=True)
            am = jnp.min(jnp.where(work == mx, lane, float(LANES)), axis=-1, keepdims=True)
            idxs = jnp.where(lane == kk, am, idxs)
            vals = jnp.where(lane == kk, mx, vals)
            work = jnp.where(lane == am, -jnp.inf, work)
            if kk == 0:
                top = mx
        e = jnp.where(lane < TOP_K, jnp.exp(vals - top), 0.0)
        gate_ref[r, :] = e / jnp.sum(e, axis=-1, keepdims=True)
        idx_ref[:, r] = jnp.transpose(idxs)[:SUBLANES].astype(jnp.int32)


def _outproj(a_out, m_out, x2, w_out, g, router_w, router_b):
    T = x2.shape[0]
    row = lambda w: pl.BlockSpec((TM_PROJ, w), lambda i: (i, 0))
    full = lambda r, c: pl.BlockSpec((r, c), lambda i: (0, 0))
    return pl.pallas_call(
        _outproj_kernel,
        grid=(T // TM_PROJ,),
        in_specs=[row(DA_WIDTH), row(ML_WIDTH), row(D_MODEL), full(D_MODEL, D_MODEL), full(1, D_MODEL),
                  full(D_MODEL, LANES), full(1, LANES)],
        out_specs=[row(D_MODEL), pl.BlockSpec((ROW_PARTS, TM_PROJ, PART_W), lambda i: (0, i, 0)),
                   pl.BlockSpec((SUBLANES, TM_PROJ), lambda i: (0, i)), row(LANES)],
        out_shape=[jax.ShapeDtypeStruct((T, D_MODEL), F32), jax.ShapeDtypeStruct((ROW_PARTS, T, PART_W), jnp.uint32),
                   jax.ShapeDtypeStruct((SUBLANES, T), jnp.int32), jax.ShapeDtypeStruct((T, LANES), F32)],
        compiler_params=pltpu.CompilerParams(dimension_semantics=("parallel",), vmem_limit_bytes=VMEM_LIMIT),
        name="outproj_router",
    )(a_out, m_out, x2, w_out, g, router_w, router_b)


def _sc_mesh():
    return plsc.VectorSubcoreMesh(core_axis_name="core", subcore_axis_name="subcore")


def _sc_gather_rows(x, idx, start, n):
    w0 = start // SC_WINDOW

    @functools.partial(pl.kernel, out_type=jax.ShapeDtypeStruct((n, x.shape[1]), x.dtype), mesh=_sc_mesh())
    def gather_kernel(x_hbm, i_hbm, o_hbm):
        def body(i_vmem, o_vmem):
            pltpu.sync_copy(x_hbm.at[i_vmem.at[0]], o_vmem)

        pltpu.emit_pipeline(
            body,
            grid=(n // SC_WINDOW,),
            in_specs=[pl.BlockSpec((1, SC_WINDOW), lambda i: (0, w0 + i))],
            out_specs=[pl.BlockSpec((SC_WINDOW, x.shape[1]), lambda i: (i, 0))],
            core_axis_name=("core", "subcore"),
            dimension_semantics=(pltpu.PARALLEL,),
        )(i_hbm, o_hbm)

    return gather_kernel(x, idx)


def _sc_scatter_rows(y, idx, start, n_out, out_ref=None):
    n = y.shape[0]
    w0 = start // SC_WINDOW
    out_type = () if out_ref is not None else jax.ShapeDtypeStruct((n_out, y.shape[1]), y.dtype)

    @functools.partial(pl.kernel, out_type=out_type, mesh=_sc_mesh())
    def scatter_kernel(y_hbm, i_hbm, o_hbm):
        def body(y_vmem, i_vmem):
            pltpu.sync_copy(y_vmem, o_hbm.at[i_vmem.at[0]])

        pltpu.emit_pipeline(
            body,
            grid=(n // SC_WINDOW,),
            in_specs=[pl.BlockSpec((SC_WINDOW, y.shape[1]), lambda i: (i, 0)),
                      pl.BlockSpec((1, SC_WINDOW), lambda i: (0, w0 + i))],
            out_specs=[],
            core_axis_name=("core", "subcore"),
            dimension_semantics=(pltpu.PARALLEL,),
        )(y_hbm, i_hbm)

    if out_ref is None:
        return scatter_kernel(y, idx)
    scatter_kernel(y, idx, out_ref)
    return out_ref


def _expert_kernel(first_block, be_ref, nu_ref, x_ref, wgu_ref, bgu_ref, wd_ref, bd_ref, y_ref, wgu_bf, wd_bf):
    step = pl.program_id(0)
    blk = first_block + step

    @pl.when(blk < nu_ref[0])
    def _():
        @pl.when((step == 0) | (be_ref[blk] != be_ref[jnp.maximum(blk - 1, 0)]))
        def _():
            wgu_bf[...] = wgu_ref[0].astype(BF16)
            wd_bf[...] = wd_ref[0].astype(BF16)

        xb = _unpack_rows([x_ref[j] for j in range(ROW_PARTS)]).astype(BF16)
        gu = jnp.dot(xb, wgu_bf[...], preferred_element_type=F32) + bgu_ref[0]
        glu = jnp.minimum(gu[:, :D_FF], SWIGLU_LIMIT)
        lin = jnp.clip(gu[:, D_FF:], -SWIGLU_LIMIT, SWIGLU_LIMIT)
        act = glu * jax.nn.sigmoid(SWIGLU_ALPHA * glu) * (lin + 1.0)
        y = jnp.dot(act.astype(BF16), wd_bf[...], preferred_element_type=F32) + bd_ref[0]
        for j, piece in enumerate(_pack_rows(y)):
            y_ref[j] = piece

    @pl.when(blk >= nu_ref[0])
    def _():
        y_ref[...] = jnp.zeros_like(y_ref)


def _experts(h2, block_expert, n_used, row_tok, row_dst, w_gu, b_gu, w_down, b_down):
    R = MOE_ROWS
    T = h2.shape[1]
    n_rows = row_tok.shape[0]
    n_blocks = n_rows // R
    bounds = sorted({n_blocks * f // MOE_GROUP_BOUNDS[-1] for f in MOE_GROUP_BOUNDS})
    groups = list(zip(bounds[:-1], bounds[1:]))
    part = jnp.arange(ROW_PARTS, dtype=jnp.int32)[:, None]
    h2_flat = h2.reshape(ROW_PARTS * T, PART_W)

    def group_major(per_row, stride):
        return jnp.concatenate([(part * stride + per_row[None, b0 * R:b1 * R]).reshape(-1) for b0, b1 in groups])[None]

    src_idx = group_major(row_tok, T)
    dst_idx = group_major(row_dst, n_rows)
    xs = [_sc_gather_rows(h2_flat, src_idx, ROW_PARTS * b0 * R, ROW_PARTS * (b1 - b0) * R)
          .reshape(ROW_PARTS, (b1 - b0) * R, PART_W) for b0, b1 in groups]

    def run_group(grp):
        first, g_blocks = groups[grp][0], groups[grp][1] - groups[grp][0]
        wspec = lambda shape: pl.BlockSpec(shape, lambda i, be, nu: (be[first + i], 0, 0))
        rows_spec = pl.BlockSpec((ROW_PARTS, R, PART_W), lambda i, be, nu: (0, i, 0))
        return pl.pallas_call(
            functools.partial(_expert_kernel, first),
            grid_spec=pltpu.PrefetchScalarGridSpec(
                num_scalar_prefetch=2,
                grid=(g_blocks,),
                in_specs=[rows_spec, wspec((1, D_MODEL, 2 * D_FF)), wspec((1, 1, 2 * D_FF)),
                          wspec((1, D_FF, D_MODEL)), wspec((1, 1, D_MODEL))],
                out_specs=rows_spec,
                scratch_shapes=[pltpu.VMEM((D_MODEL, 2 * D_FF), BF16), pltpu.VMEM((D_FF, D_MODEL), BF16)],
            ),
            out_shape=jax.ShapeDtypeStruct((ROW_PARTS, g_blocks * R, PART_W), jnp.uint32),
            compiler_params=pltpu.CompilerParams(dimension_semantics=("arbitrary",), vmem_limit_bytes=VMEM_LIMIT),
            name=f"experts_g{grp}",
        )(block_expert, n_used, xs[grp], w_gu, b_gu, w_down, b_down)

    out = None
    for grp, (b0, b1) in enumerate(groups):
        ys = run_group(grp)
        scattered = _sc_scatter_rows(ys.reshape(ROW_PARTS * (b1 - b0) * R, PART_W), dst_idx, ROW_PARTS * b0 * R,
                                     ROW_PARTS * n_rows, out)
        out = jax.new_ref(scattered) if out is None else scattered
    return jax.freeze(out).reshape(ROW_PARTS, n_rows, PART_W)


def _combine_kernel(x1_ref, gate_ref, g_ref, *refs):
    o_refs, out_ref = refs[:TOP_K], refs[TOP_K]
    y = x1_ref[...]
    gt = gate_ref[...]
    for kk in range(TOP_K):
        y = y + gt[:, kk:kk + 1] * _unpack_rows([o_refs[kk][j] for j in range(ROW_PARTS)])
    out_ref[...] = _rms(y, g_ref[...])


def _combine(x1, o4, gates, g):
    T = x1.shape[0]
    tm = TM_COMBINE
    nt = T // tm
    slot_spec = lambda kk: pl.BlockSpec((ROW_PARTS, tm, PART_W), lambda i: (0, kk * nt + i, 0))
    return pl.pallas_call(
        _combine_kernel,
        grid=(nt,),
        in_specs=[
            pl.BlockSpec((tm, D_MODEL), lambda i: (i, 0)),
            pl.BlockSpec((tm, LANES), lambda i: (i, 0)),
            pl.BlockSpec((1, D_MODEL), lambda i: (0, 0)),
        ] + [slot_spec(kk) for kk in range(TOP_K)],
        out_specs=pl.BlockSpec((tm, D_MODEL), lambda i: (i, 0)),
        out_shape=jax.ShapeDtypeStruct((T, D_MODEL), F32),
        compiler_params=pltpu.CompilerParams(dimension_semantics=("parallel",), vmem_limit_bytes=VMEM_LIMIT),
        name="combine_norm",
    )(x1, gates, g, *([o4] * TOP_K))


def _dispatch_plan(idx_t, T):
    R = MOE_ROWS
    P = T * TOP_K
    n_blocks = P // R + N_EXPERTS
    n_rows = n_blocks * R
    key_span = 2 * P
    assert R <= P and N_EXPERTS * key_span < 2 ** 31 - 1
    experts = jnp.arange(N_EXPERTS, dtype=jnp.int32)
    counts = jnp.sum(idx_t[:, :, None] == experts, axis=(0, 1), dtype=jnp.int32)
    padded = (counts + R - 1) // R * R
    pend = jnp.cumsum(padded)
    pair_keys = (idx_t * key_span + jnp.arange(P, dtype=jnp.int32).reshape(TOP_K, T)).reshape(P)
    j = jnp.arange(R, dtype=jnp.int32)[None, :]
    unused = jnp.iinfo(jnp.int32).max
    pad_keys = jnp.where(j < (padded - counts)[:, None], experts[:, None] * key_span + P + j, unused)
    keys = jnp.sort(jnp.concatenate([pair_keys, pad_keys.reshape(-1)]), stable=False)
    val = keys % key_span
    is_valid = (keys != unused) & (val < P)
    row = jnp.arange(n_rows, dtype=jnp.int32)
    row_tok = jnp.where(is_valid, val % T, row % T)
    pad_rank = jnp.cumsum(1 - is_valid.astype(jnp.int32)) - 1
    row_dst = jnp.where(is_valid, val, P + pad_rank)
    block_row0 = jnp.arange(n_blocks, dtype=jnp.int32) * R
    block_expert = jnp.minimum(jnp.sum(pend[None, :] <= block_row0[:, None], axis=1, dtype=jnp.int32), N_EXPERTS - 1)
    n_used = (pend[-1] // R).reshape(1).astype(jnp.int32)
    return block_expert, n_used, row_tok, row_dst


def _layer(x2, B, S, layer, norm_attn_g, w_in, rel_bias_table, lam_vecs, diff_norm_g, conv_w, conv_b, gate_b,
           o_b, ml_norm_g, w_out, norm_ffn_g, router_w, router_b, w_gu, b_gu, w_down, b_down):
    T = B * S
    H = ML_HEADS
    w_main = w_in[:, :N_MAIN].astype(BF16)
    wg = w_in[:, N_MAIN:]
    w_gate = jnp.zeros((D_MODEL, 2 * LANES), F32).at[:, :H].set(wg[:, :H]).at[:, LANES:LANES + H].set(wg[:, H:])
    proj, gates, v_ext = _inproj(x2, norm_attn_g.reshape(1, D_MODEL), w_main, w_gate.astype(BF16))

    lam_init = 0.8 - 0.6 * math.exp(-0.3 * layer)
    bias = _relbias(rel_bias_table.astype(F32))
    a_out = _attention(proj, v_ext, bias, lam_vecs, diff_norm_g.reshape(DA_HEADS, 1, DA_V_DIM), B, S, lam_init)

    gates_t = jnp.concatenate([gates[:, :H], gates[:, LANES:LANES + H]], axis=1).T
    gate_b_pad = jnp.zeros((1, 2 * LANES), F32).at[0, :H].set(gate_b[:H]).at[0, LANES:LANES + H].set(gate_b[H:])
    m_out = _mlstm(proj, gates, gates_t, conv_w, conv_b.reshape(1, -1), gate_b_pad, gate_b.reshape(2 * H, 1),
                   o_b.reshape(1, -1), ml_norm_g.reshape(1, -1), B, S)

    rw = jnp.zeros((D_MODEL, LANES), F32).at[:, :N_EXPERTS].set(router_w).astype(BF16)
    rb = jnp.full((1, LANES), MASK_NEG, F32).at[0, :N_EXPERTS].set(router_b)
    x1, h2, top_idx, top_gate = _outproj(a_out, m_out, x2, w_out.astype(BF16), norm_ffn_g.reshape(1, D_MODEL), rw, rb)

    block_expert, n_used, row_tok, row_dst = _dispatch_plan(top_idx[:TOP_K], T)
    o4 = _experts(h2, block_expert, n_used, row_tok, row_dst, w_gu, b_gu.reshape(N_EXPERTS, 1, 2 * D_FF),
                  w_down, b_down.reshape(N_EXPERTS, 1, D_MODEL))
    return x1, o4, top_gate


def kernel(x, norm_attn_g, w_in, rel_bias_table, lambda_q1, lambda_k1, lambda_q2, lambda_k2, diff_norm_g,
           mlstm_conv_w, mlstm_conv_b, mlstm_gate_b, mlstm_o_b, mlstm_norm_g, w_out, norm_ffn_g, router_w,
           router_b, expert_w_gu, expert_b_gu, expert_w_down, expert_b_down, final_norm_g):
    B, S, _ = x.shape
    depth = w_in.shape[0]
    assert depth == 1, "the combine kernel fuses the last layer's MoE residual with the final norm"
    assert S % TQ == 0 and S % ML_CHUNK == 0 and (B * S) % (2 * TM_PROJ) == 0 and (B * S * TOP_K) % MOE_ROWS == 0
    x2 = x.reshape(B * S, D_MODEL)
    l = 0
    lam_vecs = jnp.stack([lambda_q1[l], lambda_k1[l], lambda_q2[l], lambda_k2[l]])
    x1, o4, top_gate = _layer(
        x2, B, S, l, norm_attn_g[l], w_in[l], rel_bias_table, lam_vecs, diff_norm_g[l], mlstm_conv_w[l],
        mlstm_conv_b[l], mlstm_gate_b[l], mlstm_o_b[l], mlstm_norm_g[l], w_out[l], norm_ffn_g[l], router_w[l],
        router_b[l], expert_w_gu[l], expert_b_gu[l], expert_w_down[l], expert_b_down[l])
    out = _combine(x1, o4, top_gate, final_norm_g.reshape(1, D_MODEL))
    return out.reshape(B, S, D_MODEL)
```
